```python
import jax, jax.numpy as jnp
from jax import lax
import numpy as np

D_MODEL = 1024
BATCH = 16
SEQ = 2048
DEPTH = 4

GRID_W = 64
CTX_LEN = 256
N_MIXERS = 2
N_RWKV = (DEPTH + 1) // 2
N_MLSTM = DEPTH // 2
N_MOD = 6
NORM_EPS = 1e-6

RW_HEAD_DIM = 64
RW_HEADS = D_MODEL // RW_HEAD_DIM
RW_DECAY_LORA = 64
RW_A_LORA = 64
RW_V_LORA = 32
RW_G_LORA = 128
RW_GN_EPS = 64e-5

ML_INNER = 2 * D_MODEL
ML_HEADS = 4
ML_HEAD_DIM = ML_INNER // ML_HEADS
ML_QKV_BLOCK = 4
ML_CHUNK = 64
ML_LN_EPS = 1e-6

N_EXPERTS = 32
TOP_K = 4
D_FF_EXPERT = D_MODEL
SWIGLU_LIMIT = 7.0
SWIGLU_ALPHA = 1.702
MOE_BLOCK = 256

kernel_name = 'hybrid_rwkv7_mlstm_moe_prefix_backbone'

F32 = jnp.float32


def _rmsnorm(x, g):
    xf = x.astype(F32)
    y = xf * lax.rsqrt(jnp.mean(xf * xf, axis=-1, keepdims=True) + NORM_EPS)
    return (y * g.astype(F32)).astype(x.dtype)


def _adaln(x, g, shift, scale):
    return _rmsnorm(x, g) * (1 + scale) + shift


def _qshift_grid(h, rows):
    B, L, C = h.shape
    q = C // 4
    p = jnp.pad(h.reshape(B, rows, L // rows, C), ((0, 0), (1, 1), (1, 1), (0, 0)))
    shifted = jnp.concatenate([
        p[:, 1:-1, :-2, :q],
        p[:, 1:-1, 2:, q:2 * q],
        p[:, :-2, 1:-1, 2 * q:3 * q],
        p[:, 2:, 1:-1, 3 * q:]], axis=-1)
    return shifted.reshape(B, L, C)


def _qshift_seq(h):
    q = h.shape[-1] // 4
    p = jnp.pad(h, ((0, 0), (1, 1), (0, 0)))
    prev, nxt = p[:, :-2], p[:, 2:]
    return jnp.concatenate([prev[..., :q], nxt[..., q:2 * q], prev[..., 2 * q:3 * q], nxt[..., 3 * q:]], axis=-1)


def _rwkv_scan(S0, r, w, k, v, a, b, reverse):
    def step(S, inp):
        r_t, w_t, k_t, v_t, a_t, b_t = inp
        sa = jnp.einsum('bhvk,bhk->bhv', S, a_t)
        S = S * w_t[:, :, None, :] + sa[..., None] * b_t[:, :, None, :] + v_t[..., None] * k_t[:, :, None, :]
        return S, jnp.einsum('bhvk,bhk->bhv', S, r_t)
    xs = tuple(jnp.swapaxes(t, 0, 1) for t in (r, w, k, v, a, b))
    S, ys = lax.scan(step, S0, xs, reverse=reverse)
    return S, jnp.swapaxes(ys, 0, 1)


def _rwkv_features(h, xx, mu, wr, wk, wv, w0, w1, w2, a0, a1, a2, g1, g2, k_k, k_a, vres, v_first):
    B, L, C = h.shape

    def heads(t):
        return t.astype(F32).reshape(B, L, RW_HEADS, RW_HEAD_DIM)

    xr, xw, xk, xv, xa, xg = (h + xx * mu[j] for j in range(6))
    r = xr @ wr
    k = xk @ wk
    v = xv @ wv
    if vres is not None:
        v0, v1, v2 = vres
        v = v + (v_first - v) * jax.nn.sigmoid(v0 + (xv @ v1) @ v2)
    g = jax.nn.sigmoid(xg @ g1) @ g2
    kk = heads(k * k_k)
    kk = kk / jnp.maximum(jnp.sqrt(jnp.sum(kk * kk, axis=-1, keepdims=True)), 1e-12)
    per_dir = []
    for d in range(2):
        logw = -jax.nn.softplus(-(w0[d] + jnp.tanh(xw @ w1[d]) @ w2[d]).astype(F32)) - 0.5
        a = jax.nn.sigmoid(a0[d] + (xa @ a1[d]) @ a2[d])
        per_dir.append((heads(jnp.exp(-jnp.exp(logw))), heads(k * (1 + (a - 1) * k_a)), heads(a)))
    return heads(r), heads(v), v, g, kk, per_dir


def _rwkv_dir(S0, feats, d, reverse):
    r, vh, _, _, kk, per_dir = feats
    decay, kd, a = per_dir[d]
    return _rwkv_scan(S0, r, decay, kd, vh, -kk, kk * a, reverse)


def _rwkv_readout(y, feats, wo, r_k, lnx_g, lnx_b, dtype):
    r, vh, _, g, _, per_dir = feats
    B, L, H, N = y.shape
    mean = jnp.mean(y, axis=-1, keepdims=True)
    var = jnp.mean(jnp.square(y - mean), axis=-1, keepdims=True)
    yn = ((y - mean) * lax.rsqrt(var + RW_GN_EPS)).reshape(B, L, H * N) * lnx_g + lnx_b
    bonus = sum(jnp.sum(r * kd * r_k, axis=-1, keepdims=True) * vh for _, kd, _ in per_dir)
    out = (yn + bonus.reshape(B, L, H * N)) * g.astype(F32)
    return out.astype(dtype) @ wo


def _rwkv_mixer(h_ctx, h_lat, rows, proj, wo, r_k, lnx_g, lnx_b, vres, v_first, ctx_out):
    vf_c = None if v_first is None else v_first[0]
    vf_l = None if v_first is None else v_first[1]
    fc = _rwkv_features(h_ctx, _qshift_seq(h_ctx) - h_ctx, *proj, vres, vf_c)
    fl = _rwkv_features(h_lat, _qshift_grid(h_lat, rows) - h_lat, *proj, vres, vf_l)
    B = h_lat.shape[0]
    S0 = jnp.zeros((B, RW_HEADS, RW_HEAD_DIM, RW_HEAD_DIM), F32)
    ys_c, ys_l = [], []
    for d, rev in enumerate((False, True)):
        S_c, y_c = _rwkv_dir(S0, fc, d, rev)
        _, y_l = _rwkv_dir(S_c, fl, d, rev)
        ys_c.append(y_c)
        ys_l.append(y_l)
    out_l = _rwkv_readout(ys_l[0] + ys_l[1], fl, wo, r_k, lnx_g, lnx_b, h_lat.dtype)
    out_c = _rwkv_readout(ys_c[0] + ys_c[1], fc, wo, r_k, lnx_g, lnx_b, h_ctx.dtype) if ctx_out else None
    return out_c, out_l, (fc[2], fl[2])


def _headwise(x, w):
    B, L, C = x.shape
    xb = x.reshape(B, L, C // ML_QKV_BLOCK, ML_QKV_BLOCK)
    return jnp.einsum('blgi,gio->blgo', xb, w).reshape(B, L, C)


def _mlstm_features(h, rows, w_up, conv_w, conv_b, wq, wk, wv, wi, bi, wf, bf):
    B, L, _ = h.shape
    up = h @ w_up
    xm, z = up[..., :ML_INNER], up[..., ML_INNER:]
    img = xm.reshape(B, rows, L // rows, ML_INNER)
    xc = lax.conv_general_dilated(img, conv_w[:, :, None, :], (1, 1), 'SAME',
                                  dimension_numbers=('NHWC', 'HWIO', 'NHWC'),
                                  feature_group_count=ML_INNER)
    xc = jax.nn.silu(xc.reshape(B, L, ML_INNER) + conv_b)
    q = _headwise(xc, wq)
    k = _headwise(xc, wk)
    v = _headwise(xm, wv)

    def gate(w, b):
        pre = q @ w[:ML_INNER] + k @ w[ML_INNER:2 * ML_INNER] + v @ w[2 * ML_INNER:] + b
        return jnp.transpose(pre.astype(F32), (0, 2, 1))

    ig = [gate(wi[d], bi[d]) for d in range(2)]
    lf = [jax.nn.log_sigmoid(gate(wf[d], bf[d])) for d in range(2)]

    def heads(t):
        return jnp.transpose(t.astype(F32).reshape(B, L, ML_HEADS, ML_HEAD_DIM), (0, 2, 1, 3))

    return heads(q), heads(k) * (ML_HEAD_DIM ** -0.5), heads(v), ig, lf, xc, z


def _mlstm_scan(state, q, k, v, ig, lf, reverse):
    if reverse:
        q, k, v, ig, lf = (jnp.flip(t, axis=2) for t in (q, k, v, ig, lf))
    B, H, L, Dh = q.shape
    nc = L // ML_CHUNK
    mask = jnp.tril(jnp.ones((ML_CHUNK, ML_CHUNK), dtype=bool))

    def chunks(t):
        return jnp.moveaxis(t.reshape(B, H, nc, ML_CHUNK, *t.shape[3:]), 2, 0)

    def body(carry, inp):
        C, n, m = carry
        qc, kc, vc, ic, fc = inp
        b = jnp.cumsum(fc, axis=-1)
        log_inter = b + m[..., None]
        log_intra = jnp.where(mask, b[..., :, None] - b[..., None, :] + ic[..., None, :], -jnp.inf)
        m_t = jnp.maximum(log_inter, jnp.max(log_intra, axis=-1))
        w_intra = jnp.exp(log_intra - m_t[..., None])
        w_inter = jnp.exp(log_inter - m_t)
        s = jnp.einsum('bhtd,bhsd->bhts', qc, kc) * w_intra
        num = jnp.einsum('bhts,bhsd->bhtd', s, vc) + w_inter[..., None] * jnp.einsum('bhvk,bhtk->bhtv', C, qc)
        den = jnp.sum(s, axis=-1) + w_inter * jnp.einsum('bhk,bhtk->bht', n, qc)
        h = num / jnp.maximum(jnp.abs(den), jnp.exp(-m_t))[..., None]
        g = b[..., -1]
        log_loc = g[..., None] - b + ic
        m_new = jnp.maximum(g + m, jnp.max(log_loc, axis=-1))
        w_loc = jnp.exp(log_loc - m_new[..., None])
        w_old = jnp.exp(g + m - m_new)
        C = w_old[..., None, None] * C + jnp.einsum('bhsv,bhsk->bhvk', vc * w_loc[..., None], kc)
        n = w_old[..., None] * n + jnp.einsum('bhs,bhsk->bhk', w_loc, kc)
        return (C, n, m_new), h

    state, hs = lax.scan(body, state, tuple(chunks(t) for t in (q, k, v, ig, lf)))
    hs = jnp.moveaxis(hs, 0, 2).reshape(B, H, L, Dh)
    if reverse:
        hs = jnp.flip(hs, axis=2)
    return state, hs


def _mlstm_readout(hs, feats, norm_g, skip, w_down, dtype):
    xc, z = feats[5], feats[6]
    B, H, L, Dh = hs.shape
    mean = jnp.mean(hs, axis=-1, keepdims=True)
    var = jnp.mean(jnp.square(hs - mean), axis=-1, keepdims=True)
    hn = jnp.transpose((hs - mean) * lax.rsqrt(var + ML_LN_EPS), (0, 2, 1, 3)).reshape(B, L, H * Dh) * norm_g
    y = (hn + skip * xc.astype(F32)) * jax.nn.silu(z.astype(F32))
    return y.astype(dtype) @ w_down


def _mlstm_mixer(h_ctx, h_lat, rows, feat_params, norm_g, skip, w_down, ctx_out):
    fc = _mlstm_features(h_ctx, 1, *feat_params)
    fl = _mlstm_features(h_lat, rows, *feat_params)
    B = h_lat.shape[0]
    zero = (jnp.zeros((B, ML_HEADS, ML_HEAD_DIM, ML_HEAD_DIM), F32),
            jnp.zeros((B, ML_HEADS, ML_HEAD_DIM), F32),
            jnp.zeros((B, ML_HEADS), F32))
    hs_c, hs_l = [], []
    for d, rev in enumerate((False, True)):
        st, h_c = _mlstm_scan(zero, fc[0], fc[1], fc[2], fc[3][d], fc[4][d], rev)
        _, h_l = _mlstm_scan(st, fl[0], fl[1], fl[2], fl[3][d], fl[4][d], rev)
        hs_c.append(h_c)
        hs_l.append(h_l)
    out_l = _mlstm_readout(hs_l[0] + hs_l[1], fl, norm_g, skip, w_down, h_lat.dtype)
    out_c = _mlstm_readout(hs_c[0] + hs_c[1], fc, norm_g, skip, w_down, h_ctx.dtype) if ctx_out else None
    return out_c, out_l


def _moe(h, wr, br, w_gu, b_gu, w_d, b_d):
    shp = h.shape
    xt = h.reshape(-1, D_MODEL)
    T = xt.shape[0]
    logits = xt.astype(F32) @ wr.astype(F32) + br.astype(F32)
    top_v, top_i = lax.top_k(logits, TOP_K)
    gates = jax.nn.softmax(top_v, axis=-1)
    A = T * TOP_K
    flat_e = top_i.reshape(A)
    flat_tok = jnp.arange(A, dtype=jnp.int32) // TOP_K
    flat_gate = gates.reshape(A)
    order = jnp.argsort(flat_e)
    se = flat_e[order]
    counts = jnp.bincount(flat_e, length=N_EXPERTS)
    padded = (counts + MOE_BLOCK - 1) // MOE_BLOCK * MOE_BLOCK
    pad_end = jnp.cumsum(padded)
    pad_start = pad_end - padded
    start = jnp.cumsum(counts) - counts
    dest = pad_start[se] + jnp.arange(A, dtype=jnp.int32) - start[se]
    nb = (A + MOE_BLOCK - 1) // MOE_BLOCK + N_EXPERTS
    slot_tok = jnp.full((nb * MOE_BLOCK,), T, jnp.int32).at[dest].set(flat_tok[order])
    slot_gate = jnp.zeros((nb * MOE_BLOCK,), F32).at[dest].set(flat_gate[order])
    block_e = jnp.minimum(jnp.searchsorted(pad_end, jnp.arange(nb) * MOE_BLOCK, side='right'), N_EXPERTS - 1)
    xpad = jnp.concatenate([xt, jnp.zeros((1, D_MODEL), xt.dtype)], axis=0)

    def block(args):
        tok, e = args
        gu = xpad[tok] @ w_gu[e] + b_gu[e]
        gate = jnp.minimum(gu[:, :D_FF_EXPERT], SWIGLU_LIMIT)
        up = jnp.clip(gu[:, D_FF_EXPERT:], -SWIGLU_LIMIT, SWIGLU_LIMIT)
        act = gate * jax.nn.sigmoid(SWIGLU_ALPHA * gate) * (up + 1)
        return act @ w_d[e] + b_d[e]

    yb = lax.map(block, (slot_tok.reshape(nb, MOE_BLOCK), block_e))
    y = jnp.zeros((T + 1, D_MODEL), F32).at[slot_tok].add(yb.reshape(-1, D_MODEL).astype(F32) * slot_gate[:, None])
    return y[:T].astype(h.dtype).reshape(shp)


def setup_inputs(seed: int = 0) -> dict:
    key = jax.random.key(seed)
    keys = iter(jax.random.split(key, 64))
    D, I, H = D_MODEL, ML_INNER, ML_HEADS
    NR, NM, E, F = N_RWKV, N_MLSTM, N_EXPERTS, D_FF_EXPERT

    def nrm(shape, scale):
        return jax.random.normal(next(keys), shape, F32) * scale

    def unif(shape, lo, hi):
        return jax.random.uniform(next(keys), shape, F32, lo, hi)

    return {
        'x': nrm((BATCH, SEQ, D), 1.0),
        'c': nrm((BATCH, D), 1.0),
        'ctx': nrm((BATCH, CTX_LEN, D), 1.0),
        'c_ctx': nrm((D,), 1.0),
        'ada_w': nrm((DEPTH, D, N_MOD * D), 0.5 * D ** -0.5),
        'ada_b': nrm((DEPTH, N_MOD * D), 0.02),
        'norm1_g': 1.0 + nrm((DEPTH, D), 0.02),
        'norm2_g': 1.0 + nrm((DEPTH, D), 0.02),
        'final_g': 1.0 + nrm((D,), 0.02),
        'rw_mu': unif((NR, 6, D), 0.0, 1.0),
        'rw_wr': nrm((NR, D, D), D ** -0.5),
        'rw_wk': nrm((NR, D, D), D ** -0.5),
        'rw_wv': nrm((NR, D, D), D ** -0.5),
        'rw_wo': nrm((NR, D, D), D ** -0.5),
        'rw_w0': unif((NR, 2, D), -6.0, -1.0),
        'rw_w1': nrm((NR, 2, D, RW_DECAY_LORA), D ** -0.5),
        'rw_w2': nrm((NR, 2, RW_DECAY_LORA, D), 0.1 * RW_DECAY_LORA ** -0.5),
        'rw_a0': nrm((NR, 2, D), 0.1),
        'rw_a1': nrm((NR, 2, D, RW_A_LORA), D ** -0.5),
        'rw_a2': nrm((NR, 2, RW_A_LORA, D), 0.5 * RW_A_LORA ** -0.5),
        'rw_g1': nrm((NR, D, RW_G_LORA), D ** -0.5),
        'rw_g2': nrm((NR, RW_G_LORA, D), RW_G_LORA ** -0.5),
        'rw_kk': 0.85 + nrm((NR, D), 0.02),
        'rw_ka': 1.0 + nrm((NR, D), 0.02),
        'rw_rk': nrm((NR, RW_HEADS, RW_HEAD_DIM), 0.1),
        'rw_lnx_g': 1.0 + nrm((NR, D), 0.02),
        'rw_lnx_b': nrm((NR, D), 0.02),
        'rw_v0': 1.0 + nrm((NR - 1, D), 0.02),
        'rw_v1': nrm((NR - 1, D, RW_V_LORA), D ** -0.5),
        'rw_v2': nrm((NR - 1, RW_V_LORA, D), 0.1 * RW_V_LORA ** -0.5),
        'ml_w_up': nrm((NM, D, 2 * I), D ** -0.5),
        'ml_conv_w': nrm((NM, 3, 3, I), 1.0 / 3.0),
        'ml_conv_b': nrm((NM, I), 0.02),
        'ml_wq': nrm((NM, I // ML_QKV_BLOCK, ML_QKV_BLOCK, ML_QKV_BLOCK), ML_QKV_BLOCK ** -0.5),
        'ml_wk': nrm((NM, I // ML_QKV_BLOCK, ML_QKV_BLOCK, ML_QKV_BLOCK), ML_QKV_BLOCK ** -0.5),
        'ml_wv': nrm((NM, I // ML_QKV_BLOCK, ML_QKV_BLOCK, ML_QKV_BLOCK), ML_QKV_BLOCK ** -0.5),
        'ml_wi': nrm((NM, 2, 3 * I, H), (3 * I) ** -0.5),
        'ml_bi': nrm((NM, 2, H), 0.1),
        'ml_wf': nrm((NM, 2, 3 * I, H), (3 * I) ** -0.5),
        'ml_bf': jnp.linspace(3.0, 6.0, H, dtype=F32) + nrm((NM, 2, H), 0.1),
        'ml_norm_g': 1.0 + nrm((NM, I), 0.02),
        'ml_skip': 1.0 + nrm((NM, I), 0.02),
        'ml_w_down': nrm((NM, I, D), I ** -0.5),
        'moe_wr': nrm((DEPTH, D, E), D ** -0.5),
        'moe_br': nrm((DEPTH, E), 0.01),
        'moe_w_gu': nrm((DEPTH, E, D, 2 * F), D ** -0.5),
        'moe_b_gu': nrm((DEPTH, E, 2 * F), 0.02),
        'moe_w_d': nrm((DEPTH, E, F, D), F ** -0.5),
        'moe_b_d': nrm((DEPTH, E, D), 0.02),
    }


def reference(x, c, ctx, c_ctx, ada_w, ada_b, norm1_g, norm2_g, final_g,
              rw_mu, rw_wr, rw_wk, rw_wv, rw_wo, rw_w0, rw_w1, rw_w2, rw_a0, rw_a1, rw_a2,
              rw_g1, rw_g2, rw_kk, rw_ka, rw_rk, rw_lnx_g, rw_lnx_b, rw_v0, rw_v1, rw_v2,
              ml_w_up, ml_conv_w, ml_conv_b, ml_wq, ml_wk, ml_wv, ml_wi, ml_bi, ml_wf, ml_bf,
              ml_norm_g, ml_skip, ml_w_down,
              moe_wr, moe_br, moe_w_gu, moe_b_gu, moe_w_d, moe_b_d):
    rows = x.shape[1] // GRID_W
    s_lat = jax.nn.silu(c)
    s_ctx = jax.nn.silu(c_ctx)
    x_lat, x_ctx = x, ctx
    v_first = None
    for i in range(DEPTH):
        last = i == DEPTH - 1
        m_lat = jnp.split((s_lat @ ada_w[i] + ada_b[i])[:, None, :], N_MOD, axis=-1)
        m_ctx = jnp.split(s_ctx @ ada_w[i] + ada_b[i], N_MOD, axis=-1)
        h_lat = _adaln(x_lat, norm1_g[i], m_lat[0], m_lat[1])
        h_ctx = _adaln(x_ctx, norm1_g[i], m_ctx[0], m_ctx[1])
        j = i // N_MIXERS
        if i % N_MIXERS == 0:
            vres = None if j == 0 else (rw_v0[j - 1], rw_v1[j - 1], rw_v2[j - 1])
            proj = (rw_mu[j], rw_wr[j], rw_wk[j], rw_wv[j], rw_w0[j], rw_w1[j], rw_w2[j],
                    rw_a0[j], rw_a1[j], rw_a2[j], rw_g1[j], rw_g2[j], rw_kk[j], rw_ka[j])
            y_ctx, y_lat, v_cur = _rwkv_mixer(h_ctx, h_lat, rows, proj, rw_wo[j], rw_rk[j],
                                              rw_lnx_g[j], rw_lnx_b[j], vres, v_first, not last)
            if j == 0:
                v_first = v_cur
        else:
            feat = (ml_w_up[j], ml_conv_w[j], ml_conv_b[j], ml_wq[j], ml_wk[j], ml_wv[j],
                    ml_wi[j], ml_bi[j], ml_wf[j], ml_bf[j])
            y_ctx, y_lat = _mlstm_mixer(h_ctx, h_lat, rows, feat, ml_norm_g[j], ml_skip[j], ml_w_down[j], not last)
        x_lat = x_lat + m_lat[2] * y_lat
        moe_p = (moe_wr[i], moe_br[i], moe_w_gu[i], moe_b_gu[i], moe_w_d[i], moe_b_d[i])
        h2_lat = _adaln(x_lat, norm2_g[i], m_lat[3], m_lat[4])
        if last:
            x_lat = x_lat + m_lat[5] * _moe(h2_lat, *moe_p)
        else:
            x_ctx = x_ctx + m_ctx[2] * y_ctx
            h2_ctx = _adaln(x_ctx, norm2_g[i], m_ctx[3], m_ctx[4])
            n_ctx = x_ctx.shape[1]
            f = _moe(jnp.concatenate([h2_ctx, h2_lat], axis=1), *moe_p)
            x_ctx = x_ctx + m_ctx[5] * f[:, :n_ctx]
            x_lat = x_lat + m_lat[5] * f[:, n_ctx:]
    return _rmsnorm(x_lat, final_g)
```

```python
import functools
import math

import jax
import jax.numpy as jnp
from jax import lax
from jax.experimental import pallas as pl
from jax.experimental.pallas import tpu as pltpu

F32 = jnp.float32
BF16 = jnp.bfloat16
I32 = jnp.int32
HI = lax.Precision.HIGHEST

GRID_W = 64
N_MOD = 6
NORM_EPS = 1e-6
RW_HEAD_DIM = 64
RW_GN_EPS = 64e-5
RW_CHUNK = 64
ML_HEADS = 4
ML_QKV_BLOCK = 4
ML_LN_EPS = 1e-6
TOP_K = 4
SWIGLU_LIMIT = 7.0
SWIGLU_ALPHA = 1.702
MOE_BLOCK = 256
LANES = 128
MAX_TIME_BLOCK = 256
VMEM_LIMIT = 56 * 1024 * 1024


def _dot(a, b, precision=None):
    return jnp.dot(a, b, preferred_element_type=F32, precision=precision)


def _dot_nt(a, b):
    return lax.dot_general(a, b, (((1,), (1,)), ((), ())), preferred_element_type=F32)


def _dot_tn(a, b, precision=None):
    return lax.dot_general(a, b, (((0,), (0,)), ((), ())), preferred_element_type=F32,
                           precision=precision)


def _sigmoid(x):
    return 1.0 / (1.0 + jnp.exp(-x))


def _softplus(x):
    return jnp.maximum(x, 0.0) + jnp.log(1.0 + jnp.exp(-jnp.abs(x)))


def _silu(x):
    return x * _sigmoid(x)


def _adaln(x, g, shift, scale):
    y = x * lax.rsqrt(jnp.mean(x * x, axis=-1, keepdims=True) + NORM_EPS) * g
    return y * (1.0 + scale) + shift


def _const_spec(shape):
    nd = len(shape)
    return pl.BlockSpec(shape, lambda *_: (0,) * nd, pipeline_mode=pl.Buffered(1))


def _params(sem, vmem=VMEM_LIMIT):
    return pltpu.CompilerParams(dimension_semantics=sem, vmem_limit_bytes=vmem)


def _time_block_order(d, c, n_blocks, n_ctx_blocks):
    rev = jnp.where(c < n_ctx_blocks, n_ctx_blocks - 1 - c, n_blocks - 1 - (c - n_ctx_blocks))
    return jnp.where(d == 0, c, rev)


def _mod_kernel(c_ref, w_ref, b_ref, o_ref):
    o_ref[0] = _dot(_silu(c_ref[...]), w_ref[0], HI) + b_ref[0]


def _modulation(c, c_ctx, ada_w, ada_b):
    depth, d_model, _ = ada_w.shape
    batch = c.shape[0]
    rows = -(-(batch + 1) // 8) * 8
    cc = jnp.concatenate([c, c_ctx[None], jnp.zeros((rows - batch - 1, d_model), F32)], axis=0)
    out = pl.pallas_call(
        _mod_kernel,
        out_shape=jax.ShapeDtypeStruct((depth, rows, N_MOD * d_model), F32),
        grid=(depth, N_MOD),
        in_specs=[
            pl.BlockSpec((rows, d_model), lambda l, j: (0, 0)),
            pl.BlockSpec((1, d_model, d_model), lambda l, j: (l, 0, j)),
            pl.BlockSpec((1, 1, d_model), lambda l, j: (l, 0, j)),
        ],
        out_specs=pl.BlockSpec((1, rows, d_model), lambda l, j: (l, 0, j)),
        compiler_params=_params(("parallel", "parallel")),
        name="adaln_modulation",
    )(cc, ada_w, ada_b.reshape(depth, 1, N_MOD * d_model))
    return out.reshape(depth, rows, N_MOD, d_model)


def _neighbours(h, h_prev, h_next, is_ctx, first_lat, last_lat):
    tb = h.shape[0]
    row = lax.broadcasted_iota(I32, (tb, 1), 0)
    col = jnp.bitwise_and(row, GRID_W - 1)
    left_ok = jnp.where(is_ctx, row, col) != 0
    right_ok = jnp.where(is_ctx, row - (tb - 1), col - (GRID_W - 1)) != 0
    left = jnp.where(left_ok, pltpu.roll(h, 1, axis=0), 0.0)
    right = jnp.where(right_ok, pltpu.roll(h, tb - 1, axis=0), 0.0)
    up = jnp.concatenate([h_prev, h[:tb - GRID_W]], axis=0)
    down = jnp.concatenate([h[GRID_W:], h_next], axis=0)
    up_ok = jnp.where(first_lat, row, tb) >= GRID_W
    down_ok = jnp.where(last_lat, row, 0) < tb - GRID_W
    up = jnp.where(up_ok, up, 0.0)
    down = jnp.where(down_ok, down, 0.0)
    return left, right, up, down


def _segment_matrices(d_model, seg):
    ch = jnp.arange(d_model)[:, None] // seg
    e = (ch == jnp.arange(LANES)[None, :]).astype(F32)
    return e, e.T


def _route_top_k(logits):
    n, n_exp = logits.shape
    lane = lax.broadcasted_iota(I32, (n, n_exp), 1).astype(F32)
    lane_k = lax.broadcasted_iota(I32, (n, TOP_K), 1)
    work = logits
    vals, idxs = [], []
    for _ in range(TOP_K):
        m = jnp.max(work, axis=-1, keepdims=True)
        idx = jnp.min(jnp.where(work == m, lane, float(n_exp)), axis=-1, keepdims=True)
        vals.append(m)
        idxs.append(idx)
        work = jnp.where(lane == idx, -jnp.inf, work)
    es = [jnp.exp(v - vals[0]) for v in vals]
    den = es[0] + es[1] + es[2] + es[3]
    top_i = jnp.zeros((n, TOP_K), I32)
    gates = jnp.zeros((n, TOP_K), F32)
    for j in range(TOP_K):
        top_i = jnp.where(lane_k == j, idxs[j].astype(I32), top_i)
        gates = jnp.where(lane_k == j, es[j] / den, gates)
    return top_i, gates


def _router_params(moe_wr, moe_br):
    n_exp = moe_wr.shape[-1]
    assert n_exp <= LANES
    wr = jnp.pad(moe_wr, ((0, 0), (0, LANES - n_exp)))
    br = jnp.pad(moe_br, (0, LANES - n_exp), constant_values=-1e30)
    return wr, br.reshape(1, LANES)


def _post_mixer(x, y, mod, g2, wr, br, xo_ref, h2_ref, ti_ref, gt_ref):
    xn = x + mod[2:3] * y
    xo_ref[0] = xn
    h2 = _adaln(xn, g2, mod[3:4], mod[4:5])
    h2_ref[0] = h2
    top_i, gates = _route_top_k(_dot(h2, wr, HI) + br)
    ti_ref[0] = top_i
    gt_ref[0] = gates


def _rwkv_pre_kernel(*refs, has_vres, n_ctx_blocks):
    (x_ref, xp_ref, xn_ref, mod_ref, g_ref, mu_ref, wr_ref, wk_ref, wv_ref, wlw_ref, wla_ref,
     wg1_ref, w2_ref, a2_ref, g2_ref, w0_ref, a0_ref, kkp_ref, kap_ref, e_ref, et_ref) = refs[:21]
    if has_vres:
        vf_ref, v0_ref, v1_ref, v2_ref = refs[21:25]
        outs = refs[25:]
    else:
        outs = refs[21:]
    r_ref, v_ref, gg_ref, kk_ref, lw_ref, kd_ref, bb_ref = outs

    t = pl.program_id(1)
    n_blocks = pl.num_programs(1)
    is_ctx = t < n_ctx_blocks
    mod = mod_ref[0]
    g = g_ref[...]
    h = _adaln(x_ref[0], g, mod[0:1], mod[1:2])
    hp = _adaln(xp_ref[0], g, mod[0:1], mod[1:2])
    hn = _adaln(xn_ref[0], g, mod[0:1], mod[1:2])
    left, right, up, down = _neighbours(h, hp, hn, is_ctx, t == n_ctx_blocks, t == n_blocks - 1)
    d_model = h.shape[1]
    q = d_model // 4
    lane = lax.broadcasted_iota(I32, (1, d_model), 1)
    q2 = jnp.where(is_ctx, left, up)
    q3 = jnp.where(is_ctx, right, down)
    shifted = jnp.where(lane < q, left, jnp.where(lane < 2 * q, right, jnp.where(lane < 3 * q, q2, q3)))
    xx = shifted - h
    mu = mu_ref[...]

    def mix(j):
        return (h + xx * mu[j:j + 1]).astype(BF16)

    r = _dot(mix(0), wr_ref[...])
    k = _dot(mix(2), wk_ref[...])
    xv = mix(3)
    v = _dot(xv, wv_ref[...])
    if has_vres:
        lora = _dot(_dot(xv, v1_ref[...]).astype(BF16), v2_ref[...])
        v = v + (vf_ref[0] - v) * _sigmoid(v0_ref[...] + lora)
    gate = _dot(_sigmoid(_dot(mix(5), wg1_ref[...])).astype(BF16), g2_ref[...])
    kx = k * kkp_ref[...]
    ss = _dot(kx * kx, e_ref[...], HI)
    inv = 1.0 / jnp.maximum(jnp.sqrt(ss), 1e-12)
    kk = kx * _dot(inv, et_ref[...], HI)
    r_ref[0] = r
    v_ref[0] = v
    gg_ref[0] = gate
    kk_ref[0] = kk
    tw = jnp.tanh(_dot(mix(1), wlw_ref[...])).astype(BF16)
    la = _dot(mix(4), wla_ref[...]).astype(BF16)
    kap = kap_ref[...]
    for d in range(2):
        logw = -_softplus(-(w0_ref[d:d + 1] + _dot(tw, w2_ref[d]))) - 0.5
        lw_ref[d, 0] = -jnp.exp(logw)
        a = _sigmoid(a0_ref[d:d + 1] + _dot(la, a2_ref[d]))
        kd_ref[d, 0] = k * (1.0 + (a - 1.0) * kap)
        bb_ref[d, 0] = kk * a


def _rwkv_pre(x, mod_l, norm_g, p, vres, v_first, tb, n_ctx_blocks):
    batch, t_len, d_model = x.shape
    n_blocks = t_len // tb
    hb = tb // GRID_W
    n_halo = t_len // GRID_W
    lora = p["w1"].shape[-1]
    assert 2 * lora == LANES and p["g1"].shape[-1] == LANES
    zeros = jnp.zeros((lora, d_model), F32)
    w2p = jnp.stack([jnp.concatenate([p["w2"][0], zeros]), jnp.concatenate([zeros, p["w2"][1]])])
    a2p = jnp.stack([jnp.concatenate([p["a2"][0], zeros]), jnp.concatenate([zeros, p["a2"][1]])])
    seg_e, seg_et = _segment_matrices(d_model, RW_HEAD_DIM)
    has_vres = vres is not None

    tok = pl.BlockSpec((1, tb, d_model), lambda b, t: (b, t, 0))
    tok2 = pl.BlockSpec((2, 1, tb, d_model), lambda b, t: (0, b, t, 0))
    in_specs = [
        tok,
        pl.BlockSpec((1, GRID_W, d_model), lambda b, t: (b, jnp.maximum(t * hb - 1, 0), 0)),
        pl.BlockSpec((1, GRID_W, d_model), lambda b, t: (b, jnp.minimum((t + 1) * hb, n_halo - 1), 0)),
        pl.BlockSpec((1, N_MOD, d_model), lambda b, t: (jnp.where(t < n_ctx_blocks, batch, b), 0, 0)),
    ]
    args = [x, x, x, mod_l]
    consts = [
        norm_g.reshape(1, d_model), p["mu"],
        p["wr"].astype(BF16), p["wk"].astype(BF16), p["wv"].astype(BF16),
        jnp.concatenate([p["w1"][0], p["w1"][1]], axis=1).astype(BF16),
        jnp.concatenate([p["a1"][0], p["a1"][1]], axis=1).astype(BF16),
        p["g1"].astype(BF16), w2p.astype(BF16), a2p.astype(BF16), p["g2"].astype(BF16),
        p["w0"], p["a0"], p["kk"].reshape(1, d_model), p["ka"].reshape(1, d_model), seg_e, seg_et,
    ]
    in_specs += [_const_spec(c.shape) for c in consts]
    args += consts
    if has_vres:
        v0, v1, v2 = vres
        extra = [v0.reshape(1, d_model), v1.astype(BF16), v2.astype(BF16)]
        in_specs += [tok] + [_const_spec(c.shape) for c in extra]
        args += [v_first] + extra
    one = jax.ShapeDtypeStruct((batch, t_len, d_model), F32)
    two = jax.ShapeDtypeStruct((2, batch, t_len, d_model), F32)
    return pl.pallas_call(
        functools.partial(_rwkv_pre_kernel, has_vres=has_vres, n_ctx_blocks=n_ctx_blocks),
        out_shape=[one, one, one, one, two, two, two],
        grid=(batch, n_blocks),
        in_specs=in_specs,
        out_specs=[tok, tok, tok, tok, tok2, tok2, tok2],
        compiler_params=_params(("parallel", "parallel")),
        name="rwkv_features",
    )(*args)


def _rwkv_scan_kernel(r_ref, v_ref, kk_ref, lw_ref, kd_ref, bb_ref, y_ref, s_ref, *, n_pairs, tb):
    d = pl.program_id(0)
    c = pl.program_id(3)
    ch = RW_CHUNK
    two = 2 * ch

    @pl.when(c == 0)
    def _():
        s_ref[...] = jnp.zeros_like(s_ref)

    sgn = jnp.where(d == 0, 1, -1)
    ri = lax.broadcasted_iota(I32, (ch, ch), 0)
    ci = lax.broadcasted_iota(I32, (ch, ch), 1)
    tri = jnp.where((ci - ri) * sgn <= 0, 1.0, 0.0).astype(F32)
    r2 = lax.broadcasted_iota(I32, (two, two), 0)
    c2 = lax.broadcasted_iota(I32, (two, two), 1)
    same = jnp.right_shift(r2, 6) == jnp.right_shift(c2, 6)
    delta = (jnp.bitwise_and(c2, ch - 1) - jnp.bitwise_and(r2, ch - 1)) * sgn
    before = jnp.logical_and(same, delta < 0)
    before_eq = jnp.logical_and(same, delta <= 0)
    head0 = lax.broadcasted_iota(I32, (ch, LANES), 1) < RW_HEAD_DIM
    n_sub = tb // ch

    def sub_chunk(j, carry):
        jj = jnp.where(d == 0, j, n_sub - 1 - j)
        rows = pl.ds(pl.multiple_of(jj * ch, ch), ch)
        for p in range(n_pairs):
            cols = slice(p * LANES, (p + 1) * LANES)
            r = r_ref[0, rows, cols]
            v = v_ref[0, rows, cols]
            kk = kk_ref[0, rows, cols]
            lw = lw_ref[0, 0, rows, cols]
            kd = kd_ref[0, 0, rows, cols]
            bb = bb_ref[0, 0, rows, cols]
            s = s_ref[p]
            lp = _dot(tri, lw, HI)
            e_in = jnp.exp(lp)
            e_out = jnp.exp(-lp)
            a_t = -kk * jnp.exp(lp - lw)
            r_t = r * e_in
            b_t = bb * e_out
            k_t = kd * e_out
            lhs = jnp.concatenate([jnp.where(head0, a_t, 0.0), jnp.where(head0, 0.0, a_t),
                                   jnp.where(head0, r_t, 0.0), jnp.where(head0, 0.0, r_t)], axis=0)
            mb = _dot_nt(lhs, jnp.concatenate([b_t, b_t], axis=0))
            mk = _dot_nt(lhs, jnp.concatenate([k_t, k_t], axis=0))
            n_mat = jnp.where(before, mb[:two], 0.0)
            ak = jnp.where(before, mk[:two], 0.0)
            rb = jnp.where(before_eq, mb[two:], 0.0)
            rk = jnp.where(before_eq, mk[two:], 0.0)
            vv = jnp.concatenate([v, v], axis=0)
            xs = _dot_nt(jnp.concatenate([a_t, a_t], axis=0), s) + _dot(ak, vv)
            pw = n_mat
            xs = xs + _dot(pw, xs)
            for _ in range(int(math.log2(ch)) - 1):
                pw = _dot(pw, pw)
                xs = xs + _dot(pw, xs)
            u = jnp.where(head0, xs[:ch], xs[ch:])
            uu = jnp.concatenate([u, u], axis=0)
            yst = _dot(jnp.concatenate([rb, rk], axis=1), jnp.concatenate([uu, vv], axis=0))
            y = _dot_nt(r_t, s) + jnp.where(head0, yst[:ch], yst[ch:])
            y_ref[0, 0, rows, cols] = y
            p_last = jnp.exp(jnp.sum(lw, axis=0, keepdims=True))
            ds = _dot_tn(jnp.concatenate([u, v], axis=0),
                         jnp.concatenate([b_t * p_last, k_t * p_last], axis=0))
            s_ref[p] = s * p_last + jnp.where(same, ds, 0.0)
        return carry

    lax.fori_loop(0, n_sub, sub_chunk, 0)


def _rwkv_scan(r, v, kk, lw, kd, bb, tb, n_ctx_blocks):
    batch, t_len, d_model = r.shape
    n_blocks = t_len // tb
    lanes = min(4 * LANES, d_model)
    n_pairs = lanes // LANES

    def one_map(d, b, g, c):
        return (b, _time_block_order(d, c, n_blocks, n_ctx_blocks), g)

    def two_map(d, b, g, c):
        return (d, b, _time_block_order(d, c, n_blocks, n_ctx_blocks), g)

    one = pl.BlockSpec((1, tb, lanes), one_map)
    two = pl.BlockSpec((1, 1, tb, lanes), two_map)
    return pl.pallas_call(
        functools.partial(_rwkv_scan_kernel, n_pairs=n_pairs, tb=tb),
        out_shape=jax.ShapeDtypeStruct((2, batch, t_len, d_model), F32),
        grid=(2, batch, d_model // lanes, n_blocks),
        in_specs=[one, one, one, two, two, two],
        out_specs=two,
        scratch_shapes=[pltpu.VMEM((n_pairs, LANES, LANES), F32)],
        compiler_params=_params(("parallel", "parallel", "parallel", "arbitrary")),
        name="rwkv_scan",
    )(r, v, kk, lw, kd, bb)


def _rwkv_out_kernel(y_ref, r_ref, v_ref, gg_ref, kd_ref, x_ref, mod_ref, rk_ref, lg_ref, lb_ref,
                     e_ref, et_ref, wo_ref, g2_ref, wr_ref, br_ref,
                     xo_ref, h2_ref, ti_ref, gt_ref):
    y = y_ref[0, 0] + y_ref[1, 0]
    e = e_ref[...]
    et = et_ref[...]
    inv_n = 1.0 / RW_HEAD_DIM
    mean = _dot(_dot(y, e, HI) * inv_n, et, HI)
    yc = y - mean
    var = _dot(_dot(yc * yc, e, HI) * inv_n, et, HI)
    yn = yc * lax.rsqrt(var + RW_GN_EPS) * lg_ref[...] + lb_ref[...]
    r = r_ref[0]
    rr = r * rk_ref[...]
    dots = _dot(rr * kd_ref[0, 0], e, HI) + _dot(rr * kd_ref[1, 0], e, HI)
    bonus = _dot(dots, et, HI) * v_ref[0]
    out = ((yn + bonus) * gg_ref[0]).astype(BF16)
    _post_mixer(x_ref[0], _dot(out, wo_ref[...]), mod_ref[0], g2_ref[...], wr_ref[...], br_ref[...],
                xo_ref, h2_ref, ti_ref, gt_ref)


def _post_specs(batch, t_len, d_model, tb):
    tok = pl.BlockSpec((1, tb, d_model), lambda b, t: (b, t, 0))
    nar = pl.BlockSpec((1, tb, TOP_K), lambda b, t: (b, t, 0))
    shapes = [jax.ShapeDtypeStruct((batch, t_len, d_model), F32),
              jax.ShapeDtypeStruct((batch, t_len, d_model), F32),
              jax.ShapeDtypeStruct((batch, t_len, TOP_K), I32),
              jax.ShapeDtypeStruct((batch, t_len, TOP_K), F32)]
    return shapes, [tok, tok, nar, nar]


def _rwkv_out(y, r, v, gate, kd, x, mod_l, p, norm2_g, moe_wr, moe_br, tb, n_ctx_blocks):
    batch, t_len, d_model = x.shape
    seg_e, seg_et = _segment_matrices(d_model, RW_HEAD_DIM)
    tok = pl.BlockSpec((1, tb, d_model), lambda b, t: (b, t, 0))
    tok2 = pl.BlockSpec((2, 1, tb, d_model), lambda b, t: (0, b, t, 0))
    mod_spec = pl.BlockSpec((1, N_MOD, d_model), lambda b, t: (jnp.where(t < n_ctx_blocks, batch, b), 0, 0))
    consts = [p["rk"].reshape(1, d_model), p["lnx_g"].reshape(1, d_model), p["lnx_b"].reshape(1, d_model),
              seg_e, seg_et, p["wo"].astype(BF16), norm2_g.reshape(1, d_model), *_router_params(moe_wr, moe_br)]
    shapes, specs = _post_specs(batch, t_len, d_model, tb)
    return pl.pallas_call(
        _rwkv_out_kernel,
        out_shape=shapes,
        grid=(batch, t_len // tb),
        in_specs=[tok2, tok, tok, tok, tok2, tok, mod_spec] + [_const_spec(c.shape) for c in consts],
        out_specs=specs,
        compiler_params=_params(("parallel", "parallel")),
        name="rwkv_readout",
    )(y, r, v, gate, kd, x, mod_l, *consts)


def _ml_up_kernel(x_ref, mod_ref, g_ref, w_ref, xm_ref, z_ref):
    mod = mod_ref[0]
    h = _adaln(x_ref[0], g_ref[...], mod[0:1], mod[1:2]).astype(BF16)
    up = _dot(h, w_ref[...])
    inner = xm_ref.shape[-1]
    xm_ref[0] = up[:, :inner]
    z_ref[0] = up[:, inner:]


def _ml_up(x, mod_l, norm_g, w_up, tb, n_ctx_blocks):
    batch, t_len, d_model = x.shape
    inner = w_up.shape[1] // 2
    tok = pl.BlockSpec((1, tb, d_model), lambda b, t: (b, t, 0))
    wide = pl.BlockSpec((1, tb, inner), lambda b, t: (b, t, 0))
    mod_spec = pl.BlockSpec((1, N_MOD, d_model), lambda b, t: (jnp.where(t < n_ctx_blocks, batch, b), 0, 0))
    out = jax.ShapeDtypeStruct((batch, t_len, inner), F32)
    return pl.pallas_call(
        _ml_up_kernel,
        out_shape=[out, out],
        grid=(batch, t_len // tb),
        in_specs=[tok, mod_spec, _const_spec((1, d_model)), _const_spec(w_up.shape)],
        out_specs=[wide, wide],
        compiler_params=_params(("parallel", "parallel")),
        name="mlstm_up",
    )(x, mod_l, norm_g.reshape(1, d_model), w_up.astype(BF16))


def _ml_feat_kernel(xm_ref, xp_ref, xn_ref, cw_ref, cb_ref, wq_ref, wk_ref, wv_ref, wg_ref, bg_ref,
                    q_ref, k_ref, v_ref, xc_ref, gt_ref, *, n_ctx_blocks):
    t = pl.program_id(1)
    n_blocks = pl.num_programs(1)
    is_ctx = t < n_ctx_blocks
    xm = xm_ref[0]
    tb, inner = xm.shape
    row = lax.broadcasted_iota(I32, (tb, 1), 0)
    col = jnp.bitwise_and(row, GRID_W - 1)
    left_ok = jnp.where(is_ctx, row, col) != 0
    right_ok = jnp.where(is_ctx, row - (tb - 1), col - (GRID_W - 1)) != 0
    up = jnp.concatenate([xp_ref[0], xm[:tb - GRID_W]], axis=0)
    down = jnp.concatenate([xm[GRID_W:], xn_ref[0]], axis=0)
    up_ok = jnp.where(is_ctx, -1, jnp.where(t == n_ctx_blocks, row, tb)) >= GRID_W
    down_ok = jnp.where(is_ctx, tb, jnp.where(t == n_blocks - 1, row, 0)) < tb - GRID_W
    up = jnp.where(up_ok, up, 0.0)
    down = jnp.where(down_ok, down, 0.0)
    cw = cw_ref[...]
    acc = jnp.zeros((tb, inner), F32) + cb_ref[...]
    for dy, src in enumerate((up, xm, down)):
        acc = acc + jnp.where(left_ok, pltpu.roll(src, 1, axis=0), 0.0) * cw[3 * dy:3 * dy + 1]
        acc = acc + src * cw[3 * dy + 1:3 * dy + 2]
        acc = acc + jnp.where(right_ok, pltpu.roll(src, tb - 1, axis=0), 0.0) * cw[3 * dy + 2:3 * dy + 3]
    xc = _silu(acc)
    xc_ref[0] = xc
    xcb = xc.astype(BF16)
    xmb = xm.astype(BF16)
    for gidx in range(inner // LANES):
        cols = slice(gidx * LANES, (gidx + 1) * LANES)
        q_ref[0, :, cols] = _dot(xcb[:, cols], wq_ref[gidx]).astype(BF16)
        k_ref[0, :, cols] = _dot(xcb[:, cols], wk_ref[gidx]).astype(BF16)
        v_ref[0, :, cols] = _dot(xmb[:, cols], wv_ref[gidx]).astype(BF16)
    pre = (_dot(q_ref[0], wg_ref[0]) + _dot(k_ref[0], wg_ref[1]) + _dot(v_ref[0], wg_ref[2]) + bg_ref[...])
    lane = lax.broadcasted_iota(I32, pre.shape, 1)
    gt_ref[0] = jnp.where(lane < 2 * ML_HEADS, pre, -_softplus(-pre))


def _block_diag_lanes(w):
    n_blk, blk, _ = w.shape
    per = LANES // blk
    w = w.reshape(n_blk // per, per, blk, blk)
    eye = jnp.eye(per, dtype=w.dtype)
    return jnp.einsum("gaio,ab->gaibo", w, eye).reshape(n_blk // per, LANES, LANES)


def _ml_feat(xm, p, tb, n_ctx_blocks):
    batch, t_len, inner = xm.shape
    hb = tb // GRID_W
    n_halo = t_len // GRID_W
    wide = pl.BlockSpec((1, tb, inner), lambda b, t: (b, t, 0))
    gate_w = jnp.concatenate([p["wi"][0], p["wi"][1], p["wf"][0], p["wf"][1]], axis=1)
    n_gate = gate_w.shape[1]
    gate_w = jnp.pad(gate_w, ((0, 0), (0, LANES - n_gate))).reshape(3, inner, LANES).astype(BF16)
    gate_b = jnp.pad(jnp.concatenate([p["bi"][0], p["bi"][1], p["bf"][0], p["bf"][1]]),
                     (0, LANES - n_gate)).reshape(1, LANES)
    consts = [p["conv_w"].reshape(9, inner), p["conv_b"].reshape(1, inner),
              _block_diag_lanes(p["wq"]).astype(BF16), _block_diag_lanes(p["wk"]).astype(BF16),
              _block_diag_lanes(p["wv"]).astype(BF16), gate_w, gate_b]
    half = jax.ShapeDtypeStruct((batch, t_len, inner), BF16)
    return pl.pallas_call(
        functools.partial(_ml_feat_kernel, n_ctx_blocks=n_ctx_blocks),
        out_shape=[half, half, half, jax.ShapeDtypeStruct((batch, t_len, inner), F32),
                   jax.ShapeDtypeStruct((batch, t_len, LANES), F32)],
        grid=(batch, t_len // tb),
        in_specs=[wide,
                  pl.BlockSpec((1, GRID_W, inner), lambda b, t: (b, jnp.maximum(t * hb - 1, 0), 0)),
                  pl.BlockSpec((1, GRID_W, inner), lambda b, t: (b, jnp.minimum((t + 1) * hb, n_halo - 1), 0)),
                  ] + [_const_spec(c.shape) for c in consts],
        out_specs=[wide, wide, wide, wide, pl.BlockSpec((1, tb, LANES), lambda b, t: (b, t, 0))],
        compiler_params=_params(("parallel", "parallel")),
        name="mlstm_features",
    )(xm, xm, xm, *consts)


def _ml_scan_kernel(q_ref, k_ref, v_ref, gc_ref, gr_ref, h_ref, c_ref, n_ref, m_ref, *, scale):
    d = pl.program_id(0)
    hd = pl.program_id(2)
    c = pl.program_id(3)

    @pl.when(c == 0)
    def _():
        c_ref[...] = jnp.zeros_like(c_ref)
        n_ref[...] = jnp.zeros_like(n_ref)
        m_ref[...] = jnp.zeros_like(m_ref)

    q = q_ref[0]
    k = k_ref[0]
    v = v_ref[0]
    ln = q.shape[0]
    sgn = jnp.where(d == 0, 1, -1)
    ri = lax.broadcasted_iota(I32, (ln, ln), 0)
    ci = lax.broadcasted_iota(I32, (ln, ln), 1)
    allowed = (ci - ri) * sgn <= 0
    tri = jnp.where(allowed, 1.0, 0.0).astype(F32)
    gcol = gc_ref[0]
    grow = gr_ref[0]
    i_idx = d * ML_HEADS + hd
    f_idx = 2 * ML_HEADS + i_idx
    lane_c = lax.broadcasted_iota(I32, gcol.shape, 1)
    sub_r = lax.broadcasted_iota(I32, grow.shape, 0)
    b_all_col = _dot(tri, gcol, HI)
    b_all_row = lax.dot_general(grow, tri, (((1,), (1,)), ((), ())), preferred_element_type=F32, precision=HI)
    b_col = jnp.sum(jnp.where(lane_c == f_idx, b_all_col, 0.0), axis=1, keepdims=True)
    f_col = jnp.sum(jnp.where(lane_c == f_idx, gcol, 0.0), axis=1, keepdims=True)
    i_col = jnp.sum(jnp.where(lane_c == i_idx, gcol, 0.0), axis=1, keepdims=True)
    b_row = jnp.sum(jnp.where(sub_r == f_idx, b_all_row, 0.0), axis=0, keepdims=True)
    i_row = jnp.sum(jnp.where(sub_r == i_idx, grow, 0.0), axis=0, keepdims=True)
    m_prev = m_ref[0:1, 0:1]
    log_inter = b_col + m_prev
    log_intra = jnp.where(allowed, b_col - b_row + i_row, -jnp.inf)
    m_t = jnp.maximum(log_inter, jnp.max(log_intra, axis=1, keepdims=True))
    w_intra = jnp.exp(log_intra - m_t)
    w_inter = jnp.exp(log_inter - m_t)
    s = _dot_nt(q, k) * scale * w_intra
    ct = c_ref[...]
    nrow = n_ref[0:1, :]
    qf = q.astype(F32)
    num = _dot(s.astype(BF16), v) + w_inter * _dot(q, ct.astype(BF16))
    den = jnp.sum(s, axis=1, keepdims=True) + w_inter * jnp.sum(qf * nrow, axis=1, keepdims=True)
    h_ref[0, 0] = num / jnp.maximum(jnp.abs(den), jnp.exp(-m_t))
    gsum = jnp.sum(f_col, axis=0, keepdims=True)
    log_loc = gsum - b_col + i_col
    m_new = jnp.maximum(gsum + m_prev, jnp.max(log_loc, axis=0, keepdims=True))
    w_loc = jnp.exp(log_loc - m_new)
    w_old = jnp.exp(gsum + m_prev - m_new)
    kw = k.astype(F32) * (scale * w_loc)
    c_ref[...] = w_old * ct + _dot_tn(kw.astype(BF16), v)
    n_ref[0:1, :] = w_old * nrow + jnp.sum(kw, axis=0, keepdims=True)
    m_ref[...] = jnp.zeros_like(m_ref) + m_new


def _ml_scan(q, k, v, gates, tb, n_ctx_blocks):
    batch, t_len, inner = q.shape
    dh = inner // ML_HEADS
    n_blocks = t_len // tb
    n_gate = 4 * ML_HEADS
    gates_t = jnp.swapaxes(gates[..., :n_gate], 1, 2)

    def head_map(d, b, h, c):
        return (b, _time_block_order(d, c, n_blocks, n_ctx_blocks), h)

    def col_map(d, b, h, c):
        return (b, _time_block_order(d, c, n_blocks, n_ctx_blocks), 0)

    def row_map(d, b, h, c):
        return (b, 0, _time_block_order(d, c, n_blocks, n_ctx_blocks))

    def out_map(d, b, h, c):
        return (d, b, _time_block_order(d, c, n_blocks, n_ctx_blocks), h)

    head = pl.BlockSpec((1, tb, dh), head_map)
    return pl.pallas_call(
        functools.partial(_ml_scan_kernel, scale=float(dh) ** -0.5),
        out_shape=jax.ShapeDtypeStruct((2, batch, t_len, inner), F32),
        grid=(2, batch, ML_HEADS, n_blocks),
        in_specs=[head, head, head, pl.BlockSpec((1, tb, LANES), col_map),
                  pl.BlockSpec((1, n_gate, tb), row_map)],
        out_specs=pl.BlockSpec((1, 1, tb, dh), out_map),
        scratch_shapes=[pltpu.VMEM((dh, dh), F32), pltpu.VMEM((8, dh), F32), pltpu.VMEM((8, LANES), F32)],
        compiler_params=_params(("parallel", "parallel", "parallel", "arbitrary")),
        name="mlstm_scan",
    )(q, k, v, gates, gates_t)


def _ml_out_kernel(hs_ref, xc_ref, z_ref, x_ref, mod_ref, ng_ref, sk_ref, wd_ref, g2_ref, wr_ref, br_ref,
                   xo_ref, h2_ref, ti_ref, gt_ref):
    hs = hs_ref[0, 0] + hs_ref[1, 0]
    inner = hs.shape[1]
    dh = inner // ML_HEADS
    parts = []
    for hd in range(ML_HEADS):
        blk = hs[:, hd * dh:(hd + 1) * dh]
        mean = jnp.mean(blk, axis=1, keepdims=True)
        cen = blk - mean
        var = jnp.mean(cen * cen, axis=1, keepdims=True)
        parts.append(cen * lax.rsqrt(var + ML_LN_EPS))
    hn = jnp.concatenate(parts, axis=1) * ng_ref[...]
    y = ((hn + sk_ref[...] * xc_ref[0]) * _silu(z_ref[0])).astype(BF16)
    _post_mixer(x_ref[0], _dot(y, wd_ref[...]), mod_ref[0], g2_ref[...], wr_ref[...], br_ref[...],
                xo_ref, h2_ref, ti_ref, gt_ref)


def _ml_out(hs, xc, z, x, mod_l, p, norm2_g, moe_wr, moe_br, tb, n_ctx_blocks):
    batch, t_len, d_model = x.shape
    inner = xc.shape[-1]
    tok = pl.BlockSpec((1, tb, d_model), lambda b, t: (b, t, 0))
    wide = pl.BlockSpec((1, tb, inner), lambda b, t: (b, t, 0))
    wide2 = pl.BlockSpec((2, 1, tb, inner), lambda b, t: (0, b, t, 0))
    mod_spec = pl.BlockSpec((1, N_MOD, d_model), lambda b, t: (jnp.where(t < n_ctx_blocks, batch, b), 0, 0))
    consts = [p["norm_g"].reshape(1, inner), p["skip"].reshape(1, inner), p["w_down"].astype(BF16),
              norm2_g.reshape(1, d_model), *_router_params(moe_wr, moe_br)]
    shapes, specs = _post_specs(batch, t_len, d_model, tb)
    return pl.pallas_call(
        _ml_out_kernel,
        out_shape=shapes,
        grid=(batch, t_len // tb),
        in_specs=[wide2, wide, wide, tok, mod_spec] + [_const_spec(c.shape) for c in consts],
        out_specs=specs,
        compiler_params=_params(("parallel", "parallel")),
        name="mlstm_readout",
    )(hs, xc, z, x, mod_l, *consts)


def _moe_plan(top_i, n_exp, bm):
    n_asg = top_i.size
    flat_e = top_i.reshape(n_asg)
    onehot = (flat_e[:, None] == jnp.arange(n_exp, dtype=I32)[None, :]).astype(I32)
    csum = jnp.cumsum(onehot, axis=0)
    rank = jnp.sum(csum * onehot, axis=1) - 1
    counts = csum[-1]
    padded = (counts + bm - 1) // bm * bm
    pad_end = jnp.cumsum(padded)
    pad_start = pad_end - padded
    dest = pad_start[flat_e] + rank
    n_blk = n_asg // bm + n_exp
    asg = jnp.arange(n_asg, dtype=I32)
    slot_asg = jnp.full((n_blk * bm,), -1, I32).at[dest].set(asg)
    block_e = jnp.minimum(jnp.searchsorted(pad_end, jnp.arange(n_blk, dtype=I32) * bm, side="right"),
                          n_exp - 1).astype(I32)
    n_used = (pad_end[-1] // bm).astype(I32).reshape(1)
    return block_e, slot_asg, n_used


def _moe_expert_kernel(be_ref, dst_ref, nu_ref, h_hbm, wgu_ref, bgu_ref, wd_ref, bd_ref, y_hbm,
                       xbuf, obuf, wgu_bf, wd_bf, gsem, ssem, *, bm):
    i = pl.program_id(0)
    n_used = nu_ref[0]
    slot = i % 2

    def gather_copy(blk, buf, r):
        tok = jnp.right_shift(jnp.maximum(dst_ref[blk * bm + r], 0), 2)
        return pltpu.make_async_copy(h_hbm.at[pl.ds(tok, 1)], xbuf.at[buf, pl.ds(r, 1)], gsem.at[buf])

    def scatter_copy(blk, buf, r):
        dst = jnp.maximum(dst_ref[blk * bm + r], 0)
        return pltpu.make_async_copy(obuf.at[buf, pl.ds(r, 1)], y_hbm.at[pl.ds(dst, 1)], ssem.at[buf])

    def gather_start(blk, buf):
        def body(r, carry):
            gather_copy(blk, buf, r).start()
            return carry
        lax.fori_loop(0, bm, body, 0)

    def gather_wait(blk, buf):
        def body(r, carry):
            gather_copy(blk, buf, r).wait()
            return carry
        lax.fori_loop(0, bm, body, 0)

    def scatter_start(blk, buf):
        def body(r, carry):
            @pl.when(dst_ref[blk * bm + r] >= 0)
            def _():
                scatter_copy(blk, buf, r).start()
            return carry
        lax.fori_loop(0, bm, body, 0)

    def scatter_wait(blk, buf):
        def body(r, carry):
            @pl.when(dst_ref[blk * bm + r] >= 0)
            def _():
                scatter_copy(blk, buf, r).wait()
            return carry
        lax.fori_loop(0, bm, body, 0)

    @pl.when(i < n_used)
    def _():
        @pl.when(i == 0)
        def _():
            gather_start(0, 0)

        @pl.when(i + 1 < n_used)
        def _():
            gather_start(i + 1, 1 - slot)

        changed = jnp.logical_or(i == 0, be_ref[i] != be_ref[jnp.maximum(i - 1, 0)])

        @pl.when(changed)
        def _():
            wgu_bf[...] = wgu_ref[0].astype(BF16)
            wd_bf[...] = wd_ref[0].astype(BF16)

        gather_wait(i, slot)

        @pl.when(i >= 2)
        def _():
            scatter_wait(i - 2, slot)

        x = xbuf[slot].astype(BF16)
        gu = _dot(x, wgu_bf[...]) + bgu_ref[0]
        ff = gu.shape[1] // 2
        gate = jnp.minimum(gu[:, :ff], SWIGLU_LIMIT)
        up = jnp.clip(gu[:, ff:], -SWIGLU_LIMIT, SWIGLU_LIMIT)
        act = gate * _sigmoid(SWIGLU_ALPHA * gate) * (up + 1.0)
        obuf[slot] = _dot(act.astype(BF16), wd_bf[...]) + bd_ref[0]
        scatter_start(i, slot)

        @pl.when(i == n_used - 1)
        def _():
            @pl.when(i >= 1)
            def _():
                scatter_wait(i - 1, 1 - slot)
            scatter_wait(i, slot)


def _moe_experts(h2, top_i, w_gu, b_gu, w_d, b_d):
    n_tok, d_model = h2.shape
    n_exp, _, ff2 = w_gu.shape
    bm = MOE_BLOCK
    n_asg = n_tok * TOP_K
    assert n_asg % bm == 0
    assert TOP_K == 4
    block_e, slot_asg, n_used = _moe_plan(top_i, n_exp, bm)
    n_blk = block_e.shape[0]
    grid_spec = pltpu.PrefetchScalarGridSpec(
        num_scalar_prefetch=3,
        grid=(n_blk,),
        in_specs=[
            pl.BlockSpec(memory_space=pl.ANY),
            pl.BlockSpec((1, d_model, ff2), lambda i, be, *_: (be[i], 0, 0)),
            pl.BlockSpec((1, 1, ff2), lambda i, be, *_: (be[i], 0, 0)),
            pl.BlockSpec((1, ff2 // 2, d_model), lambda i, be, *_: (be[i], 0, 0)),
            pl.BlockSpec((1, 1, d_model), lambda i, be, *_: (be[i], 0, 0)),
        ],
        out_specs=pl.BlockSpec(memory_space=pl.ANY),
        scratch_shapes=[
            pltpu.VMEM((2, bm, d_model), F32),
            pltpu.VMEM((2, bm, d_model), F32),
            pltpu.VMEM((d_model, ff2), BF16),
            pltpu.VMEM((ff2 // 2, d_model), BF16),
            pltpu.SemaphoreType.DMA((2,)),
            pltpu.SemaphoreType.DMA((2,)),
        ],
    )
    return pl.pallas_call(
        functools.partial(_moe_expert_kernel, bm=bm),
        out_shape=jax.ShapeDtypeStruct((n_asg, d_model), F32),
        grid_spec=grid_spec,
        compiler_params=_params(("arbitrary",)),
        name="moe_experts",
    )(block_e, slot_asg, n_used, h2, w_gu, b_gu.reshape(n_exp, 1, ff2), w_d,
      b_d.reshape(n_exp, 1, d_model))


def _moe_combine_kernel(x_ref, y4_ref, gt_ref, mod_ref, fg_ref, o_ref, *, final):
    d_model = x_ref.shape[-1]
    gates = gt_ref[0]
    f = jnp.zeros(x_ref.shape[1:], F32)
    for j in range(TOP_K):
        f = f + gates[:, j:j + 1] * y4_ref[0, :, j * d_model:(j + 1) * d_model]
    xn = x_ref[0] + mod_ref[0][5:6] * f
    if final:
        xn = xn * lax.rsqrt(jnp.mean(xn * xn, axis=-1, keepdims=True) + NORM_EPS) * fg_ref[...]
    o_ref[0] = xn


def _moe_combine(x, y4, gates, mod_l, final_g, tb, n_ctx_blocks, final):
    batch, t_len, d_model = x.shape
    t_out = y4.shape[1]
    off = (t_len - t_out) // tb
    mod_spec = pl.BlockSpec((1, N_MOD, d_model),
                            lambda b, t: (jnp.where(t + off < n_ctx_blocks, batch, b), 0, 0))
    return pl.pallas_call(
        functools.partial(_moe_combine_kernel, final=final),
        out_shape=jax.ShapeDtypeStruct((batch, t_out, d_model), F32),
        grid=(batch, t_out // tb),
        in_specs=[pl.BlockSpec((1, tb, d_model), lambda b, t: (b, t + off, 0)),
                  pl.BlockSpec((1, tb, TOP_K * d_model), lambda b, t: (b, t, 0)),
                  pl.BlockSpec((1, tb, TOP_K), lambda b, t: (b, t, 0)),
                  mod_spec, _const_spec((1, d_model))],
        out_specs=pl.BlockSpec((1, tb, d_model), lambda b, t: (b, t, 0)),
        compiler_params=_params(("parallel", "parallel")),
        name="moe_combine",
    )(x, y4, gates, mod_l, final_g.reshape(1, d_model))


def _moe_layer(x, h2, top_i, gates, mod_l, moe_p, final_g, tb, n_ctx_blocks, last):
    batch, t_len, d_model = x.shape
    if last:
        n_ctx = n_ctx_blocks * tb
        h2, top_i, gates = h2[:, n_ctx:], top_i[:, n_ctx:], gates[:, n_ctx:]
    t_moe = h2.shape[1]
    y4 = _moe_experts(h2.reshape(batch * t_moe, d_model), top_i.reshape(batch * t_moe, TOP_K), *moe_p)
    return _moe_combine(x, y4.reshape(batch, t_moe, TOP_K * d_model), gates, mod_l, final_g,
                        tb, n_ctx_blocks, last)


def kernel(x, c, ctx, c_ctx, ada_w, ada_b, norm1_g, norm2_g, final_g, rw_mu, rw_wr, rw_wk, rw_wv, rw_wo, rw_w0, rw_w1, rw_w2, rw_a0, rw_a1, rw_a2, rw_g1, rw_g2, rw_kk, rw_ka, rw_rk, rw_lnx_g, rw_lnx_b, rw_v0, rw_v1, rw_v2, ml_w_up, ml_conv_w, ml_conv_b, ml_wq, ml_wk, ml_wv, ml_wi, ml_bi, ml_wf, ml_bf, ml_norm_g, ml_skip, ml_w_down, moe_wr, moe_br, moe_w_gu, moe_b_gu, moe_w_d, moe_b_d):
    batch, seq, d_model = x.shape
    n_ctx = ctx.shape[1]
    depth = ada_w.shape[0]
    tb = math.gcd(MAX_TIME_BLOCK, n_ctx, seq)
    assert tb % GRID_W == 0 and n_ctx == tb and d_model % (2 * RW_HEAD_DIM) == 0
    n_ctx_blocks = n_ctx // tb

    mod = _modulation(c, c_ctx, ada_w, ada_b)
    xs = jnp.concatenate([ctx, x], axis=1)
    v_first = None
    for i in range(depth):
        last = i == depth - 1
        j = i // 2
        mod_l = mod[i]
        if i % 2 == 0:
            p = dict(mu=rw_mu[j], wr=rw_wr[j], wk=rw_wk[j], wv=rw_wv[j], wo=rw_wo[j], w0=rw_w0[j],
                     w1=rw_w1[j], w2=rw_w2[j], a0=rw_a0[j], a1=rw_a1[j], a2=rw_a2[j], g1=rw_g1[j],
                     g2=rw_g2[j], kk=rw_kk[j], ka=rw_ka[j], rk=rw_rk[j], lnx_g=rw_lnx_g[j],
                     lnx_b=rw_lnx_b[j])
            vres = None if j == 0 else (rw_v0[j - 1], rw_v1[j - 1], rw_v2[j - 1])
            r, v, gate, kk, lw, kd, bb = _rwkv_pre(xs, mod_l, norm1_g[i], p, vres, v_first, tb, n_ctx_blocks)
            if j == 0:
                v_first = v
            y = _rwkv_scan(r, v, kk, lw, kd, bb, tb, n_ctx_blocks)
            xs, h2, top_i, gates = _rwkv_out(y, r, v, gate, kd, xs, mod_l, p, norm2_g[i],
                                             moe_wr[i], moe_br[i], tb, n_ctx_blocks)
        else:
            p = dict(conv_w=ml_conv_w[j], conv_b=ml_conv_b[j], wq=ml_wq[j], wk=ml_wk[j], wv=ml_wv[j],
                     wi=ml_wi[j], bi=ml_bi[j], wf=ml_wf[j], bf=ml_bf[j], norm_g=ml_norm_g[j],
                     skip=ml_skip[j], w_down=ml_w_down[j])
            xm, z = _ml_up(xs, mod_l, norm1_g[i], ml_w_up[j], tb, n_ctx_blocks)
            q, k, v, xc, gts = _ml_feat(xm, p, tb, n_ctx_blocks)
            hs = _ml_scan(q, k, v, gts, tb, n_ctx_blocks)
            xs, h2, top_i, gates = _ml_out(hs, xc, z, xs, mod_l, p, norm2_g[i],
                                           moe_wr[i], moe_br[i], tb, n_ctx_blocks)
        moe_p = (moe_w_gu[i], moe_b_gu[i], moe_w_d[i], moe_b_d[i])
        xs = _moe_layer(xs, h2, top_i, gates, mod_l, moe_p, final_g, tb, n_ctx_blocks, last)
    return xs
```

```python
import functools
import math

import jax
import jax.numpy as jnp
from jax import lax
from jax.experimental import pallas as pl
from jax.experimental.pallas import tpu as pltpu

F32 = jnp.float32
BF16 = jnp.bfloat16
I32 = jnp.int32
HI = lax.Precision.HIGHEST

GRID_W = 64
N_MOD = 6
NORM_EPS = 1e-6
RW_HEAD_DIM = 64
RW_GN_EPS = 64e-5
RW_CHUNK = 64
ML_HEADS = 4
ML_QKV_BLOCK = 4
ML_LN_EPS = 1e-6
TOP_K = 4
SWIGLU_LIMIT = 7.0
SWIGLU_ALPHA = 1.702
MOE_BLOCK = 256
MOE_OUT_BUFFERS = 3
MOE_LEAD_BLOCKS = 2
MOE_ROW_UNROLL = 8
LANES = 128
MAX_TIME_BLOCK = 256
VMEM_LIMIT = 56 * 1024 * 1024


def _dot(a, b, precision=None):
    return jnp.dot(a, b, preferred_element_type=F32, precision=precision)


def _dot_nt(a, b):
    return lax.dot_general(a, b, (((1,), (1,)), ((), ())), preferred_element_type=F32)


def _dot_tn(a, b, precision=None):
    return lax.dot_general(a, b, (((0,), (0,)), ((), ())), preferred_element_type=F32,
                           precision=precision)


def _sigmoid(x):
    return 1.0 / (1.0 + jnp.exp(-x))


def _softplus(x):
    return jnp.maximum(x, 0.0) + jnp.log(1.0 + jnp.exp(-jnp.abs(x)))


def _silu(x):
    return x * _sigmoid(x)


def _adaln(x, g, shift, scale):
    y = x * lax.rsqrt(jnp.mean(x * x, axis=-1, keepdims=True) + NORM_EPS) * g
    return y * (1.0 + scale) + shift


def _const_spec(shape):
    nd = len(shape)
    return pl.BlockSpec(shape, lambda *_: (0,) * nd, pipeline_mode=pl.Buffered(1))


def _params(sem, vmem=VMEM_LIMIT):
    return pltpu.CompilerParams(dimension_semantics=sem, vmem_limit_bytes=vmem)


def _time_block_order(d, c, n_blocks, n_ctx_blocks):
    rev = jnp.where(c < n_ctx_blocks, n_ctx_blocks - 1 - c, n_blocks - 1 - (c - n_ctx_blocks))
    return jnp.where(d == 0, c, rev)


def _mod_kernel(c_ref, w_ref, b_ref, o_ref):
    o_ref[0] = _dot(_silu(c_ref[...]), w_ref[0], HI) + b_ref[0]


def _modulation(c, c_ctx, ada_w, ada_b):
    depth, d_model, _ = ada_w.shape
    batch = c.shape[0]
    rows = -(-(batch + 1) // 8) * 8
    cc = jnp.concatenate([c, c_ctx[None], jnp.zeros((rows - batch - 1, d_model), F32)], axis=0)
    out = pl.pallas_call(
        _mod_kernel,
        out_shape=jax.ShapeDtypeStruct((depth, rows, N_MOD * d_model), F32),
        grid=(depth, N_MOD),
        in_specs=[
            pl.BlockSpec((rows, d_model), lambda l, j: (0, 0)),
            pl.BlockSpec((1, d_model, d_model), lambda l, j: (l, 0, j)),
            pl.BlockSpec((1, 1, d_model), lambda l, j: (l, 0, j)),
        ],
        out_specs=pl.BlockSpec((1, rows, d_model), lambda l, j: (l, 0, j)),
        compiler_params=_params(("parallel", "parallel")),
        name="adaln_modulation",
    )(cc, ada_w, ada_b.reshape(depth, 1, N_MOD * d_model))
    return out.reshape(depth, rows, N_MOD, d_model)


def _neighbours(h, h_prev, h_next, is_ctx, first_lat, last_lat):
    tb = h.shape[0]
    row = lax.broadcasted_iota(I32, (tb, 1), 0)
    col = jnp.bitwise_and(row, GRID_W - 1)
    left_ok = jnp.where(is_ctx, row, col) != 0
    right_ok = jnp.where(is_ctx, row - (tb - 1), col - (GRID_W - 1)) != 0
    left = jnp.where(left_ok, pltpu.roll(h, 1, axis=0), 0.0)
    right = jnp.where(right_ok, pltpu.roll(h, tb - 1, axis=0), 0.0)
    up = jnp.concatenate([h_prev, h[:tb - GRID_W]], axis=0)
    down = jnp.concatenate([h[GRID_W:], h_next], axis=0)
    up_ok = jnp.where(first_lat, row, tb) >= GRID_W
    down_ok = jnp.where(last_lat, row, 0) < tb - GRID_W
    up = jnp.where(up_ok, up, 0.0)
    down = jnp.where(down_ok, down, 0.0)
    return left, right, up, down


def _chunk_cumsum(x, reverse):
    n = x.shape[0]
    pos = jnp.bitwise_and(lax.broadcasted_iota(I32, (n, 1), 0), RW_CHUNK - 1)
    step = 1
    while step < RW_CHUNK:
        if reverse:
            x = x + jnp.where(pos < RW_CHUNK - step, pltpu.roll(x, n - step, axis=0), 0.0)
        else:
            x = x + jnp.where(pos >= step, pltpu.roll(x, step, axis=0), 0.0)
        step *= 2
    return x


def _segment_matrices(d_model, seg):
    ch = jnp.arange(d_model)[:, None] // seg
    e = (ch == jnp.arange(LANES)[None, :]).astype(F32)
    return e, e.T


def _route_top_k(logits):
    n, n_exp = logits.shape
    lane = lax.broadcasted_iota(I32, (n, n_exp), 1).astype(F32)
    lane_k = lax.broadcasted_iota(I32, (n, TOP_K), 1)
    work = logits
    vals, idxs = [], []
    for _ in range(TOP_K):
        m = jnp.max(work, axis=-1, keepdims=True)
        idx = jnp.min(jnp.where(work == m, lane, float(n_exp)), axis=-1, keepdims=True)
        vals.append(m)
        idxs.append(idx)
        work = jnp.where(lane == idx, -jnp.inf, work)
    es = [jnp.exp(v - vals[0]) for v in vals]
    den = es[0] + es[1] + es[2] + es[3]
    top_i = jnp.zeros((n, TOP_K), I32)
    gates = jnp.zeros((n, TOP_K), F32)
    for j in range(TOP_K):
        top_i = jnp.where(lane_k == j, idxs[j].astype(I32), top_i)
        gates = jnp.where(lane_k == j, es[j] / den, gates)
    return top_i, gates


def _router_params(moe_wr, moe_br):
    n_exp = moe_wr.shape[-1]
    assert n_exp <= LANES
    wr = jnp.pad(moe_wr, ((0, 0), (0, LANES - n_exp)))
    br = jnp.pad(moe_br, (0, LANES - n_exp), constant_values=-1e30)
    return wr, br.reshape(1, LANES)


def _post_mixer(x, y, mod, g2, wr, br, xo_ref, h2_ref, ti_ref, gt_ref):
    xn = x + mod[2:3] * y
    xo_ref[0] = xn
    h2 = _adaln(xn, g2, mod[3:4], mod[4:5])
    h2_ref[0] = h2
    top_i, gates = _route_top_k(_dot(h2, wr, HI) + br)
    ti_ref[0] = top_i
    gt_ref[0] = gates


def _rwkv_pre_kernel(*refs, has_vres, n_ctx_blocks):
    (x_ref, xp_ref, xn_ref, mod_ref, g_ref, mu_ref, wr_ref, wk_ref, wv_ref, wlw_ref, wla_ref,
     wg1_ref, w2_ref, a2_ref, g2_ref, w0_ref, a0_ref, kkp_ref, kap_ref, rk_ref, e_ref, et_ref) = refs[:22]
    if has_vres:
        vf_ref, v0_ref, v1_ref, v2_ref = refs[22:26]
        outs = refs[26:]
    else:
        outs = refs[22:]
    r_ref, v_ref, gg_ref, bo_ref, lp_ref, ap_ref, kd_ref, bb_ref = outs

    t = pl.program_id(1)
    n_blocks = pl.num_programs(1)
    is_ctx = t < n_ctx_blocks
    mod = mod_ref[0]
    g = g_ref[...]
    h = _adaln(x_ref[0], g, mod[0:1], mod[1:2])
    hp = _adaln(xp_ref[0], g, mod[0:1], mod[1:2])
    hn = _adaln(xn_ref[0], g, mod[0:1], mod[1:2])
    left, right, up, down = _neighbours(h, hp, hn, is_ctx, t == n_ctx_blocks, t == n_blocks - 1)
    d_model = h.shape[1]
    q = d_model // 4
    lane = lax.broadcasted_iota(I32, (1, d_model), 1)
    q2 = jnp.where(is_ctx, left, up)
    q3 = jnp.where(is_ctx, right, down)
    shifted = jnp.where(lane < q, left, jnp.where(lane < 2 * q, right, jnp.where(lane < 3 * q, q2, q3)))
    xx = shifted - h
    mu = mu_ref[...]

    def mix(j):
        return (h + xx * mu[j:j + 1]).astype(BF16)

    r = _dot(mix(0), wr_ref[...])
    k = _dot(mix(2), wk_ref[...])
    xv = mix(3)
    v = _dot(xv, wv_ref[...])
    if has_vres:
        lora = _dot(_dot(xv, v1_ref[...]).astype(BF16), v2_ref[...])
        v = v + (vf_ref[0] - v) * _sigmoid(v0_ref[...] + lora)
    gate = _dot(_sigmoid(_dot(mix(5), wg1_ref[...])).astype(BF16), g2_ref[...])
    kx = k * kkp_ref[...]
    ss = _dot(kx * kx, e_ref[...], HI)
    inv = 1.0 / jnp.maximum(jnp.sqrt(ss), 1e-12)
    kk = kx * _dot(inv, et_ref[...], HI)
    r_ref[0] = r
    v_ref[0] = v
    gg_ref[0] = gate
    tw = jnp.tanh(_dot(mix(1), wlw_ref[...])).astype(BF16)
    la = _dot(mix(4), wla_ref[...]).astype(BF16)
    kap = kap_ref[...]
    rr = r * rk_ref[...]
    head_dots = jnp.zeros((h.shape[0], LANES), F32)
    for d in range(2):
        logw = -_softplus(-(w0_ref[d:d + 1] + _dot(tw, w2_ref[d]))) - 0.5
        decay_rate = jnp.exp(logw)
        lp_ref[d, 0] = _chunk_cumsum(-decay_rate, reverse=d == 1)
        ap_ref[d, 0] = -kk * jnp.exp(decay_rate)
        a = _sigmoid(a0_ref[d:d + 1] + _dot(la, a2_ref[d]))
        kd = k * (1.0 + (a - 1.0) * kap)
        kd_ref[d, 0] = kd
        bb_ref[d, 0] = kk * a
        head_dots = head_dots + _dot(rr * kd, e_ref[...], HI)
    bo_ref[0] = _dot(head_dots, et_ref[...], HI) * v


def _rwkv_pre(x, mod_l, norm_g, p, vres, v_first, tb, n_ctx_blocks):
    batch, t_len, d_model = x.shape
    n_blocks = t_len // tb
    hb = tb // GRID_W
    n_halo = t_len // GRID_W
    lora = p["w1"].shape[-1]
    assert 2 * lora == LANES and p["g1"].shape[-1] == LANES
    zeros = jnp.zeros((lora, d_model), F32)
    w2p = jnp.stack([jnp.concatenate([p["w2"][0], zeros]), jnp.concatenate([zeros, p["w2"][1]])])
    a2p = jnp.stack([jnp.concatenate([p["a2"][0], zeros]), jnp.concatenate([zeros, p["a2"][1]])])
    seg_e, seg_et = _segment_matrices(d_model, RW_HEAD_DIM)
    has_vres = vres is not None

    tok = pl.BlockSpec((1, tb, d_model), lambda b, t: (b, t, 0))
    tok2 = pl.BlockSpec((2, 1, tb, d_model), lambda b, t: (0, b, t, 0))
    in_specs = [
        tok,
        pl.BlockSpec((1, GRID_W, d_model), lambda b, t: (b, jnp.maximum(t * hb - 1, 0), 0)),
        pl.BlockSpec((1, GRID_W, d_model), lambda b, t: (b, jnp.minimum((t + 1) * hb, n_halo - 1), 0)),
        pl.BlockSpec((1, N_MOD, d_model), lambda b, t: (jnp.where(t < n_ctx_blocks, batch, b), 0, 0)),
    ]
    args = [x, x, x, mod_l]
    consts = [
        norm_g.reshape(1, d_model), p["mu"],
        p["wr"].astype(BF16), p["wk"].astype(BF16), p["wv"].astype(BF16),
        jnp.concatenate([p["w1"][0], p["w1"][1]], axis=1).astype(BF16),
        jnp.concatenate([p["a1"][0], p["a1"][1]], axis=1).astype(BF16),
        p["g1"].astype(BF16), w2p.astype(BF16), a2p.astype(BF16), p["g2"].astype(BF16),
        p["w0"], p["a0"], p["kk"].reshape(1, d_model), p["ka"].reshape(1, d_model),
        p["rk"].reshape(1, d_model), seg_e, seg_et,
    ]
    in_specs += [_const_spec(c.shape) for c in consts]
    args += consts
    if has_vres:
        v0, v1, v2 = vres
        extra = [v0.reshape(1, d_model), v1.astype(BF16), v2.astype(BF16)]
        in_specs += [tok] + [_const_spec(c.shape) for c in extra]
        args += [v_first] + extra
    one = jax.ShapeDtypeStruct((batch, t_len, d_model), F32)
    two = jax.ShapeDtypeStruct((2, batch, t_len, d_model), F32)
    return pl.pallas_call(
        functools.partial(_rwkv_pre_kernel, has_vres=has_vres, n_ctx_blocks=n_ctx_blocks),
        out_shape=[one, one, one, one, two, two, two, two],
        grid=(batch, n_blocks),
        in_specs=in_specs,
        out_specs=[tok, tok, tok, tok, tok2, tok2, tok2, tok2],
        compiler_params=_params(("parallel", "parallel")),
        name="rwkv_features",
    )(*args)


def _rwkv_scan_kernel(r_ref, v_ref, lp_ref, ap_ref, kd_ref, bb_ref, y_ref, s_ref, *, n_pairs, tb):
    d = pl.program_id(0)
    c = pl.program_id(3)
    ch = RW_CHUNK
    two = 2 * ch

    @pl.when(c == 0)
    def _():
        s_ref[...] = jnp.zeros_like(s_ref)

    sgn = jnp.where(d == 0, 1, -1)
    r2 = lax.broadcasted_iota(I32, (two, two), 0)
    c2 = lax.broadcasted_iota(I32, (two, two), 1)
    same = jnp.right_shift(r2, 6) == jnp.right_shift(c2, 6)
    delta = (jnp.bitwise_and(c2, ch - 1) - jnp.bitwise_and(r2, ch - 1)) * sgn
    before = jnp.logical_and(same, delta < 0)
    before_eq = jnp.logical_and(same, delta <= 0)
    head0 = lax.broadcasted_iota(I32, (ch, LANES), 1) < RW_HEAD_DIM
    head0w = lax.broadcasted_iota(I32, (ch, 2 * LANES), 1)
    head0w = jnp.bitwise_and(head0w, LANES - 1) < RW_HEAD_DIM
    n_sub = tb // ch
    pairs = range(n_pairs)

    def stack2(x):
        return jnp.concatenate([x, x], axis=0)

    def pick(x, mask):
        return jnp.where(mask, x[:ch], x[ch:])

    def sub_chunk(j, carry):
        jj = jnp.where(d == 0, j, n_sub - 1 - j)
        start = pl.multiple_of(jj * ch, ch)
        rows = pl.ds(start, ch)
        last = pl.ds(pl.multiple_of(start + jnp.where(d == 0, ch - 8, 0), 8), 8)
        cols = [slice(p * LANES, (p + 1) * LANES) for p in pairs]

        v, vv, r_t, a_t, b_p, k_p, m_b, m_k, p_end = [], [], [], [], [], [], [], [], []
        for p in pairs:
            lp = lp_ref[0, 0, rows, cols[p]]
            e_in = jnp.exp(lp)
            e_out = jnp.exp(-lp)
            lp_edge = lp_ref[0, 0, last, cols[p]]
            p_last = jnp.exp(jnp.where(d == 0, lp_edge[7:8], lp_edge[0:1]))
            p_end.append(p_last)
            at = ap_ref[0, 0, rows, cols[p]] * e_in
            rt = r_ref[0, rows, cols[p]] * e_in
            bt = bb_ref[0, 0, rows, cols[p]] * e_out
            kt = kd_ref[0, 0, rows, cols[p]] * e_out
            vp = v_ref[0, rows, cols[p]]
            lhs = jnp.concatenate([jnp.where(head0, at, 0.0), jnp.where(head0, 0.0, at),
                                   jnp.where(head0, rt, 0.0), jnp.where(head0, 0.0, rt)],
                                  axis=0).astype(BF16)
            m_b.append(_dot_nt(lhs, stack2(bt).astype(BF16)))
            m_k.append(_dot_nt(lhs, stack2(kt).astype(BF16)))
            v.append(vp)
            vv.append(stack2(vp).astype(BF16))
            r_t.append(rt)
            a_t.append(at)
            b_p.append(bt * p_last)
            k_p.append(kt * p_last)
        n_pow = [jnp.where(before, m_b[p][:two], 0.0).astype(BF16) for p in pairs]
        a_k = [jnp.where(before, m_k[p][:two], 0.0).astype(BF16) for p in pairs]
        r_b = [jnp.where(before_eq, m_b[p][two:], 0.0).astype(BF16) for p in pairs]
        r_k = [jnp.where(before_eq, m_k[p][two:], 0.0).astype(BF16) for p in pairs]
        z = [jnp.concatenate([stack2(a_t[p]), _dot(a_k[p], vv[p])], axis=1) for p in pairs]
        z = [z[p] + _dot(n_pow[p], z[p].astype(BF16)) for p in pairs]
        for _ in range(int(math.log2(ch)) - 1):
            n_pow = [_dot(n_pow[p], n_pow[p]).astype(BF16) for p in pairs]
            z = [z[p] + _dot(n_pow[p], z[p].astype(BF16)) for p in pairs]
        q = [_dot(r_b[p], z[p].astype(BF16)) for p in pairs]
        y_k = [_dot(r_k[p], vv[p]) for p in pairs]
        w_u = [pick(z[p], head0w) for p in pairs]
        r_y = [pick(q[p], head0w) for p in pairs]
        lhs_s = [jnp.concatenate([w_u[p][:, :LANES], r_t[p] + r_y[p][:, :LANES]], axis=0).astype(BF16)
                 for p in pairs]
        u_0 = [w_u[p][:, LANES:] for p in pairs]
        y_0 = [r_y[p][:, LANES:] + pick(y_k[p], head0) for p in pairs]
        bk_p = [jnp.concatenate([b_p[p], k_p[p]], axis=0).astype(BF16) for p in pairs]

        s_old = [s_ref[p] for p in pairs]
        g = [_dot_nt(lhs_s[p], s_old[p].astype(BF16)) for p in pairs]
        u = [g[p][:ch] + u_0[p] for p in pairs]
        for p in pairs:
            y_ref[0, 0, rows, cols[p]] = g[p][ch:] + y_0[p]
        d_s = [_dot_tn(jnp.concatenate([u[p], v[p]], axis=0).astype(BF16), bk_p[p]) for p in pairs]
        for p in pairs:
            s_ref[p] = s_old[p] * p_end[p] + jnp.where(same, d_s[p], 0.0)
        return carry

    lax.fori_loop(0, n_sub, sub_chunk, 0)


def _rwkv_scan(r, v, lp, ap, kd, bb, tb, n_ctx_blocks):
    batch, t_len, d_model = r.shape
    n_blocks = t_len // tb
    lanes = min(8 * LANES, d_model)
    n_pairs = lanes // LANES

    def one_map(d, b, g, c):
        return (b, _time_block_order(d, c, n_blocks, n_ctx_blocks), g)

    def two_map(d, b, g, c):
        return (d, b, _time_block_order(d, c, n_blocks, n_ctx_blocks), g)

    one = pl.BlockSpec((1, tb, lanes), one_map)
    two = pl.BlockSpec((1, 1, tb, lanes), two_map)
    return pl.pallas_call(
        functools.partial(_rwkv_scan_kernel, n_pairs=n_pairs, tb=tb),
        out_shape=jax.ShapeDtypeStruct((2, batch, t_len, d_model), F32),
        grid=(2, batch, d_model // lanes, n_blocks),
        in_specs=[one, one, two, two, two, two],
        out_specs=two,
        scratch_shapes=[pltpu.VMEM((n_pairs, LANES, LANES), F32)],
        compiler_params=_params(("parallel", "parallel", "parallel", "arbitrary")),
        name="rwkv_scan",
    )(r, v, lp, ap, kd, bb)


def _rwkv_out_kernel(y_ref, bo_ref, gg_ref, x_ref, mod_ref, lg_ref, lb_ref,
                     e_ref, et_ref, wo_ref, g2_ref, wr_ref, br_ref,
                     xo_ref, h2_ref, ti_ref, gt_ref):
    y = y_ref[0, 0] + y_ref[1, 0]
    e = e_ref[...]
    et = et_ref[...]
    inv_n = 1.0 / RW_HEAD_DIM
    mean = _dot(_dot(y, e, HI) * inv_n, et, HI)
    yc = y - mean
    var = _dot(_dot(yc * yc, e, HI) * inv_n, et, HI)
    yn = yc * lax.rsqrt(var + RW_GN_EPS) * lg_ref[...] + lb_ref[...]
    out = ((yn + bo_ref[0]) * gg_ref[0]).astype(BF16)
    _post_mixer(x_ref[0], _dot(out, wo_ref[...]), mod_ref[0], g2_ref[...], wr_ref[...], br_ref[...],
                xo_ref, h2_ref, ti_ref, gt_ref)


def _post_specs(batch, t_len, d_model, tb):
    tok = pl.BlockSpec((1, tb, d_model), lambda b, t: (b, t, 0))
    nar = pl.BlockSpec((1, tb, TOP_K), lambda b, t: (b, t, 0))
    shapes = [jax.ShapeDtypeStruct((batch, t_len, d_model), F32),
              jax.ShapeDtypeStruct((batch, t_len, d_model), F32),
              jax.ShapeDtypeStruct((batch, t_len, TOP_K), I32),
              jax.ShapeDtypeStruct((batch, t_len, TOP_K), F32)]
    return shapes, [tok, tok, nar, nar]


def _rwkv_out(y, bonus, gate, x, mod_l, p, norm2_g, moe_wr, moe_br, tb, n_ctx_blocks):
    batch, t_len, d_model = x.shape
    seg_e, seg_et = _segment_matrices(d_model, RW_HEAD_DIM)
    tok = pl.BlockSpec((1, tb, d_model), lambda b, t: (b, t, 0))
    tok2 = pl.BlockSpec((2, 1, tb, d_model), lambda b, t: (0, b, t, 0))
    mod_spec = pl.BlockSpec((1, N_MOD, d_model), lambda b, t: (jnp.where(t < n_ctx_blocks, batch, b), 0, 0))
    consts = [p["lnx_g"].reshape(1, d_model), p["lnx_b"].reshape(1, d_model),
              seg_e, seg_et, p["wo"].astype(BF16), norm2_g.reshape(1, d_model), *_router_params(moe_wr, moe_br)]
    shapes, specs = _post_specs(batch, t_len, d_model, tb)
    return pl.pallas_call(
        _rwkv_out_kernel,
        out_shape=shapes,
        grid=(batch, t_len // tb),
        in_specs=[tok2, tok, tok, tok, mod_spec] + [_const_spec(c.shape) for c in consts],
        out_specs=specs,
        compiler_params=_params(("parallel", "parallel")),
        name="rwkv_readout",
    )(y, bonus, gate, x, mod_l, *consts)


def _ml_up_kernel(x_ref, mod_ref, g_ref, w_ref, xm_ref, z_ref):
    mod = mod_ref[0]
    h = _adaln(x_ref[0], g_ref[...], mod[0:1], mod[1:2]).astype(BF16)
    up = _dot(h, w_ref[...])
    inner = xm_ref.shape[-1]
    xm_ref[0] = up[:, :inner]
    z_ref[0] = up[:, inner:]


def _ml_up(x, mod_l, norm_g, w_up, tb, n_ctx_blocks):
    batch, t_len, d_model = x.shape
    inner = w_up.shape[1] // 2
    tok = pl.BlockSpec((1, tb, d_model), lambda b, t: (b, t, 0))
    wide = pl.BlockSpec((1, tb, inner), lambda b, t: (b, t, 0))
    mod_spec = pl.BlockSpec((1, N_MOD, d_model), lambda b, t: (jnp.where(t < n_ctx_blocks, batch, b), 0, 0))
    out = jax.ShapeDtypeStruct((batch, t_len, inner), F32)
    return pl.pallas_call(
        _ml_up_kernel,
        out_shape=[out, out],
        grid=(batch, t_len // tb),
        in_specs=[tok, mod_spec, _const_spec((1, d_model)), _const_spec(w_up.shape)],
        out_specs=[wide, wide],
        compiler_params=_params(("parallel", "parallel")),
        name="mlstm_up",
    )(x, mod_l, norm_g.reshape(1, d_model), w_up.astype(BF16))


def _ml_feat_kernel(xm_ref, xp_ref, xn_ref, cw_ref, cb_ref, wq_ref, wk_ref, wv_ref, wg_ref, bg_ref,
                    q_ref, k_ref, v_ref, xc_ref, gt_ref, *, n_ctx_blocks):
    t = pl.program_id(1)
    n_blocks = pl.num_programs(1)
    is_ctx = t < n_ctx_blocks
    xm = xm_ref[0]
    tb, inner = xm.shape
    row = lax.broadcasted_iota(I32, (tb, 1), 0)
    col = jnp.bitwise_and(row, GRID_W - 1)
    left_ok = jnp.where(is_ctx, row, col) != 0
    right_ok = jnp.where(is_ctx, row - (tb - 1), col - (GRID_W - 1)) != 0
    up = jnp.concatenate([xp_ref[0], xm[:tb - GRID_W]], axis=0)
    down = jnp.concatenate([xm[GRID_W:], xn_ref[0]], axis=0)
    up_ok = jnp.where(is_ctx, -1, jnp.where(t == n_ctx_blocks, row, tb)) >= GRID_W
    down_ok = jnp.where(is_ctx, tb, jnp.where(t == n_blocks - 1, row, 0)) < tb - GRID_W
    up = jnp.where(up_ok, up, 0.0)
    down = jnp.where(down_ok, down, 0.0)
    cw = cw_ref[...]
    acc = jnp.zeros((tb, inner), F32) + cb_ref[...]
    for dy, src in enumerate((up, xm, down)):
        acc = acc + jnp.where(left_ok, pltpu.roll(src, 1, axis=0), 0.0) * cw[3 * dy:3 * dy + 1]
        acc = acc + src * cw[3 * dy + 1:3 * dy + 2]
        acc = acc + jnp.where(right_ok, pltpu.roll(src, tb - 1, axis=0), 0.0) * cw[3 * dy + 2:3 * dy + 3]
    xc = _silu(acc)
    xc_ref[0] = xc
    xcb = xc.astype(BF16)
    xmb = xm.astype(BF16)
    for gidx in range(inner // LANES):
        cols = slice(gidx * LANES, (gidx + 1) * LANES)
        q_ref[0, :, cols] = _dot(xcb[:, cols], wq_ref[gidx]).astype(BF16)
        k_ref[0, :, cols] = _dot(xcb[:, cols], wk_ref[gidx]).astype(BF16)
        v_ref[0, :, cols] = _dot(xmb[:, cols], wv_ref[gidx]).astype(BF16)
    pre = (_dot(q_ref[0], wg_ref[0]) + _dot(k_ref[0], wg_ref[1]) + _dot(v_ref[0], wg_ref[2]) + bg_ref[...])
    lane = lax.broadcasted_iota(I32, pre.shape, 1)
    gt_ref[0] = jnp.where(lane < 2 * ML_HEADS, pre, -_softplus(-pre))


def _block_diag_lanes(w):
    n_blk, blk, _ = w.shape
    per = LANES // blk
    w = w.reshape(n_blk // per, per, blk, blk)
    eye = jnp.eye(per, dtype=w.dtype)
    return jnp.einsum("gaio,ab->gaibo", w, eye).reshape(n_blk // per, LANES, LANES)


def _ml_feat(xm, p, tb, n_ctx_blocks):
    batch, t_len, inner = xm.shape
    hb = tb // GRID_W
    n_halo = t_len // GRID_W
    wide = pl.BlockSpec((1, tb, inner), lambda b, t: (b, t, 0))
    gate_w = jnp.concatenate([p["wi"][0], p["wi"][1], p["wf"][0], p["wf"][1]], axis=1)
    n_gate = gate_w.shape[1]
    gate_w = jnp.pad(gate_w, ((0, 0), (0, LANES - n_gate))).reshape(3, inner, LANES).astype(BF16)
    gate_b = jnp.pad(jnp.concatenate([p["bi"][0], p["bi"][1], p["bf"][0], p["bf"][1]]),
                     (0, LANES - n_gate)).reshape(1, LANES)
    consts = [p["conv_w"].reshape(9, inner), p["conv_b"].reshape(1, inner),
              _block_diag_lanes(p["wq"]).astype(BF16), _block_diag_lanes(p["wk"]).astype(BF16),
              _block_diag_lanes(p["wv"]).astype(BF16), gate_w, gate_b]
    half = jax.ShapeDtypeStruct((batch, t_len, inner), BF16)
    return pl.pallas_call(
        functools.partial(_ml_feat_kernel, n_ctx_blocks=n_ctx_blocks),
        out_shape=[half, half, half, jax.ShapeDtypeStruct((batch, t_len, inner), F32),
                   jax.ShapeDtypeStruct((batch, t_len, LANES), F32)],
        grid=(batch, t_len // tb),
        in_specs=[wide,
                  pl.BlockSpec((1, GRID_W, inner), lambda b, t: (b, jnp.maximum(t * hb - 1, 0), 0)),
                  pl.BlockSpec((1, GRID_W, inner), lambda b, t: (b, jnp.minimum((t + 1) * hb, n_halo - 1), 0)),
                  ] + [_const_spec(c.shape) for c in consts],
        out_specs=[wide, wide, wide, wide, pl.BlockSpec((1, tb, LANES), lambda b, t: (b, t, 0))],
        compiler_params=_params(("parallel", "parallel")),
        name="mlstm_features",
    )(xm, xm, xm, *consts)


def _ml_scan_kernel(q_ref, k_ref, v_ref, gc_ref, gr_ref, h_ref, c_ref, n_ref, m_ref, *, scale):
    d = pl.program_id(0)
    hd = pl.program_id(2)
    c = pl.program_id(3)

    @pl.when(c == 0)
    def _():
        c_ref[...] = jnp.zeros_like(c_ref)
        n_ref[...] = jnp.zeros_like(n_ref)
        m_ref[...] = jnp.zeros_like(m_ref)

    q = q_ref[0]
    k = k_ref[0]
    v = v_ref[0]
    ln = q.shape[0]
    sgn = jnp.where(d == 0, 1, -1)
    ri = lax.broadcasted_iota(I32, (ln, ln), 0)
    ci = lax.broadcasted_iota(I32, (ln, ln), 1)
    allowed = (ci - ri) * sgn <= 0
    tri = jnp.where(allowed, 1.0, 0.0).astype(F32)
    gcol = gc_ref[0]
    grow = gr_ref[0]
    i_idx = d * ML_HEADS + hd
    f_idx = 2 * ML_HEADS + i_idx
    lane_c = lax.broadcasted_iota(I32, gcol.shape, 1)
    sub_r = lax.broadcasted_iota(I32, grow.shape, 0)
    b_all_col = _dot(tri, gcol, HI)
    b_all_row = lax.dot_general(grow, tri, (((1,), (1,)), ((), ())), preferred_element_type=F32, precision=HI)
    b_col = jnp.sum(jnp.where(lane_c == f_idx, b_all_col, 0.0), axis=1, keepdims=True)
    f_col = jnp.sum(jnp.where(lane_c == f_idx, gcol, 0.0), axis=1, keepdims=True)
    i_col = jnp.sum(jnp.where(lane_c == i_idx, gcol, 0.0), axis=1, keepdims=True)
    b_row = jnp.sum(jnp.where(sub_r == f_idx, b_all_row, 0.0), axis=0, keepdims=True)
    i_row = jnp.sum(jnp.where(sub_r == i_idx, grow, 0.0), axis=0, keepdims=True)
    m_prev = m_ref[0:1, 0:1]
    log_inter = b_col + m_prev
    log_intra = jnp.where(allowed, b_col - b_row + i_row, -jnp.inf)
    m_t = jnp.maximum(log_inter, jnp.max(log_intra, axis=1, keepdims=True))
    w_intra = jnp.exp(log_intra - m_t)
    w_inter = jnp.exp(log_inter - m_t)
    s = _dot_nt(q, k) * scale * w_intra
    ct = c_ref[...]
    nrow = n_ref[0:1, :]
    qf = q.astype(F32)
    num = _dot(s.astype(BF16), v) + w_inter * _dot(q, ct.astype(BF16))
    den = jnp.sum(s, axis=1, keepdims=True) + w_inter * jnp.sum(qf * nrow, axis=1, keepdims=True)
    h_ref[0, 0] = num / jnp.maximum(jnp.abs(den), jnp.exp(-m_t))
    gsum = jnp.sum(f_col, axis=0, keepdims=True)
    log_loc = gsum - b_col + i_col
    m_new = jnp.maximum(gsum + m_prev, jnp.max(log_loc, axis=0, keepdims=True))
    w_loc = jnp.exp(log_loc - m_new)
    w_old = jnp.exp(gsum + m_prev - m_new)
    kw = k.astype(F32) * (scale * w_loc)
    c_ref[...] = w_old * ct + _dot_tn(kw.astype(BF16), v)
    n_ref[0:1, :] = w_old * nrow + jnp.sum(kw, axis=0, keepdims=True)
    m_ref[...] = jnp.zeros_like(m_ref) + m_new


def _ml_scan(q, k, v, gates, tb, n_ctx_blocks):
    batch, t_len, inner = q.shape
    dh = inner // ML_HEADS
    n_blocks = t_len // tb
    n_gate = 4 * ML_HEADS
    gates_t = jnp.swapaxes(gates[..., :n_gate], 1, 2)

    def head_map(d, b, h, c):
        return (b, _time_block_order(d, c, n_blocks, n_ctx_blocks), h)

    def col_map(d, b, h, c):
        return (b, _time_block_order(d, c, n_blocks, n_ctx_blocks), 0)

    def row_map(d, b, h, c):
        return (b, 0, _time_block_order(d, c, n_blocks, n_ctx_blocks))

    def out_map(d, b, h, c):
        return (d, b, _time_block_order(d, c, n_blocks, n_ctx_blocks), h)

    head = pl.BlockSpec((1, tb, dh), head_map)
    return pl.pallas_call(
        functools.partial(_ml_scan_kernel, scale=float(dh) ** -0.5),
        out_shape=jax.ShapeDtypeStruct((2, batch, t_len, inner), F32),
        grid=(2, batch, ML_HEADS, n_blocks),
        in_specs=[head, head, head, pl.BlockSpec((1, tb, LANES), col_map),
                  pl.BlockSpec((1, n_gate, tb), row_map)],
        out_specs=pl.BlockSpec((1, 1, tb, dh), out_map),
        scratch_shapes=[pltpu.VMEM((dh, dh), F32), pltpu.VMEM((8, dh), F32), pltpu.VMEM((8, LANES), F32)],
        compiler_params=_params(("parallel", "parallel", "parallel", "arbitrary")),
        name="mlstm_scan",
    )(q, k, v, gates, gates_t)


def _ml_out_kernel(hs_ref, xc_ref, z_ref, x_ref, mod_ref, ng_ref, sk_ref, wd_ref, g2_ref, wr_ref, br_ref,
                   xo_ref, h2_ref, ti_ref, gt_ref):
    hs = hs_ref[0, 0] + hs_ref[1, 0]
    inner = hs.shape[1]
    dh = inner // ML_HEADS
    parts = []
    for hd in range(ML_HEADS):
        blk = hs[:, hd * dh:(hd + 1) * dh]
        mean = jnp.mean(blk, axis=1, keepdims=True)
        cen = blk - mean
        var = jnp.mean(cen * cen, axis=1, keepdims=True)
        parts.append(cen * lax.rsqrt(var + ML_LN_EPS))
    hn = jnp.concatenate(parts, axis=1) * ng_ref[...]
    y = ((hn + sk_ref[...] * xc_ref[0]) * _silu(z_ref[0])).astype(BF16)
    _post_mixer(x_ref[0], _dot(y, wd_ref[...]), mod_ref[0], g2_ref[...], wr_ref[...], br_ref[...],
                xo_ref, h2_ref, ti_ref, gt_ref)


def _ml_out(hs, xc, z, x, mod_l, p, norm2_g, moe_wr, moe_br, tb, n_ctx_blocks):
    batch, t_len, d_model = x.shape
    inner = xc.shape[-1]
    tok = pl.BlockSpec((1, tb, d_model), lambda b, t: (b, t, 0))
    wide = pl.BlockSpec((1, tb, inner), lambda b, t: (b, t, 0))
    wide2 = pl.BlockSpec((2, 1, tb, inner), lambda b, t: (0, b, t, 0))
    mod_spec = pl.BlockSpec((1, N_MOD, d_model), lambda b, t: (jnp.where(t < n_ctx_blocks, batch, b), 0, 0))
    consts = [p["norm_g"].reshape(1, inner), p["skip"].reshape(1, inner), p["w_down"].astype(BF16),
              norm2_g.reshape(1, d_model), *_router_params(moe_wr, moe_br)]
    shapes, specs = _post_specs(batch, t_len, d_model, tb)
    return pl.pallas_call(
        _ml_out_kernel,
        out_shape=shapes,
        grid=(batch, t_len // tb),
        in_specs=[wide2, wide, wide, tok, mod_spec] + [_const_spec(c.shape) for c in consts],
        out_specs=specs,
        compiler_params=_params(("parallel", "parallel")),
        name="mlstm_readout",
    )(hs, xc, z, x, mod_l, *consts)


def _moe_plan(top_i, tok_rows, n_rows, n_exp, bm):
    n_asg = top_i.size
    assert n_rows < (1 << 16) and MOE_OUT_BUFFERS * bm <= n_rows
    flat_e = top_i.reshape(n_asg)
    onehot = (flat_e[:, None] == jnp.arange(n_exp, dtype=I32)[None, :]).astype(I32)
    csum = jnp.cumsum(onehot, axis=0)
    rank = jnp.sum(csum * onehot, axis=1) - 1
    counts = csum[-1]
    padded = (counts + bm - 1) // bm * bm
    pad_end = jnp.cumsum(padded)
    pad_start = pad_end - padded
    dest = pad_start[flat_e] + rank
    n_blk = n_asg // bm + n_exp
    asg = jnp.arange(n_asg, dtype=I32)
    real = jnp.bitwise_or(tok_rows[asg // TOP_K], jnp.left_shift(asg % TOP_K, 16))
    slots = jnp.arange((n_blk + MOE_LEAD_BLOCKS + 1) * bm, dtype=I32)
    buf = (slots // bm - MOE_LEAD_BLOCKS) % MOE_OUT_BUFFERS
    pad = jnp.bitwise_or(buf * bm + slots % bm, TOP_K << 16)
    table = pad.at[MOE_LEAD_BLOCKS * bm + dest].set(real)
    starts = jnp.arange(n_blk, dtype=I32) * bm
    block_e = jnp.minimum(jnp.sum((pad_end[None, :] <= starts[:, None]).astype(I32), axis=1), n_exp - 1)
    return block_e, table


def _moe_expert_kernel(be_ref, tab_m2, tab_m1, tab_0, tab_p1, h_hbm, wgu_ref, bgu_ref, wd_ref, bd_ref, y_hbm,
                       xbuf, obuf, wgu_bf, wd_bf, gsem, ssem, *, bm, n_tok):
    i = pl.program_id(0)
    n_blk = pl.num_programs(0)
    x_cur = lax.rem(i, 2)
    x_next = 1 - x_cur
    o_cur = lax.rem(i, MOE_OUT_BUFFERS)
    o_prev = lax.rem(i + MOE_OUT_BUFFERS - 1, MOE_OUT_BUFFERS)
    o_prev2 = lax.rem(i + MOE_OUT_BUFFERS - 2, MOE_OUT_BUFFERS)
    tables = {-2: tab_m2, -1: tab_m1, 0: tab_0, 1: tab_p1}

    def gather_copy(rel, buf, r):
        tok = jnp.bitwise_and(tables[rel][0, 0, r], 0xFFFF)
        return pltpu.make_async_copy(h_hbm.at[pl.ds(tok, 1)], xbuf.at[buf, pl.ds(r, 1)], gsem.at[buf])

    def scatter_copy(rel, buf, r):
        e = tables[rel][0, 0, r]
        dst = jnp.right_shift(e, 16) * n_tok + jnp.bitwise_and(e, 0xFFFF)
        return pltpu.make_async_copy(obuf.at[buf, pl.ds(r, 1)], y_hbm.at[pl.ds(dst, 1)], ssem.at[buf])

    def for_rows(fn):
        def body(g, carry):
            for k in range(MOE_ROW_UNROLL):
                fn(g * MOE_ROW_UNROLL + k)
            return carry
        lax.fori_loop(0, bm // MOE_ROW_UNROLL, body, 0)

    @pl.when(i == 0)
    def _():
        obuf[...] = jnp.zeros_like(obuf)
        for_rows(lambda r: gather_copy(0, x_cur, r).start())
        for_rows(lambda r: scatter_copy(-2, o_prev2, r).start())

    changed = jnp.logical_or(i == 0, be_ref[i] != be_ref[jnp.maximum(i - 1, 0)])

    @pl.when(changed)
    def _():
        wgu_bf[...] = wgu_ref[0].astype(BF16)
        wd_bf[...] = wd_ref[0].astype(BF16)

    for_rows(lambda r: gather_copy(1, x_next, r).start())
    for_rows(lambda r: scatter_copy(-1, o_prev, r).start())
    for_rows(lambda r: scatter_copy(-2, o_prev2, r).wait())
    for_rows(lambda r: gather_copy(0, x_cur, r).wait())

    x = xbuf[x_cur].astype(BF16)
    gu = _dot(x, wgu_bf[...]) + bgu_ref[0]
    ff = gu.shape[1] // 2
    gate = jnp.minimum(gu[:, :ff], SWIGLU_LIMIT)
    up = jnp.clip(gu[:, ff:], -SWIGLU_LIMIT, SWIGLU_LIMIT)
    act = gate * _sigmoid(SWIGLU_ALPHA * gate) * (up + 1.0)
    obuf[o_cur] = _dot(act.astype(BF16), wd_bf[...]) + bd_ref[0]

    @pl.when(i == n_blk - 1)
    def _():
        for_rows(lambda r: scatter_copy(0, o_cur, r).start())
        for_rows(lambda r: scatter_copy(-1, o_prev, r).wait())
        for_rows(lambda r: scatter_copy(0, o_cur, r).wait())
        for_rows(lambda r: gather_copy(1, x_next, r).wait())


def _moe_experts(h2, top_i, tok_rows, w_gu, b_gu, w_d, b_d):
    n_tok, d_model = h2.shape
    n_exp, _, ff2 = w_gu.shape
    bm = MOE_BLOCK
    ff = ff2 // 2
    assert top_i.size % bm == 0 and bm % MOE_ROW_UNROLL == 0
    block_e, table = _moe_plan(top_i, tok_rows, n_tok, n_exp, bm)
    n_blk = block_e.shape[0]
    table = table.reshape(n_blk + MOE_LEAD_BLOCKS + 1, 1, bm)

    def table_spec(rel):
        return pl.BlockSpec((1, 1, bm), lambda i, be: (i + MOE_LEAD_BLOCKS + rel, 0, 0),
                            memory_space=pltpu.SMEM)

    grid_spec = pltpu.PrefetchScalarGridSpec(
        num_scalar_prefetch=1,
        grid=(n_blk,),
        in_specs=[
            table_spec(-2), table_spec(-1), table_spec(0), table_spec(1),
            pl.BlockSpec(memory_space=pl.ANY),
            pl.BlockSpec((1, d_model, ff2), lambda i, be, *_: (be[i], 0, 0)),
            pl.BlockSpec((1, 1, ff2), lambda i, be, *_: (be[i], 0, 0)),
            pl.BlockSpec((1, ff, d_model), lambda i, be, *_: (be[i], 0, 0)),
            pl.BlockSpec((1, 1, d_model), lambda i, be, *_: (be[i], 0, 0)),
        ],
        out_specs=pl.BlockSpec(memory_space=pl.ANY),
        scratch_shapes=[
            pltpu.VMEM((2, bm, d_model), F32),
            pltpu.VMEM((MOE_OUT_BUFFERS, bm, d_model), F32),
            pltpu.VMEM((d_model, ff2), BF16),
            pltpu.VMEM((ff, d_model), BF16),
            pltpu.SemaphoreType.DMA((2,)),
            pltpu.SemaphoreType.DMA((MOE_OUT_BUFFERS,)),
        ],
    )
    return pl.pallas_call(
        functools.partial(_moe_expert_kernel, bm=bm, n_tok=n_tok),
        out_shape=jax.ShapeDtypeStruct((TOP_K * n_tok + MOE_OUT_BUFFERS * bm, d_model), F32),
        grid_spec=grid_spec,
        compiler_params=_params(("arbitrary",)),
        name="moe_experts",
    )(block_e, table, table, table, table, h2, w_gu, b_gu.reshape(n_exp, 1, ff2), w_d,
      b_d.reshape(n_exp, 1, d_model))


def _moe_combine_kernel(x_ref, gt_ref, mod_ref, fg_ref, *rest, final):
    y_refs, o_ref = rest[:TOP_K], rest[TOP_K]
    gates = gt_ref[0]
    f = jnp.zeros(x_ref.shape[1:], F32)
    for j in range(TOP_K):
        f = f + gates[:, j:j + 1] * y_refs[j][...]
    xn = x_ref[0] + mod_ref[0][5:6] * f
    if final:
        xn = xn * lax.rsqrt(jnp.mean(xn * xn, axis=-1, keepdims=True) + NORM_EPS) * fg_ref[...]
    o_ref[0] = xn


def _moe_combine(x, y4, gates, mod_l, final_g, tb, n_ctx_blocks, final):
    batch, t_len, d_model = x.shape
    off = n_ctx_blocks if final else 0
    t_out = t_len - off * tb
    per_b = t_len // tb
    per_j = batch * per_b
    mod_spec = pl.BlockSpec((1, N_MOD, d_model),
                            lambda b, t: (jnp.where(t + off < n_ctx_blocks, batch, b), 0, 0))

    def choice_spec(j):
        return pl.BlockSpec((tb, d_model), lambda b, t: (j * per_j + b * per_b + t + off, 0))

    return pl.pallas_call(
        functools.partial(_moe_combine_kernel, final=final),
        out_shape=jax.ShapeDtypeStruct((batch, t_out, d_model), F32),
        grid=(batch, t_out // tb),
        in_specs=[pl.BlockSpec((1, tb, d_model), lambda b, t: (b, t + off, 0)),
                  pl.BlockSpec((1, tb, TOP_K), lambda b, t: (b, t + off, 0)),
                  mod_spec, _const_spec((1, d_model))] + [choice_spec(j) for j in range(TOP_K)],
        out_specs=pl.BlockSpec((1, tb, d_model), lambda b, t: (b, t, 0)),
        compiler_params=_params(("parallel", "parallel")),
        name="moe_combine",
    )(x, gates, mod_l, final_g.reshape(1, d_model), *([y4] * TOP_K))


def _moe_layer(x, h2, top_i, gates, mod_l, moe_p, final_g, tb, n_ctx_blocks, last):
    batch, t_len, d_model = x.shape
    tok_rows = jnp.arange(batch * t_len, dtype=I32).reshape(batch, t_len)
    if last:
        n_ctx = n_ctx_blocks * tb
        top_i, tok_rows = top_i[:, n_ctx:], tok_rows[:, n_ctx:]
    y4 = _moe_experts(h2.reshape(batch * t_len, d_model), top_i.reshape(-1, TOP_K), tok_rows.reshape(-1),
                      *moe_p)
    return _moe_combine(x, y4, gates, mod_l, final_g, tb, n_ctx_blocks, last)


def kernel(x, c, ctx, c_ctx, ada_w, ada_b, norm1_g, norm2_g, final_g, rw_mu, rw_wr, rw_wk, rw_wv, rw_wo, rw_w0, rw_w1, rw_w2, rw_a0, rw_a1, rw_a2, rw_g1, rw_g2, rw_kk, rw_ka, rw_rk, rw_lnx_g, rw_lnx_b, rw_v0, rw_v1, rw_v2, ml_w_up, ml_conv_w, ml_conv_b, ml_wq, ml_wk, ml_wv, ml_wi, ml_bi, ml_wf, ml_bf, ml_norm_g, ml_skip, ml_w_down, moe_wr, moe_br, moe_w_gu, moe_b_gu, moe_w_d, moe_b_d):
    batch, seq, d_model = x.shape
    n_ctx = ctx.shape[1]
    depth = ada_w.shape[0]
    tb = math.gcd(MAX_TIME_BLOCK, n_ctx, seq)
    assert tb % GRID_W == 0 and n_ctx == tb and d_model % (2 * RW_HEAD_DIM) == 0
    n_ctx_blocks = n_ctx // tb

    mod = _modulation(c, c_ctx, ada_w, ada_b)
    xs = jnp.concatenate([ctx, x], axis=1)
    v_first = None
    for i in range(depth):
        last = i == depth - 1
        j = i // 2
        mod_l = mod[i]
        if i % 2 == 0:
            p = dict(mu=rw_mu[j], wr=rw_wr[j], wk=rw_wk[j], wv=rw_wv[j], wo=rw_wo[j], w0=rw_w0[j],
                     w1=rw_w1[j], w2=rw_w2[j], a0=rw_a0[j], a1=rw_a1[j], a2=rw_a2[j], g1=rw_g1[j],
                     g2=rw_g2[j], kk=rw_kk[j], ka=rw_ka[j], rk=rw_rk[j], lnx_g=rw_lnx_g[j],
                     lnx_b=rw_lnx_b[j])
            vres = None if j == 0 else (rw_v0[j - 1], rw_v1[j - 1], rw_v2[j - 1])
            r, v, gate, bonus, lp, ap, kd, bb = _rwkv_pre(xs, mod_l, norm1_g[i], p, vres, v_first,
                                                          tb, n_ctx_blocks)
            if j == 0:
                v_first = v
            y = _rwkv_scan(r, v, lp, ap, kd, bb, tb, n_ctx_blocks)
            xs, h2, top_i, gates = _rwkv_out(y, bonus, gate, xs, mod_l, p, norm2_g[i],
                                             moe_wr[i], moe_br[i], tb, n_ctx_blocks)
        else:
            p = dict(conv_w=ml_conv_w[j], conv_b=ml_conv_b[j], wq=ml_wq[j], wk=ml_wk[j], wv=ml_wv[j],
                     wi=ml_wi[j], bi=ml_bi[j], wf=ml_wf[j], bf=ml_bf[j], norm_g=ml_norm_g[j],
                     skip=ml_skip[j], w_down=ml_w_down[j])
            xm, z = _ml_up(xs, mod_l, norm1_g[i], ml_w_up[j], tb, n_ctx_blocks)
            q, k, v, xc, gts = _ml_feat(xm, p, tb, n_ctx_blocks)
            hs = _ml_scan(q, k, v, gts, tb, n_ctx_blocks)
            xs, h2, top_i, gates = _ml_out(hs, xc, z, xs, mod_l, p, norm2_g[i],
                                           moe_wr[i], moe_br[i], tb, n_ctx_blocks)
        moe_p = (moe_w_gu[i], moe_b_gu[i], moe_w_d[i], moe_b_d[i])
        xs = _moe_layer(xs, h2, top_i, gates, mod_l, moe_p, final_g, tb, n_ctx_blocks, last)
    return xs
```

```python
import functools
import math

import jax
import jax.numpy as jnp
from jax import lax
from jax.experimental import pallas as pl
from jax.experimental.pallas import tpu as pltpu

F32 = jnp.float32
BF16 = jnp.bfloat16
I32 = jnp.int32
HI = lax.Precision.HIGHEST

GRID_W = 64
N_MOD = 6
NORM_EPS = 1e-6
RW_HEAD_DIM = 64
RW_GN_EPS = 64e-5
RW_CHUNK = 64
ML_HEADS = 4
ML_QKV_BLOCK = 4
ML_LN_EPS = 1e-6
TOP_K = 4
SWIGLU_LIMIT = 7.0
SWIGLU_ALPHA = 1.702
MOE_BLOCK = 256
MOE_OUT_BUFFERS = 3
MOE_LEAD_BLOCKS = 2
MOE_ROW_UNROLL = 8
LANES = 128
MXU_WIDTH = 256
MAX_TIME_BLOCK = 256
VMEM_LIMIT = 56 * 1024 * 1024


def _dot(a, b, precision=None):
    return jnp.dot(a, b, preferred_element_type=F32, precision=precision)


def _dot_nt(a, b):
    return lax.dot_general(a, b, (((1,), (1,)), ((), ())), preferred_element_type=F32)


def _dot_tn(a, b, precision=None):
    return lax.dot_general(a, b, (((0,), (0,)), ((), ())), preferred_element_type=F32,
                           precision=precision)


def _sigmoid(x):
    return 1.0 / (1.0 + jnp.exp(-x))


def _softplus(x):
    return jnp.maximum(x, 0.0) + jnp.log(1.0 + jnp.exp(-jnp.abs(x)))


def _silu(x):
    return x * _sigmoid(x)


def _adaln(x, g, shift, scale):
    y = x * lax.rsqrt(jnp.mean(x * x, axis=-1, keepdims=True) + NORM_EPS) * g
    return y * (1.0 + scale) + shift


def _const_spec(shape):
    nd = len(shape)
    return pl.BlockSpec(shape, lambda *_: (0,) * nd, pipeline_mode=pl.Buffered(1))


def _params(sem, vmem=VMEM_LIMIT):
    return pltpu.CompilerParams(dimension_semantics=sem, vmem_limit_bytes=vmem)


def _time_block_order(d, c, n_blocks, n_ctx_blocks):
    rev = jnp.where(c < n_ctx_blocks, n_ctx_blocks - 1 - c, n_blocks - 1 - (c - n_ctx_blocks))
    return jnp.where(d == 0, c, rev)


def _mod_kernel(c_ref, w_ref, b_ref, o_ref):
    o_ref[0] = _dot(_silu(c_ref[...]), w_ref[0], HI) + b_ref[0]


def _modulation(c, c_ctx, ada_w, ada_b):
    depth, d_model, _ = ada_w.shape
    batch = c.shape[0]
    rows = -(-(batch + 1) // 8) * 8
    cc = jnp.concatenate([c, c_ctx[None], jnp.zeros((rows - batch - 1, d_model), F32)], axis=0)
    out = pl.pallas_call(
        _mod_kernel,
        out_shape=jax.ShapeDtypeStruct((depth, rows, N_MOD * d_model), F32),
        grid=(depth, N_MOD),
        in_specs=[
            pl.BlockSpec((rows, d_model), lambda l, j: (0, 0)),
            pl.BlockSpec((1, d_model, d_model), lambda l, j: (l, 0, j)),
            pl.BlockSpec((1, 1, d_model), lambda l, j: (l, 0, j)),
        ],
        out_specs=pl.BlockSpec((1, rows, d_model), lambda l, j: (l, 0, j)),
        compiler_params=_params(("parallel", "parallel")),
        name="adaln_modulation",
    )(cc, ada_w, ada_b.reshape(depth, 1, N_MOD * d_model))
    return out.reshape(depth, rows, N_MOD, d_model)


def _neighbours(h, h_prev, h_next, is_ctx, first_lat, last_lat):
    tb = h.shape[0]
    row = lax.broadcasted_iota(I32, (tb, 1), 0)
    col = jnp.bitwise_and(row, GRID_W - 1)
    left_ok = jnp.where(is_ctx, row, col) != 0
    right_ok = jnp.where(is_ctx, row - (tb - 1), col - (GRID_W - 1)) != 0
    left = jnp.where(left_ok, pltpu.roll(h, 1, axis=0), 0.0)
    right = jnp.where(right_ok, pltpu.roll(h, tb - 1, axis=0), 0.0)
    up = jnp.concatenate([h_prev, h[:tb - GRID_W]], axis=0)
    down = jnp.concatenate([h[GRID_W:], h_next], axis=0)
    up_ok = jnp.where(first_lat, row, tb) >= GRID_W
    down_ok = jnp.where(last_lat, row, 0) < tb - GRID_W
    up = jnp.where(up_ok, up, 0.0)
    down = jnp.where(down_ok, down, 0.0)
    return left, right, up, down


def _chunk_cumsum(x, reverse):
    n = x.shape[0]
    pos = jnp.bitwise_and(lax.broadcasted_iota(I32, (n, 1), 0), RW_CHUNK - 1)
    step = 1
    while step < RW_CHUNK:
        if reverse:
            x = x + jnp.where(pos < RW_CHUNK - step, pltpu.roll(x, n - step, axis=0), 0.0)
        else:
            x = x + jnp.where(pos >= step, pltpu.roll(x, step, axis=0), 0.0)
        step *= 2
    return x


def _segment_allsum(x, seg):
    width = x.shape[-1]
    shift = seg.bit_length() - 1
    assert seg == 1 << shift and MXU_WIDTH % seg == 0 and width % MXU_WIDTH == 0
    r = jnp.right_shift(lax.broadcasted_iota(I32, (MXU_WIDTH, MXU_WIDTH), 0), shift)
    c = jnp.right_shift(lax.broadcasted_iota(I32, (MXU_WIDTH, MXU_WIDTH), 1), shift)
    ones = jnp.where(r == c, 1.0, 0.0).astype(BF16)
    hi = x.astype(BF16)
    rest = x - hi.astype(F32)
    mid = rest.astype(BF16)
    lo = (rest - mid.astype(F32)).astype(BF16)
    groups = []
    for g in range(width // MXU_WIDTH):
        cols = slice(g * MXU_WIDTH, (g + 1) * MXU_WIDTH)
        groups.append(_dot(hi[:, cols], ones) + _dot(mid[:, cols], ones) + _dot(lo[:, cols], ones))
    return jnp.concatenate(groups, axis=-1)


def _route_top_k(logits):
    n, n_exp = logits.shape
    lane = lax.broadcasted_iota(I32, (n, n_exp), 1).astype(F32)
    lane_k = lax.broadcasted_iota(I32, (n, TOP_K), 1)
    work = logits
    vals, idxs = [], []
    for _ in range(TOP_K):
        m = jnp.max(work, axis=-1, keepdims=True)
        idx = jnp.min(jnp.where(work == m, lane, float(n_exp)), axis=-1, keepdims=True)
        vals.append(m)
        idxs.append(idx)
        work = jnp.where(lane == idx, -jnp.inf, work)
    es = [jnp.exp(v - vals[0]) for v in vals]
    den = es[0] + es[1] + es[2] + es[3]
    top_i = jnp.zeros((n, TOP_K), I32)
    gates = jnp.zeros((n, TOP_K), F32)
    for j in range(TOP_K):
        top_i = jnp.where(lane_k == j, idxs[j].astype(I32), top_i)
        gates = jnp.where(lane_k == j, es[j] / den, gates)
    return top_i, gates


def _router_params(moe_wr, moe_br):
    n_exp = moe_wr.shape[-1]
    assert n_exp <= LANES
    wr = jnp.pad(moe_wr, ((0, 0), (0, LANES - n_exp)))
    br = jnp.pad(moe_br, (0, LANES - n_exp), constant_values=-1e30)
    return wr, br.reshape(1, LANES)


def _post_mixer(x, y, mod, g2, wr, br, xo_ref, h2_ref, ti_ref, gt_ref):
    xn = x + mod[2:3] * y
    xo_ref[0] = xn
    h2 = _adaln(xn, g2, mod[3:4], mod[4:5])
    h2_ref[0] = h2
    top_i, gates = _route_top_k(_dot(h2, wr, HI) + br)
    ti_ref[0] = top_i
    gt_ref[0] = gates


def _rwkv_pre_kernel(*refs, has_vres, n_ctx_blocks):
    (x_ref, xp_ref, xn_ref, mod_ref, g_ref, mu_ref, wr_ref, wk_ref, wv_ref, wlw_ref, wla_ref,
     wg1_ref, w2_ref, a2_ref, g2_ref, w0_ref, a0_ref, kkp_ref, kap_ref, rk_ref) = refs[:20]
    if has_vres:
        vf_ref, v0_ref, v1_ref, v2_ref = refs[20:24]
        outs = refs[24:]
    else:
        outs = refs[20:]
    r_ref, v_ref, gg_ref, bo_ref, lp_ref, ap_ref, kd_ref, bb_ref = outs

    t = pl.program_id(1)
    n_blocks = pl.num_programs(1)
    is_ctx = t < n_ctx_blocks
    mod = mod_ref[0]
    g = g_ref[...]
    h = _adaln(x_ref[0], g, mod[0:1], mod[1:2])
    hp = _adaln(xp_ref[0], g, mod[0:1], mod[1:2])
    hn = _adaln(xn_ref[0], g, mod[0:1], mod[1:2])
    left, right, up, down = _neighbours(h, hp, hn, is_ctx, t == n_ctx_blocks, t == n_blocks - 1)
    d_model = h.shape[1]
    q = d_model // 4
    lane = lax.broadcasted_iota(I32, (1, d_model), 1)
    q2 = jnp.where(is_ctx, left, up)
    q3 = jnp.where(is_ctx, right, down)
    shifted = jnp.where(lane < q, left, jnp.where(lane < 2 * q, right, jnp.where(lane < 3 * q, q2, q3)))
    xx = shifted - h
    mu = mu_ref[...]

    def mix(j):
        return (h + xx * mu[j:j + 1]).astype(BF16)

    r = _dot(mix(0), wr_ref[...])
    k = _dot(mix(2), wk_ref[...])
    xv = mix(3)
    v = _dot(xv, wv_ref[...])
    if has_vres:
        lora = _dot(_dot(xv, v1_ref[...]).astype(BF16), v2_ref[...])
        v = v + (vf_ref[0] - v) * _sigmoid(v0_ref[...] + lora)
    gate = _dot(_sigmoid(_dot(mix(5), wg1_ref[...])).astype(BF16), g2_ref[...])
    kx = k * kkp_ref[...]
    kk = kx / jnp.maximum(jnp.sqrt(_segment_allsum(kx * kx, RW_HEAD_DIM)), 1e-12)
    r_ref[0] = r
    v_ref[0] = v
    gg_ref[0] = gate
    tw = jnp.tanh(_dot(mix(1), wlw_ref[...])).astype(BF16)
    la = _dot(mix(4), wla_ref[...]).astype(BF16)
    kap = kap_ref[...]
    kd_sum = None
    for d in range(2):
        logw = -_softplus(-(w0_ref[d:d + 1] + _dot(tw, w2_ref[d]))) - 0.5
        decay_rate = jnp.exp(logw)
        lp_ref[d, 0] = _chunk_cumsum(-decay_rate, reverse=d == 1)
        ap_ref[d, 0] = -kk * jnp.exp(decay_rate)
        a = _sigmoid(a0_ref[d:d + 1] + _dot(la, a2_ref[d]))
        kd = k * (1.0 + (a - 1.0) * kap)
        kd_ref[d, 0] = kd
        bb_ref[d, 0] = kk * a
        kd_sum = kd if kd_sum is None else kd_sum + kd
    bo_ref[0] = _segment_allsum(r * rk_ref[...] * kd_sum, RW_HEAD_DIM) * v


def _rwkv_pre(x, mod_l, norm_g, p, vres, v_first, tb, n_ctx_blocks):
    batch, t_len, d_model = x.shape
    n_blocks = t_len // tb
    hb = tb // GRID_W
    n_halo = t_len // GRID_W
    lora = p["w1"].shape[-1]
    assert 2 * lora == LANES and p["g1"].shape[-1] == LANES
    zeros = jnp.zeros((lora, d_model), F32)
    w2p = jnp.stack([jnp.concatenate([p["w2"][0], zeros]), jnp.concatenate([zeros, p["w2"][1]])])
    a2p = jnp.stack([jnp.concatenate([p["a2"][0], zeros]), jnp.concatenate([zeros, p["a2"][1]])])
    has_vres = vres is not None

    tok = pl.BlockSpec((1, tb, d_model), lambda b, t: (b, t, 0))
    tok2 = pl.BlockSpec((2, 1, tb, d_model), lambda b, t: (0, b, t, 0))
    in_specs = [
        tok,
        pl.BlockSpec((1, GRID_W, d_model), lambda b, t: (b, jnp.maximum(t * hb - 1, 0), 0)),
        pl.BlockSpec((1, GRID_W, d_model), lambda b, t: (b, jnp.minimum((t + 1) * hb, n_halo - 1), 0)),
        pl.BlockSpec((1, N_MOD, d_model), lambda b, t: (jnp.where(t < n_ctx_blocks, batch, b), 0, 0)),
    ]
    args = [x, x, x, mod_l]
    consts = [
        norm_g.reshape(1, d_model), p["mu"],
        p["wr"].astype(BF16), p["wk"].astype(BF16), p["wv"].astype(BF16),
        jnp.concatenate([p["w1"][0], p["w1"][1]], axis=1).astype(BF16),
        jnp.concatenate([p["a1"][0], p["a1"][1]], axis=1).astype(BF16),
        p["g1"].astype(BF16), w2p.astype(BF16), a2p.astype(BF16), p["g2"].astype(BF16),
        p["w0"], p["a0"], p["kk"].reshape(1, d_model), p["ka"].reshape(1, d_model),
        p["rk"].reshape(1, d_model),
    ]
    in_specs += [_const_spec(c.shape) for c in consts]
    args += consts
    if has_vres:
        v0, v1, v2 = vres
        extra = [v0.reshape(1, d_model), v1.astype(BF16), v2.astype(BF16)]
        in_specs += [tok] + [_const_spec(c.shape) for c in extra]
        args += [v_first] + extra
    one = jax.ShapeDtypeStruct((batch, t_len, d_model), F32)
    two = jax.ShapeDtypeStruct((2, batch, t_len, d_model), F32)
    return pl.pallas_call(
        functools.partial(_rwkv_pre_kernel, has_vres=has_vres, n_ctx_blocks=n_ctx_blocks),
        out_shape=[one, one, one, one, two, two, two, two],
        grid=(batch, n_blocks),
        in_specs=in_specs,
        out_specs=[tok, tok, tok, tok, tok2, tok2, tok2, tok2],
        compiler_params=_params(("parallel", "parallel")),
        name="rwkv_features",
    )(*args)


def _rwkv_scan_kernel(r_ref, v_ref, lp_ref, ap_ref, kd_ref, bb_ref, y_ref, s_ref, *, n_pairs, tb):
    d = pl.program_id(0)
    c = pl.program_id(3)
    ch = RW_CHUNK
    two = 2 * ch

    @pl.when(c == 0)
    def _():
        s_ref[...] = jnp.zeros_like(s_ref)

    sgn = jnp.where(d == 0, 1, -1)
    r2 = lax.broadcasted_iota(I32, (two, two), 0)
    c2 = lax.broadcasted_iota(I32, (two, two), 1)
    same = jnp.right_shift(r2, 6) == jnp.right_shift(c2, 6)
    delta = (jnp.bitwise_and(c2, ch - 1) - jnp.bitwise_and(r2, ch - 1)) * sgn
    before = jnp.logical_and(same, delta < 0)
    before_eq = jnp.logical_and(same, delta <= 0)
    head0 = lax.broadcasted_iota(I32, (ch, LANES), 1) < RW_HEAD_DIM
    head0w = lax.broadcasted_iota(I32, (ch, 2 * LANES), 1)
    head0w = jnp.bitwise_and(head0w, LANES - 1) < RW_HEAD_DIM
    n_sub = tb // ch
    pairs = range(n_pairs)

    def stack2(x):
        return jnp.concatenate([x, x], axis=0)

    def pick(x, mask):
        return jnp.where(mask, x[:ch], x[ch:])

    def sub_chunk(j, carry):
        jj = jnp.where(d == 0, j, n_sub - 1 - j)
        start = pl.multiple_of(jj * ch, ch)
        rows = pl.ds(start, ch)
        last = pl.ds(pl.multiple_of(start + jnp.where(d == 0, ch - 8, 0), 8), 8)
        cols = [slice(p * LANES, (p + 1) * LANES) for p in pairs]

        v, vv, r_t, a_t, b_p, k_p, m_b, m_k, p_end = [], [], [], [], [], [], [], [], []
        for p in pairs:
            lp = lp_ref[0, 0, rows, cols[p]]
            e_in = jnp.exp(lp)
            e_out = jnp.exp(-lp)
            lp_edge = lp_ref[0, 0, last, cols[p]]
            p_last = jnp.exp(jnp.where(d == 0, lp_edge[7:8], lp_edge[0:1]))
            p_end.append(p_last)
            at = ap_ref[0, 0, rows, cols[p]] * e_in
            rt = r_ref[0, rows, cols[p]] * e_in
            bt = bb_ref[0, 0, rows, cols[p]] * e_out
            kt = kd_ref[0, 0, rows, cols[p]] * e_out
            vp = v_ref[0, rows, cols[p]]
            lhs = jnp.concatenate([jnp.where(head0, at, 0.0), jnp.where(head0, 0.0, at),
                                   jnp.where(head0, rt, 0.0), jnp.where(head0, 0.0, rt)],
                                  axis=0).astype(BF16)
            m_b.append(_dot_nt(lhs, stack2(bt).astype(BF16)))
            m_k.append(_dot_nt(lhs, stack2(kt).astype(BF16)))
            v.append(vp)
            vv.append(stack2(vp).astype(BF16))
            r_t.append(rt)
            a_t.append(at)
            b_p.append(bt * p_last)
            k_p.append(kt * p_last)
        n_pow = [jnp.where(before, m_b[p][:two], 0.0).astype(BF16) for p in pairs]
        a_k = [jnp.where(before, m_k[p][:two], 0.0).astype(BF16) for p in pairs]
        r_b = [jnp.where(before_eq, m_b[p][two:], 0.0).astype(BF16) for p in pairs]
        r_k = [jnp.where(before_eq, m_k[p][two:], 0.0).astype(BF16) for p in pairs]
        z = [jnp.concatenate([stack2(a_t[p]), _dot(a_k[p], vv[p])], axis=1) for p in pairs]
        z = [z[p] + _dot(n_pow[p], z[p].astype(BF16)) for p in pairs]
        for _ in range(int(math.log2(ch)) - 1):
            n_pow = [_dot(n_pow[p], n_pow[p]).astype(BF16) for p in pairs]
            z = [z[p] + _dot(n_pow[p], z[p].astype(BF16)) for p in pairs]
        q = [_dot(r_b[p], z[p].astype(BF16)) for p in pairs]
        y_k = [_dot(r_k[p], vv[p]) for p in pairs]
        w_u = [pick(z[p], head0w) for p in pairs]
        r_y = [pick(q[p], head0w) for p in pairs]
        lhs_s = [jnp.concatenate([w_u[p][:, :LANES], r_t[p] + r_y[p][:, :LANES]], axis=0).astype(BF16)
                 for p in pairs]
        u_0 = [w_u[p][:, LANES:] for p in pairs]
        y_0 = [r_y[p][:, LANES:] + pick(y_k[p], head0) for p in pairs]
        bk_p = [jnp.concatenate([b_p[p], k_p[p]], axis=0).astype(BF16) for p in pairs]

        s_old = [s_ref[p] for p in pairs]
        g = [_dot_nt(lhs_s[p], s_old[p].astype(BF16)) for p in pairs]
        u = [g[p][:ch] + u_0[p] for p in pairs]
        for p in pairs:
            y_ref[0, 0, rows, cols[p]] = g[p][ch:] + y_0[p]
        d_s = [_dot_tn(jnp.concatenate([u[p], v[p]], axis=0).astype(BF16), bk_p[p]) for p in pairs]
        for p in pairs:
            s_ref[p] = s_old[p] * p_end[p] + jnp.where(same, d_s[p], 0.0)
        return carry

    lax.fori_loop(0, n_sub, sub_chunk, 0)


def _rwkv_scan(r, v, lp, ap, kd, bb, tb, n_ctx_blocks):
    batch, t_len, d_model = r.shape
    n_blocks = t_len // tb
    lanes = min(8 * LANES, d_model)
    n_pairs = lanes // LANES

    def one_map(d, b, g, c):
        return (b, _time_block_order(d, c, n_blocks, n_ctx_blocks), g)

    def two_map(d, b, g, c):
        return (d, b, _time_block_order(d, c, n_blocks, n_ctx_blocks), g)

    one = pl.BlockSpec((1, tb, lanes), one_map)
    two = pl.BlockSpec((1, 1, tb, lanes), two_map)
    return pl.pallas_call(
        functools.partial(_rwkv_scan_kernel, n_pairs=n_pairs, tb=tb),
        out_shape=jax.ShapeDtypeStruct((2, batch, t_len, d_model), F32),
        grid=(2, batch, d_model // lanes, n_blocks),
        in_specs=[one, one, two, two, two, two],
        out_specs=two,
        scratch_shapes=[pltpu.VMEM((n_pairs, LANES, LANES), F32)],
        compiler_params=_params(("parallel", "parallel", "parallel", "arbitrary")),
        name="rwkv_scan",
    )(r, v, lp, ap, kd, bb)


def _rwkv_out_kernel(y_ref, bo_ref, gg_ref, x_ref, mod_ref, lg_ref, lb_ref,
                     wo_ref, g2_ref, wr_ref, br_ref,
                     xo_ref, h2_ref, ti_ref, gt_ref):
    y = y_ref[0, 0] + y_ref[1, 0]
    inv_n = 1.0 / RW_HEAD_DIM
    yc = y - _segment_allsum(y, RW_HEAD_DIM) * inv_n
    var = _segment_allsum(yc * yc, RW_HEAD_DIM) * inv_n
    yn = yc * lax.rsqrt(var + RW_GN_EPS) * lg_ref[...] + lb_ref[...]
    out = ((yn + bo_ref[0]) * gg_ref[0]).astype(BF16)
    _post_mixer(x_ref[0], _dot(out, wo_ref[...]), mod_ref[0], g2_ref[...], wr_ref[...], br_ref[...],
                xo_ref, h2_ref, ti_ref, gt_ref)


def _post_specs(batch, t_len, d_model, tb):
    tok = pl.BlockSpec((1, tb, d_model), lambda b, t: (b, t, 0))
    nar = pl.BlockSpec((1, tb, TOP_K), lambda b, t: (b, t, 0))
    shapes = [jax.ShapeDtypeStruct((batch, t_len, d_model), F32),
              jax.ShapeDtypeStruct((batch, t_len, d_model), F32),
              jax.ShapeDtypeStruct((batch, t_len, TOP_K), I32),
              jax.ShapeDtypeStruct((batch, t_len, TOP_K), F32)]
    return shapes, [tok, tok, nar, nar]


def _rwkv_out(y, bonus, gate, x, mod_l, p, norm2_g, moe_wr, moe_br, tb, n_ctx_blocks):
    batch, t_len, d_model = x.shape
    tok = pl.BlockSpec((1, tb, d_model), lambda b, t: (b, t, 0))
    tok2 = pl.BlockSpec((2, 1, tb, d_model), lambda b, t: (0, b, t, 0))
    mod_spec = pl.BlockSpec((1, N_MOD, d_model), lambda b, t: (jnp.where(t < n_ctx_blocks, batch, b), 0, 0))
    consts = [p["lnx_g"].reshape(1, d_model), p["lnx_b"].reshape(1, d_model), p["wo"].astype(BF16), norm2_g.reshape(1, d_model), *_router_params(moe_wr, moe_br)]
    shapes, specs = _post_specs(batch, t_len, d_model, tb)
    return pl.pallas_call(
        _rwkv_out_kernel,
        out_shape=shapes,
        grid=(batch, t_len // tb),
        in_specs=[tok2, tok, tok, tok, mod_spec] + [_const_spec(c.shape) for c in consts],
        out_specs=specs,
        compiler_params=_params(("parallel", "parallel")),
        name="rwkv_readout",
    )(y, bonus, gate, x, mod_l, *consts)


def _ml_up_kernel(x_ref, mod_ref, g_ref, w_ref, xm_ref, z_ref):
    mod = mod_ref[0]
    h = _adaln(x_ref[0], g_ref[...], mod[0:1], mod[1:2]).astype(BF16)
    up = _dot(h, w_ref[...])
    inner = xm_ref.shape[-1]
    xm_ref[0] = up[:, :inner]
    z_ref[0] = up[:, inner:]


def _ml_up(x, mod_l, norm_g, w_up, tb, n_ctx_blocks):
    batch, t_len, d_model = x.shape
    inner = w_up.shape[1] // 2
    tok = pl.BlockSpec((1, tb, d_model), lambda b, t: (b, t, 0))
    wide = pl.BlockSpec((1, tb, inner), lambda b, t: (b, t, 0))
    mod_spec = pl.BlockSpec((1, N_MOD, d_model), lambda b, t: (jnp.where(t < n_ctx_blocks, batch, b), 0, 0))
    out = jax.ShapeDtypeStruct((batch, t_len, inner), F32)
    return pl.pallas_call(
        _ml_up_kernel,
        out_shape=[out, out],
        grid=(batch, t_len // tb),
        in_specs=[tok, mod_spec, _const_spec((1, d_model)), _const_spec(w_up.shape)],
        out_specs=[wide, wide],
        compiler_params=_params(("parallel", "parallel")),
        name="mlstm_up",
    )(x, mod_l, norm_g.reshape(1, d_model), w_up.astype(BF16))


def _ml_feat_kernel(xm_ref, xp_ref, xn_ref, cw_ref, cb_ref, wq_ref, wk_ref, wv_ref, wg_ref, bg_ref,
                    q_ref, k_ref, v_ref, xc_ref, gt_ref, gs_ref, *, n_ctx_blocks):
    t = pl.program_id(1)
    n_blocks = pl.num_programs(1)
    is_ctx = t < n_ctx_blocks
    xm = xm_ref[0]
    tb, inner = xm.shape
    row = lax.broadcasted_iota(I32, (tb, 1), 0)
    col = jnp.bitwise_and(row, GRID_W - 1)
    left_ok = jnp.where(is_ctx, row, col) != 0
    right_ok = jnp.where(is_ctx, row - (tb - 1), col - (GRID_W - 1)) != 0
    up = jnp.concatenate([xp_ref[0], xm[:tb - GRID_W]], axis=0)
    down = jnp.concatenate([xm[GRID_W:], xn_ref[0]], axis=0)
    up_ok = jnp.where(is_ctx, -1, jnp.where(t == n_ctx_blocks, row, tb)) >= GRID_W
    down_ok = jnp.where(is_ctx, tb, jnp.where(t == n_blocks - 1, row, 0)) < tb - GRID_W
    up = jnp.where(up_ok, up, 0.0)
    down = jnp.where(down_ok, down, 0.0)
    cw = cw_ref[...]
    acc = jnp.zeros((tb, inner), F32) + cb_ref[...]
    for dy, src in enumerate((up, xm, down)):
        acc = acc + jnp.where(left_ok, pltpu.roll(src, 1, axis=0), 0.0) * cw[3 * dy:3 * dy + 1]
        acc = acc + src * cw[3 * dy + 1:3 * dy + 2]
        acc = acc + jnp.where(right_ok, pltpu.roll(src, tb - 1, axis=0), 0.0) * cw[3 * dy + 2:3 * dy + 3]
    xc = _silu(acc)
    xc_ref[0] = xc
    xcb = xc.astype(BF16)
    xmb = xm.astype(BF16)
    for gidx in range(inner // LANES):
        cols = slice(gidx * LANES, (gidx + 1) * LANES)
        q_ref[0, :, cols] = _dot(xcb[:, cols], wq_ref[gidx]).astype(BF16)
        k_ref[0, :, cols] = _dot(xcb[:, cols], wk_ref[gidx]).astype(BF16)
        v_ref[0, :, cols] = _dot(xmb[:, cols], wv_ref[gidx]).astype(BF16)
    pre = (_dot(q_ref[0], wg_ref[0]) + _dot(k_ref[0], wg_ref[1]) + _dot(v_ref[0], wg_ref[2]) + bg_ref[...])
    lane = lax.broadcasted_iota(I32, pre.shape, 1)
    gates = jnp.where(lane < 2 * ML_HEADS, pre, -_softplus(-pre))
    gt_ref[0] = gates
    fwd = bwd = gates
    step = 1
    while step < tb:
        fwd = fwd + jnp.where(row >= step, pltpu.roll(fwd, step, axis=0), 0.0)
        bwd = bwd + jnp.where(row < tb - step, pltpu.roll(bwd, tb - step, axis=0), 0.0)
        step *= 2
    backward_lane = jnp.bitwise_and(jnp.right_shift(lane, 2), 1) == 1
    gs_ref[0] = jnp.where(backward_lane, bwd, fwd)


def _block_diag_lanes(w):
    n_blk, blk, _ = w.shape
    per = LANES // blk
    w = w.reshape(n_blk // per, per, blk, blk)
    eye = jnp.eye(per, dtype=w.dtype)
    return jnp.einsum("gaio,ab->gaibo", w, eye).reshape(n_blk // per, LANES, LANES)


def _ml_feat(xm, p, tb, n_ctx_blocks):
    batch, t_len, inner = xm.shape
    hb = tb // GRID_W
    n_halo = t_len // GRID_W
    wide = pl.BlockSpec((1, tb, inner), lambda b, t: (b, t, 0))
    gate_w = jnp.concatenate([p["wi"][0], p["wi"][1], p["wf"][0], p["wf"][1]], axis=1)
    n_gate = gate_w.shape[1]
    gate_w = jnp.pad(gate_w, ((0, 0), (0, LANES - n_gate))).reshape(3, inner, LANES).astype(BF16)
    gate_b = jnp.pad(jnp.concatenate([p["bi"][0], p["bi"][1], p["bf"][0], p["bf"][1]]),
                     (0, LANES - n_gate)).reshape(1, LANES)
    consts = [p["conv_w"].reshape(9, inner), p["conv_b"].reshape(1, inner),
              _block_diag_lanes(p["wq"]).astype(BF16), _block_diag_lanes(p["wk"]).astype(BF16),
              _block_diag_lanes(p["wv"]).astype(BF16), gate_w, gate_b]
    half = jax.ShapeDtypeStruct((batch, t_len, inner), BF16)
    narrow = jax.ShapeDtypeStruct((batch, t_len, LANES), F32)
    narrow_spec = pl.BlockSpec((1, tb, LANES), lambda b, t: (b, t, 0))
    return pl.pallas_call(
        functools.partial(_ml_feat_kernel, n_ctx_blocks=n_ctx_blocks),
        out_shape=[half, half, half, jax.ShapeDtypeStruct((batch, t_len, inner), F32), narrow, narrow],
        grid=(batch, t_len // tb),
        in_specs=[wide,
                  pl.BlockSpec((1, GRID_W, inner), lambda b, t: (b, jnp.maximum(t * hb - 1, 0), 0)),
                  pl.BlockSpec((1, GRID_W, inner), lambda b, t: (b, jnp.minimum((t + 1) * hb, n_halo - 1), 0)),
                  ] + [_const_spec(c.shape) for c in consts],
        out_specs=[wide, wide, wide, wide, narrow_spec, narrow_spec],
        compiler_params=_params(("parallel", "parallel")),
        name="mlstm_features",
    )(xm, xm, xm, *consts)


def _ml_scan_kernel(q_ref, k_ref, v_ref, gc_ref, sc_ref, gr_ref, sr_ref, h_ref, c_ref, n_ref, m_ref, *, scale):
    d = pl.program_id(0)
    hd = pl.program_id(2)
    c = pl.program_id(3)

    @pl.when(c == 0)
    def _():
        c_ref[...] = jnp.zeros_like(c_ref)
        n_ref[...] = jnp.zeros_like(n_ref)
        m_ref[...] = jnp.zeros_like(m_ref)

    q = q_ref[0]
    k = k_ref[0]
    v = v_ref[0]
    ln = q.shape[0]
    sgn = jnp.where(d == 0, 1, -1)
    ri = lax.broadcasted_iota(I32, (ln, ln), 0)
    ci = lax.broadcasted_iota(I32, (ln, ln), 1)
    allowed = (ci - ri) * sgn <= 0
    gcol = gc_ref[0]
    grow = gr_ref[0]
    i_idx = d * ML_HEADS + hd
    f_idx = 2 * ML_HEADS + i_idx
    lane_c = lax.broadcasted_iota(I32, gcol.shape, 1)
    sub_r = lax.broadcasted_iota(I32, grow.shape, 0)
    b_col = jnp.sum(jnp.where(lane_c == f_idx, sc_ref[0], 0.0), axis=1, keepdims=True)
    f_col = jnp.sum(jnp.where(lane_c == f_idx, gcol, 0.0), axis=1, keepdims=True)
    i_col = jnp.sum(jnp.where(lane_c == i_idx, gcol, 0.0), axis=1, keepdims=True)
    b_row = jnp.sum(jnp.where(sub_r == f_idx, sr_ref[0], 0.0), axis=0, keepdims=True)
    i_row = jnp.sum(jnp.where(sub_r == i_idx, grow, 0.0), axis=0, keepdims=True)
    m_prev = m_ref[0:1, 0:1]
    log_inter = b_col + m_prev
    log_intra = jnp.where(allowed, b_col - b_row + i_row, -jnp.inf)
    m_t = jnp.maximum(log_inter, jnp.max(log_intra, axis=1, keepdims=True))
    w_intra = jnp.exp(log_intra - m_t)
    w_inter = jnp.exp(log_inter - m_t)
    s = _dot_nt(q, k) * scale * w_intra
    ct = c_ref[...]
    nrow = n_ref[0:1, :]
    qf = q.astype(F32)
    num = _dot(s.astype(BF16), v) + w_inter * _dot(q, ct.astype(BF16))
    den = jnp.sum(s, axis=1, keepdims=True) + w_inter * jnp.sum(qf * nrow, axis=1, keepdims=True)
    h_ref[0, 0] = num / jnp.maximum(jnp.abs(den), jnp.exp(-m_t))
    gsum = jnp.sum(f_col, axis=0, keepdims=True)
    log_loc = gsum - b_col + i_col
    m_new = jnp.maximum(gsum + m_prev, jnp.max(log_loc, axis=0, keepdims=True))
    w_loc = jnp.exp(log_loc - m_new)
    w_old = jnp.exp(gsum + m_prev - m_new)
    kw = k.astype(F32) * (scale * w_loc)
    c_ref[...] = w_old * ct + _dot_tn(kw.astype(BF16), v)
    n_ref[0:1, :] = w_old * nrow + jnp.sum(kw, axis=0, keepdims=True)
    m_ref[...] = jnp.zeros_like(m_ref) + m_new


def _ml_scan(q, k, v, gates, gate_sums, tb, n_ctx_blocks):
    batch, t_len, inner = q.shape
    dh = inner // ML_HEADS
    n_blocks = t_len // tb
    n_gate = 4 * ML_HEADS
    gates_t = jnp.swapaxes(gates[..., :n_gate], 1, 2)
    sums_t = jnp.swapaxes(gate_sums[..., :n_gate], 1, 2)

    def head_map(d, b, h, c):
        return (b, _time_block_order(d, c, n_blocks, n_ctx_blocks), h)

    def col_map(d, b, h, c):
        return (b, _time_block_order(d, c, n_blocks, n_ctx_blocks), 0)

    def row_map(d, b, h, c):
        return (b, 0, _time_block_order(d, c, n_blocks, n_ctx_blocks))

    def out_map(d, b, h, c):
        return (d, b, _time_block_order(d, c, n_blocks, n_ctx_blocks), h)

    head = pl.BlockSpec((1, tb, dh), head_map)
    col = pl.BlockSpec((1, tb, LANES), col_map)
    row = pl.BlockSpec((1, n_gate, tb), row_map)
    return pl.pallas_call(
        functools.partial(_ml_scan_kernel, scale=float(dh) ** -0.5),
        out_shape=jax.ShapeDtypeStruct((2, batch, t_len, inner), F32),
        grid=(2, batch, ML_HEADS, n_blocks),
        in_specs=[head, head, head, col, col, row, row],
        out_specs=pl.BlockSpec((1, 1, tb, dh), out_map),
        scratch_shapes=[pltpu.VMEM((dh, dh), F32), pltpu.VMEM((8, dh), F32), pltpu.VMEM((8, LANES), F32)],
        compiler_params=_params(("parallel", "parallel", "parallel", "arbitrary")),
        name="mlstm_scan",
    )(q, k, v, gates, gate_sums, gates_t, sums_t)


def _ml_out_kernel(hs_ref, xc_ref, z_ref, x_ref, mod_ref, ng_ref, sk_ref, wd_ref, g2_ref, wr_ref, br_ref,
                   xo_ref, h2_ref, ti_ref, gt_ref):
    hs = hs_ref[0, 0] + hs_ref[1, 0]
    inner = hs.shape[1]
    dh = inner // ML_HEADS
    parts = []
    for hd in range(ML_HEADS):
        blk = hs[:, hd * dh:(hd + 1) * dh]
        mean = jnp.mean(blk, axis=1, keepdims=True)
        cen = blk - mean
        var = jnp.mean(cen * cen, axis=1, keepdims=True)
        parts.append(cen * lax.rsqrt(var + ML_LN_EPS))
    hn = jnp.concatenate(parts, axis=1) * ng_ref[...]
    y = ((hn + sk_ref[...] * xc_ref[0]) * _silu(z_ref[0])).astype(BF16)
    _post_mixer(x_ref[0], _dot(y, wd_ref[...]), mod_ref[0], g2_ref[...], wr_ref[...], br_ref[...],
                xo_ref, h2_ref, ti_ref, gt_ref)


def _ml_out(hs, xc, z, x, mod_l, p, norm2_g, moe_wr, moe_br, tb, n_ctx_blocks):
    batch, t_len, d_model = x.shape
    inner = xc.shape[-1]
    tok = pl.BlockSpec((1, tb, d_model), lambda b, t: (b, t, 0))
    wide = pl.BlockSpec((1, tb, inner), lambda b, t: (b, t, 0))
    wide2 = pl.BlockSpec((2, 1, tb, inner), lambda b, t: (0, b, t, 0))
    mod_spec = pl.BlockSpec((1, N_MOD, d_model), lambda b, t: (jnp.where(t < n_ctx_blocks, batch, b), 0, 0))
    consts = [p["norm_g"].reshape(1, inner), p["skip"].reshape(1, inner), p["w_down"].astype(BF16),
              norm2_g.reshape(1, d_model), *_router_params(moe_wr, moe_br)]
    shapes, specs = _post_specs(batch, t_len, d_model, tb)
    return pl.pallas_call(
        _ml_out_kernel,
        out_shape=shapes,
        grid=(batch, t_len // tb),
        in_specs=[wide2, wide, wide, tok, mod_spec] + [_const_spec(c.shape) for c in consts],
        out_specs=specs,
        compiler_params=_params(("parallel", "parallel")),
        name="mlstm_readout",
    )(hs, xc, z, x, mod_l, *consts)


def _moe_plan(top_i, tok_rows, n_rows, n_exp, bm):
    n_asg = top_i.size
    assert n_rows < (1 << 16) and MOE_OUT_BUFFERS * bm <= n_rows
    flat_e = top_i.reshape(n_asg)
    onehot = (flat_e[:, None] == jnp.arange(n_exp, dtype=I32)[None, :]).astype(I32)
    csum = jnp.cumsum(onehot, axis=0)
    counts = csum[-1]
    padded = (counts + bm - 1) // bm * bm
    pad_end = jnp.cumsum(padded)
    pad_start = pad_end - padded
    dest = jnp.sum(onehot * (csum - 1 + pad_start[None, :]), axis=1)
    n_blk = n_asg // bm + n_exp
    choice = jnp.arange(TOP_K, dtype=I32)
    real = jnp.bitwise_or(tok_rows[:, None], jnp.left_shift(choice, 16)[None, :]).reshape(n_asg)
    slots = jnp.arange((n_blk + MOE_LEAD_BLOCKS + 1) * bm, dtype=I32)
    buf = (slots // bm - MOE_LEAD_BLOCKS) % MOE_OUT_BUFFERS
    pad = jnp.bitwise_or(buf * bm + slots % bm, TOP_K << 16)
    table = pad.at[MOE_LEAD_BLOCKS * bm + dest].set(real)
    starts = jnp.arange(n_blk, dtype=I32) * bm
    block_e = jnp.minimum(jnp.sum((pad_end[None, :] <= starts[:, None]).astype(I32), axis=1), n_exp - 1)
    return block_e, table


def _moe_expert_kernel(be_ref, tab_m2, tab_m1, tab_0, tab_p1, h_hbm, wgu_ref, bgu_ref, wd_ref, bd_ref, y_hbm,
                       xbuf, obuf, wgu_bf, wd_bf, gsem, ssem, *, bm, n_tok):
    i = pl.program_id(0)
    n_blk = pl.num_programs(0)
    x_cur = lax.rem(i, 2)
    x_next = 1 - x_cur
    o_cur = lax.rem(i, MOE_OUT_BUFFERS)
    o_prev = lax.rem(i + MOE_OUT_BUFFERS - 1, MOE_OUT_BUFFERS)
    o_prev2 = lax.rem(i + MOE_OUT_BUFFERS - 2, MOE_OUT_BUFFERS)
    tables = {-2: tab_m2, -1: tab_m1, 0: tab_0, 1: tab_p1}

    def gather_copy(rel, buf, r):
        tok = jnp.bitwise_and(tables[rel][0, 0, r], 0xFFFF)
        return pltpu.make_async_copy(h_hbm.at[pl.ds(tok, 1)], xbuf.at[buf, pl.ds(r, 1)], gsem.at[buf])

    def scatter_copy(rel, buf, r):
        e = tables[rel][0, 0, r]
        dst = jnp.right_shift(e, 16) * n_tok + jnp.bitwise_and(e, 0xFFFF)
        return pltpu.make_async_copy(obuf.at[buf, pl.ds(r, 1)], y_hbm.at[pl.ds(dst, 1)], ssem.at[buf])

    def for_rows(fn):
        def body(g, carry):
            for k in range(MOE_ROW_UNROLL):
                fn(g * MOE_ROW_UNROLL + k)
            return carry
        lax.fori_loop(0, bm // MOE_ROW_UNROLL, body, 0)

    @pl.when(i == 0)
    def _():
        obuf[...] = jnp.zeros_like(obuf)
        for_rows(lambda r: gather_copy(0, x_cur, r).start())
        for_rows(lambda r: scatter_copy(-2, o_prev2, r).start())

    changed = jnp.logical_or(i == 0, be_ref[i] != be_ref[jnp.maximum(i - 1, 0)])

    @pl.when(changed)
    def _():
        wgu_bf[...] = wgu_ref[0].astype(BF16)
        wd_bf[...] = wd_ref[0].astype(BF16)

    for_rows(lambda r: gather_copy(1, x_next, r).start())
    for_rows(lambda r: scatter_copy(-1, o_prev, r).start())
    for_rows(lambda r: scatter_copy(-2, o_prev2, r).wait())
    for_rows(lambda r: gather_copy(0, x_cur, r).wait())

    x = xbuf[x_cur].astype(BF16)
    gu = _dot(x, wgu_bf[...]) + bgu_ref[0]
    ff = gu.shape[1] // 2
    gate = jnp.minimum(gu[:, :ff], SWIGLU_LIMIT)
    up = jnp.clip(gu[:, ff:], -SWIGLU_LIMIT, SWIGLU_LIMIT)
    act = gate * _sigmoid(SWIGLU_ALPHA * gate) * (up + 1.0)
    obuf[o_cur] = _dot(act.astype(BF16), wd_bf[...]) + bd_ref[0]

    @pl.when(i == n_blk - 1)
    def _():
        for_rows(lambda r: scatter_copy(0, o_cur, r).start())
        for_rows(lambda r: scatter_copy(-1, o_prev, r).wait())
        for_rows(lambda r: scatter_copy(0, o_cur, r).wait())
        for_rows(lambda r: gather_copy(1, x_next, r).wait())


def _moe_experts(h2, top_i, tok_rows, w_gu, b_gu, w_d, b_d):
    n_tok, d_model = h2.shape
    n_exp, _, ff2 = w_gu.shape
    bm = MOE_BLOCK
    ff = ff2 // 2
    assert top_i.size % bm == 0 and bm % MOE_ROW_UNROLL == 0
    block_e, table = _moe_plan(top_i, tok_rows, n_tok, n_exp, bm)
    n_blk = block_e.shape[0]
    table = table.reshape(n_blk + MOE_LEAD_BLOCKS + 1, 1, bm)

    def table_spec(rel):
        return pl.BlockSpec((1, 1, bm), lambda i, be: (i + MOE_LEAD_BLOCKS + rel, 0, 0),
                            memory_space=pltpu.SMEM)

    grid_spec = pltpu.PrefetchScalarGridSpec(
        num_scalar_prefetch=1,
        grid=(n_blk,),
        in_specs=[
            table_spec(-2), table_spec(-1), table_spec(0), table_spec(1),
            pl.BlockSpec(memory_space=pl.ANY),
            pl.BlockSpec((1, d_model, ff2), lambda i, be, *_: (be[i], 0, 0)),
            pl.BlockSpec((1, 1, ff2), lambda i, be, *_: (be[i], 0, 0)),
            pl.BlockSpec((1, ff, d_model), lambda i, be, *_: (be[i], 0, 0)),
            pl.BlockSpec((1, 1, d_model), lambda i, be, *_: (be[i], 0, 0)),
        ],
        out_specs=pl.BlockSpec(memory_space=pl.ANY),
        scratch_shapes=[
            pltpu.VMEM((2, bm, d_model), F32),
            pltpu.VMEM((MOE_OUT_BUFFERS, bm, d_model), F32),
            pltpu.VMEM((d_model, ff2), BF16),
            pltpu.VMEM((ff, d_model), BF16),
            pltpu.SemaphoreType.DMA((2,)),
            pltpu.SemaphoreType.DMA((MOE_OUT_BUFFERS,)),
        ],
    )
    return pl.pallas_call(
        functools.partial(_moe_expert_kernel, bm=bm, n_tok=n_tok),
        out_shape=jax.ShapeDtypeStruct((TOP_K * n_tok + MOE_OUT_BUFFERS * bm, d_model), F32),
        grid_spec=grid_spec,
        compiler_params=_params(("arbitrary",)),
        name="moe_experts",
    )(block_e, table, table, table, table, h2, w_gu, b_gu.reshape(n_exp, 1, ff2), w_d,
      b_d.reshape(n_exp, 1, d_model))


def _moe_combine_kernel(x_ref, gt_ref, mod_ref, fg_ref, *rest, final):
    y_refs, o_ref = rest[:TOP_K], rest[TOP_K]
    gates = gt_ref[0]
    f = jnp.zeros(x_ref.shape[1:], F32)
    for j in range(TOP_K):
        f = f + gates[:, j:j + 1] * y_refs[j][...]
    xn = x_ref[0] + mod_ref[0][5:6] * f
    if final:
        xn = xn * lax.rsqrt(jnp.mean(xn * xn, axis=-1, keepdims=True) + NORM_EPS) * fg_ref[...]
    o_ref[0] = xn


def _moe_combine(x, y4, gates, mod_l, final_g, tb, n_ctx_blocks, final):
    batch, t_len, d_model = x.shape
    off = n_ctx_blocks if final else 0
    t_out = t_len - off * tb
    per_b = t_len // tb
    per_j = batch * per_b
    mod_spec = pl.BlockSpec((1, N_MOD, d_model),
                            lambda b, t: (jnp.where(t + off < n_ctx_blocks, batch, b), 0, 0))

    def choice_spec(j):
        return pl.BlockSpec((tb, d_model), lambda b, t: (j * per_j + b * per_b + t + off, 0))

    return pl.pallas_call(
        functools.partial(_moe_combine_kernel, final=final),
        out_shape=jax.ShapeDtypeStruct((batch, t_out, d_model), F32),
        grid=(batch, t_out // tb),
        in_specs=[pl.BlockSpec((1, tb, d_model), lambda b, t: (b, t + off, 0)),
                  pl.BlockSpec((1, tb, TOP_K), lambda b, t: (b, t + off, 0)),
                  mod_spec, _const_spec((1, d_model))] + [choice_spec(j) for j in range(TOP_K)],
        out_specs=pl.BlockSpec((1, tb, d_model), lambda b, t: (b, t, 0)),
        compiler_params=_params(("parallel", "parallel")),
        name="moe_combine",
    )(x, gates, mod_l, final_g.reshape(1, d_model), *([y4] * TOP_K))


def _moe_layer(x, h2, top_i, gates, mod_l, moe_p, final_g, tb, n_ctx_blocks, last):
    batch, t_len, d_model = x.shape
    tok_rows = jnp.arange(batch * t_len, dtype=I32).reshape(batch, t_len)
    if last:
        n_ctx = n_ctx_blocks * tb
        top_i, tok_rows = top_i[:, n_ctx:], tok_rows[:, n_ctx:]
    y4 = _moe_experts(h2.reshape(batch * t_len, d_model), top_i.reshape(-1, TOP_K), tok_rows.reshape(-1),
                      *moe_p)
    return _moe_combine(x, y4, gates, mod_l, final_g, tb, n_ctx_blocks, last)


def kernel(x, c, ctx, c_ctx, ada_w, ada_b, norm1_g, norm2_g, final_g, rw_mu, rw_wr, rw_wk, rw_wv, rw_wo, rw_w0, rw_w1, rw_w2, rw_a0, rw_a1, rw_a2, rw_g1, rw_g2, rw_kk, rw_ka, rw_rk, rw_lnx_g, rw_lnx_b, rw_v0, rw_v1, rw_v2, ml_w_up, ml_conv_w, ml_conv_b, ml_wq, ml_wk, ml_wv, ml_wi, ml_bi, ml_wf, ml_bf, ml_norm_g, ml_skip, ml_w_down, moe_wr, moe_br, moe_w_gu, moe_b_gu, moe_w_d, moe_b_d):
    batch, seq, d_model = x.shape
    n_ctx = ctx.shape[1]
    depth = ada_w.shape[0]
    tb = math.gcd(MAX_TIME_BLOCK, n_ctx, seq)
    assert tb % GRID_W == 0 and n_ctx == tb and d_model % (2 * RW_HEAD_DIM) == 0
    n_ctx_blocks = n_ctx // tb

    mod = _modulation(c, c_ctx, ada_w, ada_b)
    xs = jnp.concatenate([ctx, x], axis=1)
    v_first = None
    for i in range(depth):
        last = i == depth - 1
        j = i // 2
        mod_l = mod[i]
        if i % 2 == 0:
            p = dict(mu=rw_mu[j], wr=rw_wr[j], wk=rw_wk[j], wv=rw_wv[j], wo=rw_wo[j], w0=rw_w0[j],
                     w1=rw_w1[j], w2=rw_w2[j], a0=rw_a0[j], a1=rw_a1[j], a2=rw_a2[j], g1=rw_g1[j],
                     g2=rw_g2[j], kk=rw_kk[j], ka=rw_ka[j], rk=rw_rk[j], lnx_g=rw_lnx_g[j],
                     lnx_b=rw_lnx_b[j])
            vres = None if j == 0 else (rw_v0[j - 1], rw_v1[j - 1], rw_v2[j - 1])
            r, v, gate, bonus, lp, ap, kd, bb = _rwkv_pre(xs, mod_l, norm1_g[i], p, vres, v_first,
                                                          tb, n_ctx_blocks)
            if j == 0:
                v_first = v
            y = _rwkv_scan(r, v, lp, ap, kd, bb, tb, n_ctx_blocks)
            xs, h2, top_i, gates = _rwkv_out(y, bonus, gate, xs, mod_l, p, norm2_g[i],
                                             moe_wr[i], moe_br[i], tb, n_ctx_blocks)
        else:
            p = dict(conv_w=ml_conv_w[j], conv_b=ml_conv_b[j], wq=ml_wq[j], wk=ml_wk[j], wv=ml_wv[j],
                     wi=ml_wi[j], bi=ml_bi[j], wf=ml_wf[j], bf=ml_bf[j], norm_g=ml_norm_g[j],
                     skip=ml_skip[j], w_down=ml_w_down[j])
            xm, z = _ml_up(xs, mod_l, norm1_g[i], ml_w_up[j], tb, n_ctx_blocks)
            q, k, v, xc, gts, gsums = _ml_feat(xm, p, tb, n_ctx_blocks)
            hs = _ml_scan(q, k, v, gts, gsums, tb, n_ctx_blocks)
            xs, h2, top_i, gates = _ml_out(hs, xc, z, xs, mod_l, p, norm2_g[i],
                                           moe_wr[i], moe_br[i], tb, n_ctx_blocks)
        moe_p = (moe_w_gu[i], moe_b_gu[i], moe_w_d[i], moe_b_d[i])
        xs = _moe_layer(xs, h2, top_i, gates, mod_l, moe_p, final_g, tb, n_ctx_blocks, last)
    return xs
```

```python
import functools
import math

import jax
import jax.numpy as jnp
from jax import lax
from jax.experimental import pallas as pl
from jax.experimental.pallas import tpu as pltpu

F32 = jnp.float32
BF16 = jnp.bfloat16
I32 = jnp.int32
HI = lax.Precision.HIGHEST

GRID_W = 64
N_MOD = 6
NORM_EPS = 1e-6
RW_HEAD_DIM = 64
RW_GN_EPS = 64e-5
RW_CHUNK = 64
ML_HEADS = 4
ML_QKV_BLOCK = 4
ML_LN_EPS = 1e-6
TOP_K = 4
SWIGLU_LIMIT = 7.0
SWIGLU_ALPHA = 1.702
MOE_BLOCK = 256
MOE_OUT_BUFFERS = 3
MOE_LEAD_BLOCKS = 2
SUBLANES = 8
SUBLANE_SHIFT = SUBLANES.bit_length() - 1
LANES = 128
MXU_WIDTH = 256
MAX_TIME_BLOCK = 256
VMEM_LIMIT = 56 * 1024 * 1024


def _dot(a, b, precision=None):
    return jnp.dot(a, b, preferred_element_type=F32, precision=precision)


def _dot_nt(a, b):
    return lax.dot_general(a, b, (((1,), (1,)), ((), ())), preferred_element_type=F32)


def _dot_tn(a, b, precision=None):
    return lax.dot_general(a, b, (((0,), (0,)), ((), ())), preferred_element_type=F32,
                           precision=precision)


def _sigmoid(x):
    return 1.0 / (1.0 + jnp.exp(-x))


def _softplus(x):
    return jnp.maximum(x, 0.0) + jnp.log(1.0 + jnp.exp(-jnp.abs(x)))


def _silu(x):
    return x * _sigmoid(x)


def _adaln(x, g, shift, scale):
    y = x * lax.rsqrt(jnp.mean(x * x, axis=-1, keepdims=True) + NORM_EPS) * g
    return y * (1.0 + scale) + shift


def _const_spec(shape):
    nd = len(shape)
    return pl.BlockSpec(shape, lambda *_: (0,) * nd, pipeline_mode=pl.Buffered(1))


def _params(sem, vmem=VMEM_LIMIT):
    return pltpu.CompilerParams(dimension_semantics=sem, vmem_limit_bytes=vmem)


def _time_block_order(d, c, n_blocks, n_ctx_blocks):
    rev = jnp.where(c < n_ctx_blocks, n_ctx_blocks - 1 - c, n_blocks - 1 - (c - n_ctx_blocks))
    return jnp.where(d == 0, c, rev)


def _mod_kernel(c_ref, w_ref, b_ref, o_ref):
    o_ref[0] = _dot(_silu(c_ref[...]), w_ref[0], HI) + b_ref[0]


def _modulation(c, c_ctx, ada_w, ada_b):
    depth, d_model, _ = ada_w.shape
    batch = c.shape[0]
    rows = -(-(batch + 1) // 8) * 8
    cc = jnp.concatenate([c, c_ctx[None], jnp.zeros((rows - batch - 1, d_model), F32)], axis=0)
    out = pl.pallas_call(
        _mod_kernel,
        out_shape=jax.ShapeDtypeStruct((depth, rows, N_MOD * d_model), F32),
        grid=(depth, N_MOD),
        in_specs=[
            pl.BlockSpec((rows, d_model), lambda l, j: (0, 0)),
            pl.BlockSpec((1, d_model, d_model), lambda l, j: (l, 0, j)),
            pl.BlockSpec((1, 1, d_model), lambda l, j: (l, 0, j)),
        ],
        out_specs=pl.BlockSpec((1, rows, d_model), lambda l, j: (l, 0, j)),
        compiler_params=_params(("parallel", "parallel")),
        name="adaln_modulation",
    )(cc, ada_w, ada_b.reshape(depth, 1, N_MOD * d_model))
    return out.reshape(depth, rows, N_MOD, d_model)


def _neighbours(h, h_prev, h_next, is_ctx, first_lat, last_lat):
    tb = h.shape[0]
    row = lax.broadcasted_iota(I32, (tb, 1), 0)
    col = jnp.bitwise_and(row, GRID_W - 1)
    left_ok = jnp.where(is_ctx, row, col) != 0
    right_ok = jnp.where(is_ctx, row - (tb - 1), col - (GRID_W - 1)) != 0
    left = jnp.where(left_ok, pltpu.roll(h, 1, axis=0), 0.0)
    right = jnp.where(right_ok, pltpu.roll(h, tb - 1, axis=0), 0.0)
    up = jnp.concatenate([h_prev, h[:tb - GRID_W]], axis=0)
    down = jnp.concatenate([h[GRID_W:], h_next], axis=0)
    up_ok = jnp.where(first_lat, row, tb) >= GRID_W
    down_ok = jnp.where(last_lat, row, 0) < tb - GRID_W
    up = jnp.where(up_ok, up, 0.0)
    down = jnp.where(down_ok, down, 0.0)
    return left, right, up, down


def _chunk_cumsum(x, reverse):
    n = x.shape[0]
    pos = jnp.bitwise_and(lax.broadcasted_iota(I32, (n, 1), 0), RW_CHUNK - 1)
    step = 1
    while step < RW_CHUNK:
        if reverse:
            x = x + jnp.where(pos < RW_CHUNK - step, pltpu.roll(x, n - step, axis=0), 0.0)
        else:
            x = x + jnp.where(pos >= step, pltpu.roll(x, step, axis=0), 0.0)
        step *= 2
    return x


def _segment_allsum(x, seg):
    width = x.shape[-1]
    shift = seg.bit_length() - 1
    assert seg == 1 << shift and MXU_WIDTH % seg == 0 and width % MXU_WIDTH == 0
    r = jnp.right_shift(lax.broadcasted_iota(I32, (MXU_WIDTH, MXU_WIDTH), 0), shift)
    c = jnp.right_shift(lax.broadcasted_iota(I32, (MXU_WIDTH, MXU_WIDTH), 1), shift)
    ones = jnp.where(r == c, 1.0, 0.0).astype(BF16)
    hi = x.astype(BF16)
    rest = x - hi.astype(F32)
    mid = rest.astype(BF16)
    lo = (rest - mid.astype(F32)).astype(BF16)
    groups = []
    for g in range(width // MXU_WIDTH):
        cols = slice(g * MXU_WIDTH, (g + 1) * MXU_WIDTH)
        groups.append(_dot(hi[:, cols], ones) + _dot(mid[:, cols], ones) + _dot(lo[:, cols], ones))
    return jnp.concatenate(groups, axis=-1)


def _route_top_k(logits):
    n, n_exp = logits.shape
    lane = lax.broadcasted_iota(I32, (n, n_exp), 1).astype(F32)
    lane_k = lax.broadcasted_iota(I32, (n, TOP_K), 1)
    work = logits
    vals, idxs = [], []
    for _ in range(TOP_K):
        m = jnp.max(work, axis=-1, keepdims=True)
        idx = jnp.min(jnp.where(work == m, lane, float(n_exp)), axis=-1, keepdims=True)
        vals.append(m)
        idxs.append(idx)
        work = jnp.where(lane == idx, -jnp.inf, work)
    es = [jnp.exp(v - vals[0]) for v in vals]
    den = es[0] + es[1] + es[2] + es[3]
    top_i = jnp.zeros((n, TOP_K), I32)
    gates = jnp.zeros((n, TOP_K), F32)
    for j in range(TOP_K):
        top_i = jnp.where(lane_k == j, idxs[j].astype(I32), top_i)
        gates = jnp.where(lane_k == j, es[j] / den, gates)
    return top_i, gates


def _router_params(moe_wr, moe_br):
    n_exp = moe_wr.shape[-1]
    assert n_exp <= LANES
    wr = jnp.pad(moe_wr, ((0, 0), (0, LANES - n_exp)))
    br = jnp.pad(moe_br, (0, LANES - n_exp), constant_values=-1e30)
    return wr, br.reshape(1, LANES)


def _post_mixer(x, y, mod, g2, wr, br, xo_ref, h2_ref, ti_ref, gt_ref):
    xn = x + mod[2:3] * y
    xo_ref[0] = xn
    h2 = _adaln(xn, g2, mod[3:4], mod[4:5])
    h2_ref[0] = h2
    top_i, gates = _route_top_k(_dot(h2, wr, HI) + br)
    ti_ref[0] = top_i
    gt_ref[0] = gates


def _rwkv_pre_kernel(*refs, has_vres, n_ctx_blocks):
    (x_ref, xp_ref, xn_ref, mod_ref, g_ref, mu_ref, wr_ref, wk_ref, wv_ref, wlw_ref, wla_ref,
     wg1_ref, w2_ref, a2_ref, g2_ref, w0_ref, a0_ref, kkp_ref, kap_ref, rk_ref) = refs[:20]
    if has_vres:
        vf_ref, v0_ref, v1_ref, v2_ref = refs[20:24]
        outs = refs[24:]
    else:
        outs = refs[20:]
    r_ref, v_ref, gg_ref, bo_ref, lp_ref, ap_ref, kd_ref, bb_ref = outs

    t = pl.program_id(1)
    n_blocks = pl.num_programs(1)
    is_ctx = t < n_ctx_blocks
    mod = mod_ref[0]
    g = g_ref[...]
    h = _adaln(x_ref[0], g, mod[0:1], mod[1:2])
    hp = _adaln(xp_ref[0], g, mod[0:1], mod[1:2])
    hn = _adaln(xn_ref[0], g, mod[0:1], mod[1:2])
    left, right, up, down = _neighbours(h, hp, hn, is_ctx, t == n_ctx_blocks, t == n_blocks - 1)
    d_model = h.shape[1]
    q = d_model // 4
    lane = lax.broadcasted_iota(I32, (1, d_model), 1)
    q2 = jnp.where(is_ctx, left, up)
    q3 = jnp.where(is_ctx, right, down)
    shifted = jnp.where(lane < q, left, jnp.where(lane < 2 * q, right, jnp.where(lane < 3 * q, q2, q3)))
    xx = shifted - h
    mu = mu_ref[...]

    def mix(j):
        return (h + xx * mu[j:j + 1]).astype(BF16)

    r = _dot(mix(0), wr_ref[...])
    k = _dot(mix(2), wk_ref[...])
    xv = mix(3)
    v = _dot(xv, wv_ref[...])
    if has_vres:
        lora = _dot(_dot(xv, v1_ref[...]).astype(BF16), v2_ref[...])
        v = v + (vf_ref[0] - v) * _sigmoid(v0_ref[...] + lora)
    gate = _dot(_sigmoid(_dot(mix(5), wg1_ref[...])).astype(BF16), g2_ref[...])
    kx = k * kkp_ref[...]
    kk = kx / jnp.maximum(jnp.sqrt(_segment_allsum(kx * kx, RW_HEAD_DIM)), 1e-12)
    r_ref[0] = r
    v_ref[0] = v
    gg_ref[0] = gate
    tw = jnp.tanh(_dot(mix(1), wlw_ref[...])).astype(BF16)
    la = _dot(mix(4), wla_ref[...]).astype(BF16)
    kap = kap_ref[...]
    kd_sum = None
    for d in range(2):
        logw = -_softplus(-(w0_ref[d:d + 1] + _dot(tw, w2_ref[d]))) - 0.5
        decay_rate = jnp.exp(logw)
        lp_ref[d, 0] = _chunk_cumsum(-decay_rate, reverse=d == 1)
        ap_ref[d, 0] = -kk * jnp.exp(decay_rate)
        a = _sigmoid(a0_ref[d:d + 1] + _dot(la, a2_ref[d]))
        kd = k * (1.0 + (a - 1.0) * kap)
        kd_ref[d, 0] = kd
        bb_ref[d, 0] = kk * a
        kd_sum = kd if kd_sum is None else kd_sum + kd
    bo_ref[0] = _segment_allsum(r * rk_ref[...] * kd_sum, RW_HEAD_DIM) * v


def _rwkv_pre(x, mod_l, norm_g, p, vres, v_first, tb, n_ctx_blocks):
    batch, t_len, d_model = x.shape
    n_blocks = t_len // tb
    hb = tb // GRID_W
    n_halo = t_len // GRID_W
    lora = p["w1"].shape[-1]
    assert 2 * lora == LANES and p["g1"].shape[-1] == LANES
    zeros = jnp.zeros((lora, d_model), F32)
    w2p = jnp.stack([jnp.concatenate([p["w2"][0], zeros]), jnp.concatenate([zeros, p["w2"][1]])])
    a2p = jnp.stack([jnp.concatenate([p["a2"][0], zeros]), jnp.concatenate([zeros, p["a2"][1]])])
    has_vres = vres is not None

    tok = pl.BlockSpec((1, tb, d_model), lambda b, t: (b, t, 0))
    tok2 = pl.BlockSpec((2, 1, tb, d_model), lambda b, t: (0, b, t, 0))
    in_specs = [
        tok,
        pl.BlockSpec((1, GRID_W, d_model), lambda b, t: (b, jnp.maximum(t * hb - 1, 0), 0)),
        pl.BlockSpec((1, GRID_W, d_model), lambda b, t: (b, jnp.minimum((t + 1) * hb, n_halo - 1), 0)),
        pl.BlockSpec((1, N_MOD, d_model), lambda b, t: (jnp.where(t < n_ctx_blocks, batch, b), 0, 0)),
    ]
    args = [x, x, x, mod_l]
    consts = [
        norm_g.reshape(1, d_model), p["mu"],
        p["wr"].astype(BF16), p["wk"].astype(BF16), p["wv"].astype(BF16),
        jnp.concatenate([p["w1"][0], p["w1"][1]], axis=1).astype(BF16),
        jnp.concatenate([p["a1"][0], p["a1"][1]], axis=1).astype(BF16),
        p["g1"].astype(BF16), w2p.astype(BF16), a2p.astype(BF16), p["g2"].astype(BF16),
        p["w0"], p["a0"], p["kk"].reshape(1, d_model), p["ka"].reshape(1, d_model),
        p["rk"].reshape(1, d_model),
    ]
    in_specs += [_const_spec(c.shape) for c in consts]
    args += consts
    if has_vres:
        v0, v1, v2 = vres
        extra = [v0.reshape(1, d_model), v1.astype(BF16), v2.astype(BF16)]
        in_specs += [tok] + [_const_spec(c.shape) for c in extra]
        args += [v_first] + extra
    one = jax.ShapeDtypeStruct((batch, t_len, d_model), F32)
    two = jax.ShapeDtypeStruct((2, batch, t_len, d_model), F32)
    return pl.pallas_call(
        functools.partial(_rwkv_pre_kernel, has_vres=has_vres, n_ctx_blocks=n_ctx_blocks),
        out_shape=[one, one, one, one, two, two, two, two],
        grid=(batch, n_blocks),
        in_specs=in_specs,
        out_specs=[tok, tok, tok, tok, tok2, tok2, tok2, tok2],
        compiler_params=_params(("parallel", "parallel")),
        name="rwkv_features",
    )(*args)


def _rwkv_scan_kernel(r_ref, v_ref, lp_ref, ap_ref, kd_ref, bb_ref, y_ref, s_ref, *, n_pairs, tb):
    d = pl.program_id(0)
    c = pl.program_id(3)
    ch = RW_CHUNK
    two = 2 * ch

    @pl.when(c == 0)
    def _():
        s_ref[...] = jnp.zeros_like(s_ref)

    sgn = jnp.where(d == 0, 1, -1)
    r2 = lax.broadcasted_iota(I32, (two, two), 0)
    c2 = lax.broadcasted_iota(I32, (two, two), 1)
    same = jnp.right_shift(r2, 6) == jnp.right_shift(c2, 6)
    delta = (jnp.bitwise_and(c2, ch - 1) - jnp.bitwise_and(r2, ch - 1)) * sgn
    before = jnp.logical_and(same, delta < 0)
    before_eq = jnp.logical_and(same, delta <= 0)
    head0 = lax.broadcasted_iota(I32, (ch, LANES), 1) < RW_HEAD_DIM
    head0w = lax.broadcasted_iota(I32, (ch, 2 * LANES), 1)
    head0w = jnp.bitwise_and(head0w, LANES - 1) < RW_HEAD_DIM
    n_sub = tb // ch
    pairs = range(n_pairs)

    def stack2(x):
        return jnp.concatenate([x, x], axis=0)

    def pick(x, mask):
        return jnp.where(mask, x[:ch], x[ch:])

    def sub_chunk(j, carry):
        jj = jnp.where(d == 0, j, n_sub - 1 - j)
        start = pl.multiple_of(jj * ch, ch)
        rows = pl.ds(start, ch)
        last = pl.ds(pl.multiple_of(start + jnp.where(d == 0, ch - 8, 0), 8), 8)
        cols = [slice(p * LANES, (p + 1) * LANES) for p in pairs]

        v, vv, r_t, a_t, b_p, k_p, m_b, m_k, p_end = [], [], [], [], [], [], [], [], []
        for p in pairs:
            lp = lp_ref[0, 0, rows, cols[p]]
            e_in = jnp.exp(lp)
            e_out = jnp.exp(-lp)
            lp_edge = lp_ref[0, 0, last, cols[p]]
            p_last = jnp.exp(jnp.where(d == 0, lp_edge[7:8], lp_edge[0:1]))
            p_end.append(p_last)
            at = ap_ref[0, 0, rows, cols[p]] * e_in
            rt = r_ref[0, rows, cols[p]] * e_in
            bt = bb_ref[0, 0, rows, cols[p]] * e_out
            kt = kd_ref[0, 0, rows, cols[p]] * e_out
            vp = v_ref[0, rows, cols[p]]
            lhs = jnp.concatenate([jnp.where(head0, at, 0.0), jnp.where(head0, 0.0, at),
                                   jnp.where(head0, rt, 0.0), jnp.where(head0, 0.0, rt)],
                                  axis=0).astype(BF16)
            m_bk = _dot_nt(lhs, jnp.concatenate([bt, bt, kt, kt], axis=0).astype(BF16))
            m_b.append(m_bk[:, :two])
            m_k.append(m_bk[:, two:])
            v.append(vp)
            vv.append(stack2(vp).astype(BF16))
            r_t.append(rt)
            a_t.append(at)
            b_p.append(bt * p_last)
            k_p.append(kt * p_last)
        n_pow = [jnp.where(before, m_b[p][:two], 0.0).astype(BF16) for p in pairs]
        a_k = [jnp.where(before, m_k[p][:two], 0.0).astype(BF16) for p in pairs]
        r_b = [jnp.where(before_eq, m_b[p][two:], 0.0).astype(BF16) for p in pairs]
        r_k = [jnp.where(before_eq, m_k[p][two:], 0.0).astype(BF16) for p in pairs]
        z = [jnp.concatenate([stack2(a_t[p]), _dot(a_k[p], vv[p])], axis=1) for p in pairs]
        z = [z[p] + _dot(n_pow[p], z[p].astype(BF16)) for p in pairs]
        for _ in range(int(math.log2(ch)) - 1):
            n_pow = [_dot(n_pow[p], n_pow[p]).astype(BF16) for p in pairs]
            z = [z[p] + _dot(n_pow[p], z[p].astype(BF16)) for p in pairs]
        q = [_dot(r_b[p], z[p].astype(BF16)) for p in pairs]
        y_k = [_dot(r_k[p], vv[p]) for p in pairs]
        w_u = [pick(z[p], head0w) for p in pairs]
        r_y = [pick(q[p], head0w) for p in pairs]
        lhs_s = [jnp.concatenate([w_u[p][:, :LANES], r_t[p] + r_y[p][:, :LANES]], axis=0).astype(BF16)
                 for p in pairs]
        u_0 = [w_u[p][:, LANES:] for p in pairs]
        y_0 = [r_y[p][:, LANES:] + pick(y_k[p], head0) for p in pairs]
        bk_p = [jnp.concatenate([b_p[p], k_p[p]], axis=0).astype(BF16) for p in pairs]

        s_old = [s_ref[p] for p in pairs]
        g = [_dot_nt(lhs_s[p], s_old[p].astype(BF16)) for p in pairs]
        u = [g[p][:ch] + u_0[p] for p in pairs]
        for p in pairs:
            y_ref[0, 0, rows, cols[p]] = g[p][ch:] + y_0[p]
        d_s = [_dot_tn(jnp.concatenate([u[p], v[p]], axis=0).astype(BF16), bk_p[p]) for p in pairs]
        for p in pairs:
            s_ref[p] = s_old[p] * p_end[p] + jnp.where(same, d_s[p], 0.0)
        return carry

    lax.fori_loop(0, n_sub, sub_chunk, 0)


def _rwkv_scan(r, v, lp, ap, kd, bb, tb, n_ctx_blocks):
    batch, t_len, d_model = r.shape
    n_blocks = t_len // tb
    lanes = min(8 * LANES, d_model)
    n_pairs = lanes // LANES

    def one_map(d, b, g, c):
        return (b, _time_block_order(d, c, n_blocks, n_ctx_blocks), g)

    def two_map(d, b, g, c):
        return (d, b, _time_block_order(d, c, n_blocks, n_ctx_blocks), g)

    one = pl.BlockSpec((1, tb, lanes), one_map)
    two = pl.BlockSpec((1, 1, tb, lanes), two_map)
    return pl.pallas_call(
        functools.partial(_rwkv_scan_kernel, n_pairs=n_pairs, tb=tb),
        out_shape=jax.ShapeDtypeStruct((2, batch, t_len, d_model), F32),
        grid=(2, batch, d_model // lanes, n_blocks),
        in_specs=[one, one, two, two, two, two],
        out_specs=two,
        scratch_shapes=[pltpu.VMEM((n_pairs, LANES, LANES), F32)],
        compiler_params=_params(("parallel", "parallel", "parallel", "arbitrary")),
        name="rwkv_scan",
    )(r, v, lp, ap, kd, bb)


def _rwkv_out_kernel(y_ref, bo_ref, gg_ref, x_ref, mod_ref, lg_ref, lb_ref,
                     wo_ref, g2_ref, wr_ref, br_ref,
                     xo_ref, h2_ref, ti_ref, gt_ref):
    y = y_ref[0, 0] + y_ref[1, 0]
    inv_n = 1.0 / RW_HEAD_DIM
    yc = y - _segment_allsum(y, RW_HEAD_DIM) * inv_n
    var = _segment_allsum(yc * yc, RW_HEAD_DIM) * inv_n
    yn = yc * lax.rsqrt(var + RW_GN_EPS) * lg_ref[...] + lb_ref[...]
    out = ((yn + bo_ref[0]) * gg_ref[0]).astype(BF16)
    _post_mixer(x_ref[0], _dot(out, wo_ref[...]), mod_ref[0], g2_ref[...], wr_ref[...], br_ref[...],
                xo_ref, h2_ref, ti_ref, gt_ref)


def _post_specs(batch, t_len, d_model, tb):
    tok = pl.BlockSpec((1, tb, d_model), lambda b, t: (b, t, 0))
    nar = pl.BlockSpec((1, tb, TOP_K), lambda b, t: (b, t, 0))
    shapes = [jax.ShapeDtypeStruct((batch, t_len, d_model), F32),
              jax.ShapeDtypeStruct((batch, t_len, d_model), F32),
              jax.ShapeDtypeStruct((batch, t_len, TOP_K), I32),
              jax.ShapeDtypeStruct((batch, t_len, TOP_K), F32)]
    return shapes, [tok, tok, nar, nar]


def _rwkv_out(y, bonus, gate, x, mod_l, p, norm2_g, moe_wr, moe_br, tb, n_ctx_blocks):
    batch, t_len, d_model = x.shape
    tok = pl.BlockSpec((1, tb, d_model), lambda b, t: (b, t, 0))
    tok2 = pl.BlockSpec((2, 1, tb, d_model), lambda b, t: (0, b, t, 0))
    mod_spec = pl.BlockSpec((1, N_MOD, d_model), lambda b, t: (jnp.where(t < n_ctx_blocks, batch, b), 0, 0))
    consts = [p["lnx_g"].reshape(1, d_model), p["lnx_b"].reshape(1, d_model), p["wo"].astype(BF16), norm2_g.reshape(1, d_model), *_router_params(moe_wr, moe_br)]
    shapes, specs = _post_specs(batch, t_len, d_model, tb)
    return pl.pallas_call(
        _rwkv_out_kernel,
        out_shape=shapes,
        grid=(batch, t_len // tb),
        in_specs=[tok2, tok, tok, tok, mod_spec] + [_const_spec(c.shape) for c in consts],
        out_specs=specs,
        compiler_params=_params(("parallel", "parallel")),
        name="rwkv_readout",
    )(y, bonus, gate, x, mod_l, *consts)


def _ml_up_kernel(x_ref, mod_ref, g_ref, w_ref, xm_ref, z_ref):
    mod = mod_ref[0]
    h = _adaln(x_ref[0], g_ref[...], mod[0:1], mod[1:2]).astype(BF16)
    up = _dot(h, w_ref[...])
    inner = xm_ref.shape[-1]
    xm_ref[0] = up[:, :inner]
    z_ref[0] = up[:, inner:]


def _ml_up(x, mod_l, norm_g, w_up, tb, n_ctx_blocks):
    batch, t_len, d_model = x.shape
    inner = w_up.shape[1] // 2
    tok = pl.BlockSpec((1, tb, d_model), lambda b, t: (b, t, 0))
    wide = pl.BlockSpec((1, tb, inner), lambda b, t: (b, t, 0))
    mod_spec = pl.BlockSpec((1, N_MOD, d_model), lambda b, t: (jnp.where(t < n_ctx_blocks, batch, b), 0, 0))
    out = jax.ShapeDtypeStruct((batch, t_len, inner), F32)
    return pl.pallas_call(
        _ml_up_kernel,
        out_shape=[out, out],
        grid=(batch, t_len // tb),
        in_specs=[tok, mod_spec, _const_spec((1, d_model)), _const_spec(w_up.shape)],
        out_specs=[wide, wide],
        compiler_params=_params(("parallel", "parallel")),
        name="mlstm_up",
    )(x, mod_l, norm_g.reshape(1, d_model), w_up.astype(BF16))


def _ml_feat_kernel(xm_ref, xp_ref, xn_ref, cw_ref, cb_ref, wq_ref, wk_ref, wv_ref, wg_ref, bg_ref,
                    q_ref, k_ref, v_ref, xc_ref, gt_ref, gs_ref, gtt_ref, gst_ref, *, n_ctx_blocks):
    t = pl.program_id(1)
    n_blocks = pl.num_programs(1)
    is_ctx = t < n_ctx_blocks
    xm = xm_ref[0]
    tb, inner = xm.shape
    row = lax.broadcasted_iota(I32, (tb, 1), 0)
    col = jnp.bitwise_and(row, GRID_W - 1)
    left_ok = jnp.where(is_ctx, row, col) != 0
    right_ok = jnp.where(is_ctx, row - (tb - 1), col - (GRID_W - 1)) != 0
    up = jnp.concatenate([xp_ref[0], xm[:tb - GRID_W]], axis=0)
    down = jnp.concatenate([xm[GRID_W:], xn_ref[0]], axis=0)
    up_ok = jnp.where(is_ctx, -1, jnp.where(t == n_ctx_blocks, row, tb)) >= GRID_W
    down_ok = jnp.where(is_ctx, tb, jnp.where(t == n_blocks - 1, row, 0)) < tb - GRID_W
    up = jnp.where(up_ok, up, 0.0)
    down = jnp.where(down_ok, down, 0.0)
    cw = cw_ref[...]
    acc = jnp.zeros((tb, inner), F32) + cb_ref[...]
    for dy, src in enumerate((up, xm, down)):
        acc = acc + jnp.where(left_ok, pltpu.roll(src, 1, axis=0), 0.0) * cw[3 * dy:3 * dy + 1]
        acc = acc + src * cw[3 * dy + 1:3 * dy + 2]
        acc = acc + jnp.where(right_ok, pltpu.roll(src, tb - 1, axis=0), 0.0) * cw[3 * dy + 2:3 * dy + 3]
    xc = _silu(acc)
    xc_ref[0] = xc
    xcb = xc.astype(BF16)
    xmb = xm.astype(BF16)
    for gidx in range(inner // LANES):
        cols = slice(gidx * LANES, (gidx + 1) * LANES)
        q_ref[0, :, cols] = _dot(xcb[:, cols], wq_ref[gidx]).astype(BF16)
        k_ref[0, :, cols] = _dot(xcb[:, cols], wk_ref[gidx]).astype(BF16)
        v_ref[0, :, cols] = _dot(xmb[:, cols], wv_ref[gidx]).astype(BF16)
    pre = (_dot(q_ref[0], wg_ref[0]) + _dot(k_ref[0], wg_ref[1]) + _dot(v_ref[0], wg_ref[2]) + bg_ref[...])
    lane = lax.broadcasted_iota(I32, pre.shape, 1)
    gates = jnp.where(lane < 2 * ML_HEADS, pre, -_softplus(-pre))
    gt_ref[0] = gates
    fwd = bwd = gates
    step = 1
    while step < tb:
        fwd = fwd + jnp.where(row >= step, pltpu.roll(fwd, step, axis=0), 0.0)
        bwd = bwd + jnp.where(row < tb - step, pltpu.roll(bwd, tb - step, axis=0), 0.0)
        step *= 2
    backward_lane = jnp.bitwise_and(jnp.right_shift(lane, 2), 1) == 1
    sums = jnp.where(backward_lane, bwd, fwd)
    gs_ref[0] = sums
    n_gate = gtt_ref.shape[1]
    gtt_ref[0] = jnp.transpose(gates)[:n_gate]
    gst_ref[0] = jnp.transpose(sums)[:n_gate]


def _block_diag_lanes(w):
    n_blk, blk, _ = w.shape
    per = LANES // blk
    w = w.reshape(n_blk // per, per, blk, blk)
    eye = jnp.eye(per, dtype=w.dtype)
    return jnp.einsum("gaio,ab->gaibo", w, eye).reshape(n_blk // per, LANES, LANES)


def _ml_feat(xm, p, tb, n_ctx_blocks):
    batch, t_len, inner = xm.shape
    hb = tb // GRID_W
    n_halo = t_len // GRID_W
    wide = pl.BlockSpec((1, tb, inner), lambda b, t: (b, t, 0))
    gate_w = jnp.concatenate([p["wi"][0], p["wi"][1], p["wf"][0], p["wf"][1]], axis=1)
    n_gate = gate_w.shape[1]
    gate_w = jnp.pad(gate_w, ((0, 0), (0, LANES - n_gate))).reshape(3, inner, LANES).astype(BF16)
    gate_b = jnp.pad(jnp.concatenate([p["bi"][0], p["bi"][1], p["bf"][0], p["bf"][1]]),
                     (0, LANES - n_gate)).reshape(1, LANES)
    consts = [p["conv_w"].reshape(9, inner), p["conv_b"].reshape(1, inner),
              _block_diag_lanes(p["wq"]).astype(BF16), _block_diag_lanes(p["wk"]).astype(BF16),
              _block_diag_lanes(p["wv"]).astype(BF16), gate_w, gate_b]
    half = jax.ShapeDtypeStruct((batch, t_len, inner), BF16)
    narrow = jax.ShapeDtypeStruct((batch, t_len, LANES), F32)
    narrow_spec = pl.BlockSpec((1, tb, LANES), lambda b, t: (b, t, 0))
    flat = jax.ShapeDtypeStruct((batch, n_gate, t_len), F32)
    flat_spec = pl.BlockSpec((1, n_gate, tb), lambda b, t: (b, 0, t))
    return pl.pallas_call(
        functools.partial(_ml_feat_kernel, n_ctx_blocks=n_ctx_blocks),
        out_shape=[half, half, half, jax.ShapeDtypeStruct((batch, t_len, inner), F32), narrow, narrow,
                   flat, flat],
        grid=(batch, t_len // tb),
        in_specs=[wide,
                  pl.BlockSpec((1, GRID_W, inner), lambda b, t: (b, jnp.maximum(t * hb - 1, 0), 0)),
                  pl.BlockSpec((1, GRID_W, inner), lambda b, t: (b, jnp.minimum((t + 1) * hb, n_halo - 1), 0)),
                  ] + [_const_spec(c.shape) for c in consts],
        out_specs=[wide, wide, wide, wide, narrow_spec, narrow_spec, flat_spec, flat_spec],
        compiler_params=_params(("parallel", "parallel")),
        name="mlstm_features",
    )(xm, xm, xm, *consts)


def _ml_scan_kernel(q_ref, k_ref, v_ref, gc_ref, sc_ref, gr_ref, sr_ref, h_ref, c_ref, n_ref, m_ref, *, scale):
    d = pl.program_id(0)
    hd = pl.program_id(2)
    c = pl.program_id(3)

    @pl.when(c == 0)
    def _():
        c_ref[...] = jnp.zeros_like(c_ref)
        n_ref[...] = jnp.zeros_like(n_ref)
        m_ref[...] = jnp.zeros_like(m_ref)

    q = q_ref[0]
    k = k_ref[0]
    v = v_ref[0]
    ln = q.shape[0]
    sgn = jnp.where(d == 0, 1, -1)
    ri = lax.broadcasted_iota(I32, (ln, ln), 0)
    ci = lax.broadcasted_iota(I32, (ln, ln), 1)
    allowed = (ci - ri) * sgn <= 0
    gcol = gc_ref[0]
    grow = gr_ref[0]
    i_idx = d * ML_HEADS + hd
    f_idx = 2 * ML_HEADS + i_idx
    lane_c = lax.broadcasted_iota(I32, gcol.shape, 1)
    sub_r = lax.broadcasted_iota(I32, grow.shape, 0)
    b_col = jnp.sum(jnp.where(lane_c == f_idx, sc_ref[0], 0.0), axis=1, keepdims=True)
    f_col = jnp.sum(jnp.where(lane_c == f_idx, gcol, 0.0), axis=1, keepdims=True)
    i_col = jnp.sum(jnp.where(lane_c == i_idx, gcol, 0.0), axis=1, keepdims=True)
    b_row = jnp.sum(jnp.where(sub_r == f_idx, sr_ref[0], 0.0), axis=0, keepdims=True)
    i_row = jnp.sum(jnp.where(sub_r == i_idx, grow, 0.0), axis=0, keepdims=True)
    m_prev = m_ref[0:1, 0:1]
    log_inter = b_col + m_prev
    log_intra = jnp.where(allowed, b_col - b_row + i_row, -jnp.inf)
    m_t = jnp.maximum(log_inter, jnp.max(log_intra, axis=1, keepdims=True))
    w_intra = jnp.exp(log_intra - m_t)
    w_inter = jnp.exp(log_inter - m_t)
    s = _dot_nt(q, k) * scale * w_intra
    ct = c_ref[...]
    nrow = n_ref[0:1, :]
    qf = q.astype(F32)
    num = _dot(s.astype(BF16), v) + w_inter * _dot(q, ct.astype(BF16))
    den = jnp.sum(s, axis=1, keepdims=True) + w_inter * jnp.sum(qf * nrow, axis=1, keepdims=True)
    h_ref[0, 0] = num / jnp.maximum(jnp.abs(den), jnp.exp(-m_t))
    gsum = jnp.sum(f_col, axis=0, keepdims=True)
    log_loc = gsum - b_col + i_col
    m_new = jnp.maximum(gsum + m_prev, jnp.max(log_loc, axis=0, keepdims=True))
    w_loc = jnp.exp(log_loc - m_new)
    w_old = jnp.exp(gsum + m_prev - m_new)
    kw = k.astype(F32) * (scale * w_loc)
    c_ref[...] = w_old * ct + _dot_tn(kw.astype(BF16), v)
    n_ref[0:1, :] = w_old * nrow + jnp.sum(kw, axis=0, keepdims=True)
    m_ref[...] = jnp.zeros_like(m_ref) + m_new


def _ml_scan(q, k, v, gates, gate_sums, gates_t, sums_t, tb, n_ctx_blocks):
    batch, t_len, inner = q.shape
    dh = inner // ML_HEADS
    n_blocks = t_len // tb
    n_gate = gates_t.shape[1]

    def head_map(d, b, h, c):
        return (b, _time_block_order(d, c, n_blocks, n_ctx_blocks), h)

    def col_map(d, b, h, c):
        return (b, _time_block_order(d, c, n_blocks, n_ctx_blocks), 0)

    def row_map(d, b, h, c):
        return (b, 0, _time_block_order(d, c, n_blocks, n_ctx_blocks))

    def out_map(d, b, h, c):
        return (d, b, _time_block_order(d, c, n_blocks, n_ctx_blocks), h)

    head = pl.BlockSpec((1, tb, dh), head_map)
    col = pl.BlockSpec((1, tb, LANES), col_map)
    row = pl.BlockSpec((1, n_gate, tb), row_map)
    return pl.pallas_call(
        functools.partial(_ml_scan_kernel, scale=float(dh) ** -0.5),
        out_shape=jax.ShapeDtypeStruct((2, batch, t_len, inner), F32),
        grid=(2, batch, ML_HEADS, n_blocks),
        in_specs=[head, head, head, col, col, row, row],
        out_specs=pl.BlockSpec((1, 1, tb, dh), out_map),
        scratch_shapes=[pltpu.VMEM((dh, dh), F32), pltpu.VMEM((8, dh), F32), pltpu.VMEM((8, LANES), F32)],
        compiler_params=_params(("parallel", "parallel", "parallel", "arbitrary")),
        name="mlstm_scan",
    )(q, k, v, gates, gate_sums, gates_t, sums_t)


def _ml_out_kernel(hs_ref, xc_ref, z_ref, x_ref, mod_ref, ng_ref, sk_ref, wd_ref, g2_ref, wr_ref, br_ref,
                   xo_ref, h2_ref, ti_ref, gt_ref):
    hs = hs_ref[0, 0] + hs_ref[1, 0]
    inner = hs.shape[1]
    dh = inner // ML_HEADS
    parts = []
    for hd in range(ML_HEADS):
        blk = hs[:, hd * dh:(hd + 1) * dh]
        mean = jnp.mean(blk, axis=1, keepdims=True)
        cen = blk - mean
        var = jnp.mean(cen * cen, axis=1, keepdims=True)
        parts.append(cen * lax.rsqrt(var + ML_LN_EPS))
    hn = jnp.concatenate(parts, axis=1) * ng_ref[...]
    y = ((hn + sk_ref[...] * xc_ref[0]) * _silu(z_ref[0])).astype(BF16)
    _post_mixer(x_ref[0], _dot(y, wd_ref[...]), mod_ref[0], g2_ref[...], wr_ref[...], br_ref[...],
                xo_ref, h2_ref, ti_ref, gt_ref)


def _ml_out(hs, xc, z, x, mod_l, p, norm2_g, moe_wr, moe_br, tb, n_ctx_blocks):
    batch, t_len, d_model = x.shape
    inner = xc.shape[-1]
    tok = pl.BlockSpec((1, tb, d_model), lambda b, t: (b, t, 0))
    wide = pl.BlockSpec((1, tb, inner), lambda b, t: (b, t, 0))
    wide2 = pl.BlockSpec((2, 1, tb, inner), lambda b, t: (0, b, t, 0))
    mod_spec = pl.BlockSpec((1, N_MOD, d_model), lambda b, t: (jnp.where(t < n_ctx_blocks, batch, b), 0, 0))
    consts = [p["norm_g"].reshape(1, inner), p["skip"].reshape(1, inner), p["w_down"].astype(BF16),
              norm2_g.reshape(1, d_model), *_router_params(moe_wr, moe_br)]
    shapes, specs = _post_specs(batch, t_len, d_model, tb)
    return pl.pallas_call(
        _ml_out_kernel,
        out_shape=shapes,
        grid=(batch, t_len // tb),
        in_specs=[wide2, wide, wide, tok, mod_spec] + [_const_spec(c.shape) for c in consts],
        out_specs=specs,
        compiler_params=_params(("parallel", "parallel")),
        name="mlstm_readout",
    )(hs, xc, z, x, mod_l, *consts)


def _moe_plan(top_i, tok_rows, n_rows, n_exp, bm):
    n_asg = top_i.size
    assert n_rows < (1 << 16) and MOE_OUT_BUFFERS * bm <= n_rows
    flat_e = top_i.reshape(n_asg)
    onehot = (flat_e[:, None] == jnp.arange(n_exp, dtype=I32)[None, :]).astype(I32)
    csum = jnp.cumsum(onehot, axis=0)
    counts = csum[-1]
    padded = (counts + bm - 1) // bm * bm
    pad_end = jnp.cumsum(padded)
    pad_start = pad_end - padded
    dest = jnp.sum(onehot * (csum - 1 + pad_start[None, :]), axis=1)
    n_blk = n_asg // bm + n_exp
    choice = jnp.arange(TOP_K, dtype=I32)
    real = jnp.bitwise_or(tok_rows[:, None], jnp.left_shift(choice, 16)[None, :]).reshape(n_asg)
    slots = jnp.arange((n_blk + MOE_LEAD_BLOCKS + 1) * bm, dtype=I32)
    buf = (slots // bm - MOE_LEAD_BLOCKS) % MOE_OUT_BUFFERS
    pad = jnp.bitwise_or(buf * bm + slots % bm, TOP_K << 16)
    table = pad.at[MOE_LEAD_BLOCKS * bm + dest].set(real)
    starts = jnp.arange(n_blk, dtype=I32) * bm
    block_e = jnp.minimum(jnp.sum((pad_end[None, :] <= starts[:, None]).astype(I32), axis=1), n_exp - 1)
    return block_e, table


def _moe_expert_kernel(be_ref, src_0, src_p1, dst_m2, dst_m1, dst_0, h_hbm, wgu_ref, bgu_ref, wd_ref, bd_ref,
                       y_hbm, xbuf, obuf, wgu_bf, wd_bf, gsem, ssem):
    i = pl.program_id(0)
    n_blk = pl.num_programs(0)
    n_groups = xbuf.shape[1]
    x_cur = lax.rem(i, 2)
    x_next = 1 - x_cur
    o_cur = lax.rem(i, MOE_OUT_BUFFERS)
    o_prev = lax.rem(i + MOE_OUT_BUFFERS - 1, MOE_OUT_BUFFERS)
    o_prev2 = lax.rem(i + MOE_OUT_BUFFERS - 2, MOE_OUT_BUFFERS)
    src_tables = {0: src_0, 1: src_p1}
    dst_tables = {-2: dst_m2, -1: dst_m1, 0: dst_0}

    def gather_copy(rel, buf, g, k):
        row = src_tables[rel][0, 0, g * SUBLANES + k]
        src = h_hbm.at[jnp.right_shift(row, SUBLANE_SHIFT), pl.ds(jnp.bitwise_and(row, SUBLANES - 1), 1)]
        return pltpu.make_async_copy(src, xbuf.at[buf, g, pl.ds(k, 1)], gsem.at[buf])

    def scatter_copy(rel, buf, g, k):
        row = dst_tables[rel][0, 0, g * SUBLANES + k]
        dst = y_hbm.at[jnp.right_shift(row, SUBLANE_SHIFT), pl.ds(jnp.bitwise_and(row, SUBLANES - 1), 1)]
        return pltpu.make_async_copy(obuf.at[buf, g, pl.ds(k, 1)], dst, ssem.at[buf])

    def for_rows(fn):
        def body(g, carry):
            for k in range(SUBLANES):
                fn(g, k)
            return carry
        lax.fori_loop(0, n_groups, body, 0)

    @pl.when(i == 0)
    def _():
        obuf[...] = jnp.zeros_like(obuf)
        for_rows(lambda g, k: gather_copy(0, x_cur, g, k).start())
        for_rows(lambda g, k: scatter_copy(-2, o_prev2, g, k).start())

    changed = jnp.logical_or(i == 0, be_ref[i] != be_ref[jnp.maximum(i - 1, 0)])

    @pl.when(changed)
    def _():
        wgu_bf[...] = wgu_ref[0].astype(BF16)
        wd_bf[...] = wd_ref[0].astype(BF16)

    for_rows(lambda g, k: gather_copy(1, x_next, g, k).start())
    for_rows(lambda g, k: scatter_copy(-1, o_prev, g, k).start())
    for_rows(lambda g, k: scatter_copy(-2, o_prev2, g, k).wait())
    for_rows(lambda g, k: gather_copy(0, x_cur, g, k).wait())

    d_model = xbuf.shape[-1]
    x = xbuf[x_cur].reshape(n_groups * SUBLANES, d_model).astype(BF16)
    gu = _dot(x, wgu_bf[...]) + bgu_ref[0]
    ff = gu.shape[1] // 2
    gate = jnp.minimum(gu[:, :ff], SWIGLU_LIMIT)
    up = jnp.clip(gu[:, ff:], -SWIGLU_LIMIT, SWIGLU_LIMIT)
    act = gate * _sigmoid(SWIGLU_ALPHA * gate) * (up + 1.0)
    out = _dot(act.astype(BF16), wd_bf[...]) + bd_ref[0]
    obuf[o_cur] = out.reshape(n_groups, SUBLANES, d_model)

    @pl.when(i == n_blk - 1)
    def _():
        for_rows(lambda g, k: scatter_copy(0, o_cur, g, k).start())
        for_rows(lambda g, k: scatter_copy(-1, o_prev, g, k).wait())
        for_rows(lambda g, k: scatter_copy(0, o_cur, g, k).wait())
        for_rows(lambda g, k: gather_copy(1, x_next, g, k).wait())


def _moe_experts(h2, top_i, tok_rows, w_gu, b_gu, w_d, b_d):
    n_tok, d_model = h2.shape
    n_exp, _, ff2 = w_gu.shape
    bm = MOE_BLOCK
    ff = ff2 // 2
    n_out = TOP_K * n_tok + MOE_OUT_BUFFERS * bm
    assert top_i.size % bm == 0 and bm % SUBLANES == 0 and n_tok % SUBLANES == 0 and n_out % SUBLANES == 0
    block_e, table = _moe_plan(top_i, tok_rows, n_tok, n_exp, bm)
    n_blk = block_e.shape[0]
    table = table.reshape(n_blk + MOE_LEAD_BLOCKS + 1, 1, bm)
    src_rows = jnp.bitwise_and(table, 0xFFFF)
    dst_rows = jnp.right_shift(table, 16) * n_tok + src_rows

    def table_spec(rel):
        return pl.BlockSpec((1, 1, bm), lambda i, be: (i + MOE_LEAD_BLOCKS + rel, 0, 0),
                            memory_space=pltpu.SMEM)

    grid_spec = pltpu.PrefetchScalarGridSpec(
        num_scalar_prefetch=1,
        grid=(n_blk,),
        in_specs=[
            table_spec(0), table_spec(1), table_spec(-2), table_spec(-1), table_spec(0),
            pl.BlockSpec(memory_space=pl.ANY),
            pl.BlockSpec((1, d_model, ff2), lambda i, be, *_: (be[i], 0, 0)),
            pl.BlockSpec((1, 1, ff2), lambda i, be, *_: (be[i], 0, 0)),
            pl.BlockSpec((1, ff, d_model), lambda i, be, *_: (be[i], 0, 0)),
            pl.BlockSpec((1, 1, d_model), lambda i, be, *_: (be[i], 0, 0)),
        ],
        out_specs=pl.BlockSpec(memory_space=pl.ANY),
        scratch_shapes=[
            pltpu.VMEM((2, bm // SUBLANES, SUBLANES, d_model), F32),
            pltpu.VMEM((MOE_OUT_BUFFERS, bm // SUBLANES, SUBLANES, d_model), F32),
            pltpu.VMEM((d_model, ff2), BF16),
            pltpu.VMEM((ff, d_model), BF16),
            pltpu.SemaphoreType.DMA((2,)),
            pltpu.SemaphoreType.DMA((MOE_OUT_BUFFERS,)),
        ],
    )
    y4 = pl.pallas_call(
        _moe_expert_kernel,
        out_shape=jax.ShapeDtypeStruct((n_out // SUBLANES, SUBLANES, d_model), F32),
        grid_spec=grid_spec,
        compiler_params=_params(("arbitrary",)),
        name="moe_experts",
    )(block_e, src_rows, src_rows, dst_rows, dst_rows, dst_rows,
      h2.reshape(n_tok // SUBLANES, SUBLANES, d_model), w_gu, b_gu.reshape(n_exp, 1, ff2), w_d,
      b_d.reshape(n_exp, 1, d_model))
    return y4.reshape(n_out, d_model)


def _moe_combine_kernel(x_ref, gt_ref, mod_ref, fg_ref, *rest, final):
    y_refs, o_ref = rest[:TOP_K], rest[TOP_K]
    gates = gt_ref[0]
    f = jnp.zeros(x_ref.shape[1:], F32)
    for j in range(TOP_K):
        f = f + gates[:, j:j + 1] * y_refs[j][...]
    xn = x_ref[0] + mod_ref[0][5:6] * f
    if final:
        xn = xn * lax.rsqrt(jnp.mean(xn * xn, axis=-1, keepdims=True) + NORM_EPS) * fg_ref[...]
    o_ref[0] = xn


def _moe_combine(x, y4, gates, mod_l, final_g, tb, n_ctx_blocks, final):
    batch, t_len, d_model = x.shape
    off = n_ctx_blocks if final else 0
    t_out = t_len - off * tb
    per_b = t_len // tb
    per_j = batch * per_b
    mod_spec = pl.BlockSpec((1, N_MOD, d_model),
                            lambda b, t: (jnp.where(t + off < n_ctx_blocks, batch, b), 0, 0))

    def choice_spec(j):
        return pl.BlockSpec((tb, d_model), lambda b, t: (j * per_j + b * per_b + t + off, 0))

    return pl.pallas_call(
        functools.partial(_moe_combine_kernel, final=final),
        out_shape=jax.ShapeDtypeStruct((batch, t_out, d_model), F32),
        grid=(batch, t_out // tb),
        in_specs=[pl.BlockSpec((1, tb, d_model), lambda b, t: (b, t + off, 0)),
                  pl.BlockSpec((1, tb, TOP_K), lambda b, t: (b, t + off, 0)),
                  mod_spec, _const_spec((1, d_model))] + [choice_spec(j) for j in range(TOP_K)],
        out_specs=pl.BlockSpec((1, tb, d_model), lambda b, t: (b, t, 0)),
        compiler_params=_params(("parallel", "parallel")),
        name="moe_combine",
    )(x, gates, mod_l, final_g.reshape(1, d_model), *([y4] * TOP_K))


def _moe_layer(x, h2, top_i, gates, mod_l, moe_p, final_g, tb, n_ctx_blocks, last):
    batch, t_len, d_model = x.shape
    tok_rows = jnp.arange(batch * t_len, dtype=I32).reshape(batch, t_len)
    if last:
        n_ctx = n_ctx_blocks * tb
        top_i, tok_rows = top_i[:, n_ctx:], tok_rows[:, n_ctx:]
    y4 = _moe_experts(h2.reshape(batch * t_len, d_model), top_i.reshape(-1, TOP_K), tok_rows.reshape(-1),
                      *moe_p)
    return _moe_combine(x, y4, gates, mod_l, final_g, tb, n_ctx_blocks, last)


def kernel(x, c, ctx, c_ctx, ada_w, ada_b, norm1_g, norm2_g, final_g, rw_mu, rw_wr, rw_wk, rw_wv, rw_wo, rw_w0, rw_w1, rw_w2, rw_a0, rw_a1, rw_a2, rw_g1, rw_g2, rw_kk, rw_ka, rw_rk, rw_lnx_g, rw_lnx_b, rw_v0, rw_v1, rw_v2, ml_w_up, ml_conv_w, ml_conv_b, ml_wq, ml_wk, ml_wv, ml_wi, ml_bi, ml_wf, ml_bf, ml_norm_g, ml_skip, ml_w_down, moe_wr, moe_br, moe_w_gu, moe_b_gu, moe_w_d, moe_b_d):
    batch, seq, d_model = x.shape
    n_ctx = ctx.shape[1]
    depth = ada_w.shape[0]
    tb = math.gcd(MAX_TIME_BLOCK, n_ctx, seq)
    assert tb % GRID_W == 0 and n_ctx == tb and d_model % (2 * RW_HEAD_DIM) == 0
    n_ctx_blocks = n_ctx // tb

    mod = _modulation(c, c_ctx, ada_w, ada_b)
    xs = jnp.concatenate([ctx, x], axis=1)
    v_first = None
    for i in range(depth):
        last = i == depth - 1
        j = i // 2
        mod_l = mod[i]
        if i % 2 == 0:
            p = dict(mu=rw_mu[j], wr=rw_wr[j], wk=rw_wk[j], wv=rw_wv[j], wo=rw_wo[j], w0=rw_w0[j],
                     w1=rw_w1[j], w2=rw_w2[j], a0=rw_a0[j], a1=rw_a1[j], a2=rw_a2[j], g1=rw_g1[j],
                     g2=rw_g2[j], kk=rw_kk[j], ka=rw_ka[j], rk=rw_rk[j], lnx_g=rw_lnx_g[j],
                     lnx_b=rw_lnx_b[j])
            vres = None if j == 0 else (rw_v0[j - 1], rw_v1[j - 1], rw_v2[j - 1])
            r, v, gate, bonus, lp, ap, kd, bb = _rwkv_pre(xs, mod_l, norm1_g[i], p, vres, v_first,
                                                          tb, n_ctx_blocks)
            if j == 0:
                v_first = v
            y = _rwkv_scan(r, v, lp, ap, kd, bb, tb, n_ctx_blocks)
            xs, h2, top_i, gates = _rwkv_out(y, bonus, gate, xs, mod_l, p, norm2_g[i],
                                             moe_wr[i], moe_br[i], tb, n_ctx_blocks)
        else:
            p = dict(conv_w=ml_conv_w[j], conv_b=ml_conv_b[j], wq=ml_wq[j], wk=ml_wk[j], wv=ml_wv[j],
                     wi=ml_wi[j], bi=ml_bi[j], wf=ml_wf[j], bf=ml_bf[j], norm_g=ml_norm_g[j],
                     skip=ml_skip[j], w_down=ml_w_down[j])
            xm, z = _ml_up(xs, mod_l, norm1_g[i], ml_w_up[j], tb, n_ctx_blocks)
            q, k, v, xc, gts, gsums, gts_t, gsums_t = _ml_feat(xm, p, tb, n_ctx_blocks)
            hs = _ml_scan(q, k, v, gts, gsums, gts_t, gsums_t, tb, n_ctx_blocks)
            xs, h2, top_i, gates = _ml_out(hs, xc, z, xs, mod_l, p, norm2_g[i],
                                           moe_wr[i], moe_br[i], tb, n_ctx_blocks)
        moe_p = (moe_w_gu[i], moe_b_gu[i], moe_w_d[i], moe_b_d[i])
        xs = _moe_layer(xs, h2, top_i, gates, mod_l, moe_p, final_g, tb, n_ctx_blocks, last)
    return xs
```

```python
import functools
import math

import jax
import jax.numpy as jnp
from jax import lax
from jax.experimental import pallas as pl
from jax.experimental.pallas import tpu as pltpu

F32 = jnp.float32
BF16 = jnp.bfloat16
I32 = jnp.int32
HI = lax.Precision.HIGHEST

GRID_W = 64
N_MOD = 6
NORM_EPS = 1e-6
RW_HEAD_DIM = 64
RW_GN_EPS = 64e-5
RW_CHUNK = 64
ML_HEADS = 4
ML_QKV_BLOCK = 4
ML_LN_EPS = 1e-6
TOP_K = 4
SWIGLU_LIMIT = 7.0
SWIGLU_ALPHA = 1.702
MOE_BLOCK = 256
MOE_OUT_BUFFERS = 3
MOE_LEAD_BLOCKS = 2
SUBLANES = 8
SUBLANE_SHIFT = SUBLANES.bit_length() - 1
LANES = 128
MXU_WIDTH = 256
MAX_TIME_BLOCK = 256
VMEM_LIMIT = 56 * 1024 * 1024


def _dot(a, b, precision=None):
    return jnp.dot(a, b, preferred_element_type=F32, precision=precision)


def _dot_nt(a, b):
    return lax.dot_general(a, b, (((1,), (1,)), ((), ())), preferred_element_type=F32)


def _dot_tn(a, b, precision=None):
    return lax.dot_general(a, b, (((0,), (0,)), ((), ())), preferred_element_type=F32,
                           precision=precision)


def _sigmoid(x):
    return 1.0 / (1.0 + jnp.exp(-x))


def _softplus(x):
    return jnp.maximum(x, 0.0) + jnp.log(1.0 + jnp.exp(-jnp.abs(x)))


def _silu(x):
    return x * _sigmoid(x)


def _adaln(x, g, shift, scale):
    y = x * lax.rsqrt(jnp.mean(x * x, axis=-1, keepdims=True) + NORM_EPS) * g
    return y * (1.0 + scale) + shift


def _const_spec(shape):
    nd = len(shape)
    return pl.BlockSpec(shape, lambda *_: (0,) * nd, pipeline_mode=pl.Buffered(1))


def _params(sem, vmem=VMEM_LIMIT):
    return pltpu.CompilerParams(dimension_semantics=sem, vmem_limit_bytes=vmem)


def _time_block_order(d, c, n_blocks, n_ctx_blocks):
    rev = jnp.where(c < n_ctx_blocks, n_ctx_blocks - 1 - c, n_blocks - 1 - (c - n_ctx_blocks))
    return jnp.where(d == 0, c, rev)


def _mod_kernel(c_ref, w_ref, b_ref, o_ref):
    o_ref[0] = _dot(_silu(c_ref[...]), w_ref[0], HI) + b_ref[0]


def _modulation(c, c_ctx, ada_w, ada_b):
    depth, d_model, _ = ada_w.shape
    batch = c.shape[0]
    rows = -(-(batch + 1) // 8) * 8
    cc = jnp.concatenate([c, c_ctx[None], jnp.zeros((rows - batch - 1, d_model), F32)], axis=0)
    out = pl.pallas_call(
        _mod_kernel,
        out_shape=jax.ShapeDtypeStruct((depth, rows, N_MOD * d_model), F32),
        grid=(depth, N_MOD),
        in_specs=[
            pl.BlockSpec((rows, d_model), lambda l, j: (0, 0)),
            pl.BlockSpec((1, d_model, d_model), lambda l, j: (l, 0, j)),
            pl.BlockSpec((1, 1, d_model), lambda l, j: (l, 0, j)),
        ],
        out_specs=pl.BlockSpec((1, rows, d_model), lambda l, j: (l, 0, j)),
        compiler_params=_params(("parallel", "parallel")),
        name="adaln_modulation",
    )(cc, ada_w, ada_b.reshape(depth, 1, N_MOD * d_model))
    return out.reshape(depth, rows, N_MOD, d_model)


def _neighbours(h, h_prev, h_next, is_ctx, first_lat, last_lat):
    tb = h.shape[0]
    row = lax.broadcasted_iota(I32, (tb, 1), 0)
    col = jnp.bitwise_and(row, GRID_W - 1)
    left_ok = jnp.where(is_ctx, row, col) != 0
    right_ok = jnp.where(is_ctx, row - (tb - 1), col - (GRID_W - 1)) != 0
    left = jnp.where(left_ok, pltpu.roll(h, 1, axis=0), 0.0)
    right = jnp.where(right_ok, pltpu.roll(h, tb - 1, axis=0), 0.0)
    up = jnp.concatenate([h_prev, h[:tb - GRID_W]], axis=0)
    down = jnp.concatenate([h[GRID_W:], h_next], axis=0)
    up_ok = jnp.where(first_lat, row, tb) >= GRID_W
    down_ok = jnp.where(last_lat, row, 0) < tb - GRID_W
    up = jnp.where(up_ok, up, 0.0)
    down = jnp.where(down_ok, down, 0.0)
    return left, right, up, down


def _chunk_cumsum(x, reverse):
    n = x.shape[0]
    pos = jnp.bitwise_and(lax.broadcasted_iota(I32, (n, 1), 0), RW_CHUNK - 1)
    step = 1
    while step < RW_CHUNK:
        if reverse:
            x = x + jnp.where(pos < RW_CHUNK - step, pltpu.roll(x, n - step, axis=0), 0.0)
        else:
            x = x + jnp.where(pos >= step, pltpu.roll(x, step, axis=0), 0.0)
        step *= 2
    return x


def _segment_allsum(x, seg):
    width = x.shape[-1]
    shift = seg.bit_length() - 1
    assert seg == 1 << shift and MXU_WIDTH % seg == 0 and width % MXU_WIDTH == 0
    r = jnp.right_shift(lax.broadcasted_iota(I32, (MXU_WIDTH, MXU_WIDTH), 0), shift)
    c = jnp.right_shift(lax.broadcasted_iota(I32, (MXU_WIDTH, MXU_WIDTH), 1), shift)
    ones = jnp.where(r == c, 1.0, 0.0).astype(BF16)
    hi = x.astype(BF16)
    rest = x - hi.astype(F32)
    mid = rest.astype(BF16)
    lo = (rest - mid.astype(F32)).astype(BF16)
    groups = []
    for g in range(width // MXU_WIDTH):
        cols = slice(g * MXU_WIDTH, (g + 1) * MXU_WIDTH)
        groups.append(_dot(hi[:, cols], ones) + _dot(mid[:, cols], ones) + _dot(lo[:, cols], ones))
    return jnp.concatenate(groups, axis=-1)


def _route_top_k(logits):
    n, n_exp = logits.shape
    lane = lax.broadcasted_iota(I32, (n, n_exp), 1).astype(F32)
    lane_k = lax.broadcasted_iota(I32, (n, TOP_K), 1)
    work = logits
    vals, idxs = [], []
    for _ in range(TOP_K):
        m = jnp.max(work, axis=-1, keepdims=True)
        idx = jnp.min(jnp.where(work == m, lane, float(n_exp)), axis=-1, keepdims=True)
        vals.append(m)
        idxs.append(idx)
        work = jnp.where(lane == idx, -jnp.inf, work)
    es = [jnp.exp(v - vals[0]) for v in vals]
    den = es[0] + es[1] + es[2] + es[3]
    top_i = jnp.zeros((n, TOP_K), I32)
    gates = jnp.zeros((n, TOP_K), F32)
    for j in range(TOP_K):
        top_i = jnp.where(lane_k == j, idxs[j].astype(I32), top_i)
        gates = jnp.where(lane_k == j, es[j] / den, gates)
    return top_i, gates


def _router_params(moe_wr, moe_br):
    n_exp = moe_wr.shape[-1]
    assert n_exp <= LANES
    wr = jnp.pad(moe_wr, ((0, 0), (0, LANES - n_exp)))
    br = jnp.pad(moe_br, (0, LANES - n_exp), constant_values=-1e30)
    return wr, br.reshape(1, LANES)


def _post_mixer(x, y, mod, g2, wr, br, xo_ref, h2_ref, ti_ref, gt_ref):
    xn = x + mod[2:3] * y
    xo_ref[0] = xn
    h2 = _adaln(xn, g2, mod[3:4], mod[4:5])
    h2_ref[0] = h2
    top_i, gates = _route_top_k(_dot(h2, wr, HI) + br)
    ti_ref[0] = top_i
    gt_ref[0] = gates


def _rwkv_pre_kernel(*refs, has_vres, n_ctx_blocks):
    (x_ref, xp_ref, xn_ref, mod_ref, g_ref, mu_ref, wr_ref, wk_ref, wv_ref, wlw_ref, wla_ref,
     wg1_ref, w2_ref, a2_ref, g2_ref, w0_ref, a0_ref, kkp_ref, kap_ref, rk_ref) = refs[:20]
    if has_vres:
        vf_ref, v0_ref, v1_ref, v2_ref = refs[20:24]
        outs = refs[24:]
    else:
        outs = refs[20:]
    r_ref, v_ref, gg_ref, bo_ref, lp_ref, ap_ref, kd_ref, bb_ref = outs

    t = pl.program_id(1)
    n_blocks = pl.num_programs(1)
    is_ctx = t < n_ctx_blocks
    mod = mod_ref[0]
    g = g_ref[...]
    h = _adaln(x_ref[0], g, mod[0:1], mod[1:2])
    hp = _adaln(xp_ref[0], g, mod[0:1], mod[1:2])
    hn = _adaln(xn_ref[0], g, mod[0:1], mod[1:2])
    left, right, up, down = _neighbours(h, hp, hn, is_ctx, t == n_ctx_blocks, t == n_blocks - 1)
    d_model = h.shape[1]
    q = d_model // 4
    lane = lax.broadcasted_iota(I32, (1, d_model), 1)
    q2 = jnp.where(is_ctx, left, up)
    q3 = jnp.where(is_ctx, right, down)
    shifted = jnp.where(lane < q, left, jnp.where(lane < 2 * q, right, jnp.where(lane < 3 * q, q2, q3)))
    xx = shifted - h
    mu = mu_ref[...]

    def mix(j):
        return (h + xx * mu[j:j + 1]).astype(BF16)

    r = _dot(mix(0), wr_ref[...])
    k = _dot(mix(2), wk_ref[...])
    xv = mix(3)
    v = _dot(xv, wv_ref[...])
    if has_vres:
        lora = _dot(_dot(xv, v1_ref[...]).astype(BF16), v2_ref[...])
        v = v + (vf_ref[0] - v) * _sigmoid(v0_ref[...] + lora)
    gate = _dot(_sigmoid(_dot(mix(5), wg1_ref[...])).astype(BF16), g2_ref[...])
    kx = k * kkp_ref[...]
    kk = kx / jnp.maximum(jnp.sqrt(_segment_allsum(kx * kx, RW_HEAD_DIM)), 1e-12)
    r_ref[0] = r
    v_ref[0] = v
    gg_ref[0] = gate
    tw = jnp.tanh(_dot(mix(1), wlw_ref[...])).astype(BF16)
    la = _dot(mix(4), wla_ref[...]).astype(BF16)
    kap = kap_ref[...]
    kd_sum = None
    for d in range(2):
        logw = -_softplus(-(w0_ref[d:d + 1] + _dot(tw, w2_ref[d]))) - 0.5
        decay_rate = jnp.exp(logw)
        lp_ref[d, 0] = _chunk_cumsum(-decay_rate, reverse=d == 1)
        ap_ref[d, 0] = -kk * jnp.exp(decay_rate)
        a = _sigmoid(a0_ref[d:d + 1] + _dot(la, a2_ref[d]))
        kd = k * (1.0 + (a - 1.0) * kap)
        kd_ref[d, 0] = kd
        bb_ref[d, 0] = kk * a
        kd_sum = kd if kd_sum is None else kd_sum + kd
    bo_ref[0] = _segment_allsum(r * rk_ref[...] * kd_sum, RW_HEAD_DIM) * v


def _rwkv_pre(x, mod_l, norm_g, p, vres, v_first, tb, n_ctx_blocks):
    batch, t_len, d_model = x.shape
    n_blocks = t_len // tb
    hb = tb // GRID_W
    n_halo = t_len // GRID_W
    lora = p["w1"].shape[-1]
    assert 2 * lora == LANES and p["g1"].shape[-1] == LANES
    zeros = jnp.zeros((lora, d_model), F32)
    w2p = jnp.stack([jnp.concatenate([p["w2"][0], zeros]), jnp.concatenate([zeros, p["w2"][1]])])
    a2p = jnp.stack([jnp.concatenate([p["a2"][0], zeros]), jnp.concatenate([zeros, p["a2"][1]])])
    has_vres = vres is not None

    tok = pl.BlockSpec((1, tb, d_model), lambda b, t: (b, t, 0))
    tok2 = pl.BlockSpec((2, 1, tb, d_model), lambda b, t: (0, b, t, 0))
    in_specs = [
        tok,
        pl.BlockSpec((1, GRID_W, d_model), lambda b, t: (b, jnp.maximum(t * hb - 1, 0), 0)),
        pl.BlockSpec((1, GRID_W, d_model), lambda b, t: (b, jnp.minimum((t + 1) * hb, n_halo - 1), 0)),
        pl.BlockSpec((1, N_MOD, d_model), lambda b, t: (jnp.where(t < n_ctx_blocks, batch, b), 0, 0)),
    ]
    args = [x, x, x, mod_l]
    consts = [
        norm_g.reshape(1, d_model), p["mu"],
        p["wr"].astype(BF16), p["wk"].astype(BF16), p["wv"].astype(BF16),
        jnp.concatenate([p["w1"][0], p["w1"][1]], axis=1).astype(BF16),
        jnp.concatenate([p["a1"][0], p["a1"][1]], axis=1).astype(BF16),
        p["g1"].astype(BF16), w2p.astype(BF16), a2p.astype(BF16), p["g2"].astype(BF16),
        p["w0"], p["a0"], p["kk"].reshape(1, d_model), p["ka"].reshape(1, d_model),
        p["rk"].reshape(1, d_model),
    ]
    in_specs += [_const_spec(c.shape) for c in consts]
    args += consts
    if has_vres:
        v0, v1, v2 = vres
        extra = [v0.reshape(1, d_model), v1.astype(BF16), v2.astype(BF16)]
        in_specs += [tok] + [_const_spec(c.shape) for c in extra]
        args += [v_first] + extra
    one = jax.ShapeDtypeStruct((batch, t_len, d_model), F32)
    two = jax.ShapeDtypeStruct((2, batch, t_len, d_model), F32)
    return pl.pallas_call(
        functools.partial(_rwkv_pre_kernel, has_vres=has_vres, n_ctx_blocks=n_ctx_blocks),
        out_shape=[one, one, one, one, two, two, two, two],
        grid=(batch, n_blocks),
        in_specs=in_specs,
        out_specs=[tok, tok, tok, tok, tok2, tok2, tok2, tok2],
        compiler_params=_params(("parallel", "parallel")),
        name="rwkv_features",
    )(*args)


def _rwkv_scan_kernel(r_ref, v_ref, lp_ref, ap_ref, kd_ref, bb_ref, y_ref, s_ref, *, n_pairs, tb):
    d = pl.program_id(0)
    c = pl.program_id(3)
    ch = RW_CHUNK
    two = 2 * ch

    @pl.when(c == 0)
    def _():
        s_ref[...] = jnp.zeros_like(s_ref)

    sgn = jnp.where(d == 0, 1, -1)
    r2 = lax.broadcasted_iota(I32, (two, two), 0)
    c2 = lax.broadcasted_iota(I32, (two, two), 1)
    same = jnp.right_shift(r2, 6) == jnp.right_shift(c2, 6)
    delta = (jnp.bitwise_and(c2, ch - 1) - jnp.bitwise_and(r2, ch - 1)) * sgn
    before = jnp.logical_and(same, delta < 0)
    before_eq = jnp.logical_and(same, delta <= 0)
    head0 = lax.broadcasted_iota(I32, (ch, LANES), 1) < RW_HEAD_DIM
    head0w = lax.broadcasted_iota(I32, (ch, 2 * LANES), 1)
    head0w = jnp.bitwise_and(head0w, LANES - 1) < RW_HEAD_DIM
    n_sub = tb // ch
    pairs = range(n_pairs)

    def stack2(x):
        return jnp.concatenate([x, x], axis=0)

    def pick(x, mask):
        return jnp.where(mask, x[:ch], x[ch:])

    def sub_chunk(j, carry):
        jj = jnp.where(d == 0, j, n_sub - 1 - j)
        start = pl.multiple_of(jj * ch, ch)
        rows = pl.ds(start, ch)
        last = pl.ds(pl.multiple_of(start + jnp.where(d == 0, ch - 8, 0), 8), 8)
        cols = [slice(p * LANES, (p + 1) * LANES) for p in pairs]

        v, vv, r_t, a_t, b_p, k_p, m_b, m_k, p_end = [], [], [], [], [], [], [], [], []
        for p in pairs:
            lp = lp_ref[0, 0, rows, cols[p]]
            e_in = jnp.exp(lp)
            e_out = jnp.exp(-lp)
            lp_edge = lp_ref[0, 0, last, cols[p]]
            p_last = jnp.exp(jnp.where(d == 0, lp_edge[7:8], lp_edge[0:1]))
            p_end.append(p_last)
            at = ap_ref[0, 0, rows, cols[p]] * e_in
            rt = r_ref[0, rows, cols[p]] * e_in
            bt = bb_ref[0, 0, rows, cols[p]] * e_out
            kt = kd_ref[0, 0, rows, cols[p]] * e_out
            vp = v_ref[0, rows, cols[p]]
            lhs = jnp.concatenate([jnp.where(head0, at, 0.0), jnp.where(head0, 0.0, at),
                                   jnp.where(head0, rt, 0.0), jnp.where(head0, 0.0, rt)],
                                  axis=0).astype(BF16)
            m_bk = _dot_nt(lhs, jnp.concatenate([bt, bt, kt, kt], axis=0).astype(BF16))
            m_b.append(m_bk[:, :two])
            m_k.append(m_bk[:, two:])
            v.append(vp)
            vv.append(stack2(vp).astype(BF16))
            r_t.append(rt)
            a_t.append(at)
            b_p.append(bt * p_last)
            k_p.append(kt * p_last)
        n_pow = [jnp.where(before, m_b[p][:two], 0.0).astype(BF16) for p in pairs]
        a_k = [jnp.where(before, m_k[p][:two], 0.0).astype(BF16) for p in pairs]
        r_b = [jnp.where(before_eq, m_b[p][two:], 0.0).astype(BF16) for p in pairs]
        r_k = [jnp.where(before_eq, m_k[p][two:], 0.0).astype(BF16) for p in pairs]
        z = [jnp.concatenate([stack2(a_t[p]), _dot(a_k[p], vv[p])], axis=1) for p in pairs]
        z = [z[p] + _dot(n_pow[p], z[p].astype(BF16)) for p in pairs]
        for _ in range(int(math.log2(ch)) - 1):
            n_pow = [_dot(n_pow[p], n_pow[p]).astype(BF16) for p in pairs]
            z = [z[p] + _dot(n_pow[p], z[p].astype(BF16)) for p in pairs]
        q = [_dot(r_b[p], z[p].astype(BF16)) for p in pairs]
        y_k = [_dot(r_k[p], vv[p]) for p in pairs]
        w_u = [pick(z[p], head0w) for p in pairs]
        r_y = [pick(q[p], head0w) for p in pairs]
        lhs_s = [jnp.concatenate([w_u[p][:, :LANES], r_t[p] + r_y[p][:, :LANES]], axis=0).astype(BF16)
                 for p in pairs]
        u_0 = [w_u[p][:, LANES:] for p in pairs]
        y_0 = [r_y[p][:, LANES:] + pick(y_k[p], head0) for p in pairs]
        bk_p = [jnp.concatenate([b_p[p], k_p[p]], axis=0).astype(BF16) for p in pairs]

        s_old = [s_ref[p] for p in pairs]
        g = [_dot_nt(lhs_s[p], s_old[p].astype(BF16)) for p in pairs]
        u = [g[p][:ch] + u_0[p] for p in pairs]
        for p in pairs:
            y_ref[0, 0, rows, cols[p]] = g[p][ch:] + y_0[p]
        d_s = [_dot_tn(jnp.concatenate([u[p], v[p]], axis=0).astype(BF16), bk_p[p]) for p in pairs]
        for p in pairs:
            s_ref[p] = s_old[p] * p_end[p] + jnp.where(same, d_s[p], 0.0)
        return carry

    lax.fori_loop(0, n_sub, sub_chunk, 0)


def _rwkv_scan(r, v, lp, ap, kd, bb, tb, n_ctx_blocks):
    batch, t_len, d_model = r.shape
    n_blocks = t_len // tb
    lanes = min(8 * LANES, d_model)
    n_pairs = lanes // LANES

    def one_map(d, b, g, c):
        return (b, _time_block_order(d, c, n_blocks, n_ctx_blocks), g)

    def two_map(d, b, g, c):
        return (d, b, _time_block_order(d, c, n_blocks, n_ctx_blocks), g)

    one = pl.BlockSpec((1, tb, lanes), one_map)
    two = pl.BlockSpec((1, 1, tb, lanes), two_map)
    return pl.pallas_call(
        functools.partial(_rwkv_scan_kernel, n_pairs=n_pairs, tb=tb),
        out_shape=jax.ShapeDtypeStruct((2, batch, t_len, d_model), F32),
        grid=(2, batch, d_model // lanes, n_blocks),
        in_specs=[one, one, two, two, two, two],
        out_specs=two,
        scratch_shapes=[pltpu.VMEM((n_pairs, LANES, LANES), F32)],
        compiler_params=_params(("parallel", "parallel", "parallel", "arbitrary")),
        name="rwkv_scan",
    )(r, v, lp, ap, kd, bb)


def _rwkv_out_kernel(y_ref, bo_ref, gg_ref, x_ref, mod_ref, lg_ref, lb_ref,
                     wo_ref, g2_ref, wr_ref, br_ref,
                     xo_ref, h2_ref, ti_ref, gt_ref):
    y = y_ref[0, 0] + y_ref[1, 0]
    inv_n = 1.0 / RW_HEAD_DIM
    yc = y - _segment_allsum(y, RW_HEAD_DIM) * inv_n
    var = _segment_allsum(yc * yc, RW_HEAD_DIM) * inv_n
    yn = yc * lax.rsqrt(var + RW_GN_EPS) * lg_ref[...] + lb_ref[...]
    out = ((yn + bo_ref[0]) * gg_ref[0]).astype(BF16)
    _post_mixer(x_ref[0], _dot(out, wo_ref[...]), mod_ref[0], g2_ref[...], wr_ref[...], br_ref[...],
                xo_ref, h2_ref, ti_ref, gt_ref)


def _post_specs(batch, t_len, d_model, tb):
    tok = pl.BlockSpec((1, tb, d_model), lambda b, t: (b, t, 0))
    nar = pl.BlockSpec((1, tb, TOP_K), lambda b, t: (b, t, 0))
    shapes = [jax.ShapeDtypeStruct((batch, t_len, d_model), F32),
              jax.ShapeDtypeStruct((batch, t_len, d_model), F32),
              jax.ShapeDtypeStruct((batch, t_len, TOP_K), I32),
              jax.ShapeDtypeStruct((batch, t_len, TOP_K), F32)]
    return shapes, [tok, tok, nar, nar]


def _rwkv_out(y, bonus, gate, x, mod_l, p, norm2_g, moe_wr, moe_br, tb, n_ctx_blocks):
    batch, t_len, d_model = x.shape
    tok = pl.BlockSpec((1, tb, d_model), lambda b, t: (b, t, 0))
    tok2 = pl.BlockSpec((2, 1, tb, d_model), lambda b, t: (0, b, t, 0))
    mod_spec = pl.BlockSpec((1, N_MOD, d_model), lambda b, t: (jnp.where(t < n_ctx_blocks, batch, b), 0, 0))
    consts = [p["lnx_g"].reshape(1, d_model), p["lnx_b"].reshape(1, d_model), p["wo"].astype(BF16), norm2_g.reshape(1, d_model), *_router_params(moe_wr, moe_br)]
    shapes, specs = _post_specs(batch, t_len, d_model, tb)
    return pl.pallas_call(
        _rwkv_out_kernel,
        out_shape=shapes,
        grid=(batch, t_len // tb),
        in_specs=[tok2, tok, tok, tok, mod_spec] + [_const_spec(c.shape) for c in consts],
        out_specs=specs,
        compiler_params=_params(("parallel", "parallel")),
        name="rwkv_readout",
    )(y, bonus, gate, x, mod_l, *consts)


def _ml_up_kernel(x_ref, mod_ref, g_ref, w_ref, xm_ref, z_ref):
    mod = mod_ref[0]
    h = _adaln(x_ref[0], g_ref[...], mod[0:1], mod[1:2]).astype(BF16)
    up = _dot(h, w_ref[...])
    inner = xm_ref.shape[-1]
    xm_ref[0] = up[:, :inner]
    z_ref[0] = up[:, inner:]


def _ml_up(x, mod_l, norm_g, w_up, tb, n_ctx_blocks):
    batch, t_len, d_model = x.shape
    inner = w_up.shape[1] // 2
    tok = pl.BlockSpec((1, tb, d_model), lambda b, t: (b, t, 0))
    wide = pl.BlockSpec((1, tb, inner), lambda b, t: (b, t, 0))
    mod_spec = pl.BlockSpec((1, N_MOD, d_model), lambda b, t: (jnp.where(t < n_ctx_blocks, batch, b), 0, 0))
    out = jax.ShapeDtypeStruct((batch, t_len, inner), F32)
    return pl.pallas_call(
        _ml_up_kernel,
        out_shape=[out, out],
        grid=(batch, t_len // tb),
        in_specs=[tok, mod_spec, _const_spec((1, d_model)), _const_spec(w_up.shape)],
        out_specs=[wide, wide],
        compiler_params=_params(("parallel", "parallel")),
        name="mlstm_up",
    )(x, mod_l, norm_g.reshape(1, d_model), w_up.astype(BF16))


def _ml_feat_kernel(xm_ref, xp_ref, xn_ref, cw_ref, cb_ref, wq_ref, wk_ref, wv_ref, wg_ref, bg_ref,
                    q_ref, k_ref, v_ref, xc_ref, gt_ref, gs_ref, gtt_ref, gst_ref, *, n_ctx_blocks):
    t = pl.program_id(1)
    n_blocks = pl.num_programs(1)
    is_ctx = t < n_ctx_blocks
    xm = xm_ref[0]
    tb, inner = xm.shape
    row = lax.broadcasted_iota(I32, (tb, 1), 0)
    col = jnp.bitwise_and(row, GRID_W - 1)
    left_ok = jnp.where(is_ctx, row, col) != 0
    right_ok = jnp.where(is_ctx, row - (tb - 1), col - (GRID_W - 1)) != 0
    up = jnp.concatenate([xp_ref[0], xm[:tb - GRID_W]], axis=0)
    down = jnp.concatenate([xm[GRID_W:], xn_ref[0]], axis=0)
    up_ok = jnp.where(is_ctx, -1, jnp.where(t == n_ctx_blocks, row, tb)) >= GRID_W
    down_ok = jnp.where(is_ctx, tb, jnp.where(t == n_blocks - 1, row, 0)) < tb - GRID_W
    up = jnp.where(up_ok, up, 0.0)
    down = jnp.where(down_ok, down, 0.0)
    cw = cw_ref[...]
    acc = jnp.zeros((tb, inner), F32) + cb_ref[...]
    for dy, src in enumerate((up, xm, down)):
        acc = acc + jnp.where(left_ok, pltpu.roll(src, 1, axis=0), 0.0) * cw[3 * dy:3 * dy + 1]
        acc = acc + src * cw[3 * dy + 1:3 * dy + 2]
        acc = acc + jnp.where(right_ok, pltpu.roll(src, tb - 1, axis=0), 0.0) * cw[3 * dy + 2:3 * dy + 3]
    xc = _silu(acc)
    xc_ref[0] = xc
    xcb = xc.astype(BF16)
    xmb = xm.astype(BF16)
    for gidx in range(inner // LANES):
        cols = slice(gidx * LANES, (gidx + 1) * LANES)
        q_ref[0, :, cols] = _dot(xcb[:, cols], wq_ref[gidx]).astype(BF16)
        k_ref[0, :, cols] = _dot(xcb[:, cols], wk_ref[gidx]).astype(BF16)
        v_ref[0, :, cols] = _dot(xmb[:, cols], wv_ref[gidx]).astype(BF16)
    pre = (_dot(q_ref[0], wg_ref[0]) + _dot(k_ref[0], wg_ref[1]) + _dot(v_ref[0], wg_ref[2]) + bg_ref[...])
    lane = lax.broadcasted_iota(I32, pre.shape, 1)
    gates = jnp.where(lane < 2 * ML_HEADS, pre, -_softplus(-pre))
    gt_ref[0] = gates
    fwd = bwd = gates
    step = 1
    while step < tb:
        fwd = fwd + jnp.where(row >= step, pltpu.roll(fwd, step, axis=0), 0.0)
        bwd = bwd + jnp.where(row < tb - step, pltpu.roll(bwd, tb - step, axis=0), 0.0)
        step *= 2
    backward_lane = jnp.bitwise_and(jnp.right_shift(lane, 2), 1) == 1
    sums = jnp.where(backward_lane, bwd, fwd)
    gs_ref[0] = sums
    n_gate = gtt_ref.shape[1]
    gtt_ref[0] = jnp.transpose(gates)[:n_gate]
    gst_ref[0] = jnp.transpose(sums)[:n_gate]


def _block_diag_lanes(w):
    n_blk, blk, _ = w.shape
    per = LANES // blk
    w = w.reshape(n_blk // per, per, blk, blk)
    eye = jnp.eye(per, dtype=w.dtype)
    return jnp.einsum("gaio,ab->gaibo", w, eye).reshape(n_blk // per, LANES, LANES)


def _ml_feat(xm, p, tb, n_ctx_blocks):
    batch, t_len, inner = xm.shape
    hb = tb // GRID_W
    n_halo = t_len // GRID_W
    wide = pl.BlockSpec((1, tb, inner), lambda b, t: (b, t, 0))
    gate_w = jnp.concatenate([p["wi"][0], p["wi"][1], p["wf"][0], p["wf"][1]], axis=1)
    n_gate = gate_w.shape[1]
    gate_w = jnp.pad(gate_w, ((0, 0), (0, LANES - n_gate))).reshape(3, inner, LANES).astype(BF16)
    gate_b = jnp.pad(jnp.concatenate([p["bi"][0], p["bi"][1], p["bf"][0], p["bf"][1]]),
                     (0, LANES - n_gate)).reshape(1, LANES)
    consts = [p["conv_w"].reshape(9, inner), p["conv_b"].reshape(1, inner),
              _block_diag_lanes(p["wq"]).astype(BF16), _block_diag_lanes(p["wk"]).astype(BF16),
              _block_diag_lanes(p["wv"]).astype(BF16), gate_w, gate_b]
    half = jax.ShapeDtypeStruct((batch, t_len, inner), BF16)
    narrow = jax.ShapeDtypeStruct((batch, t_len, LANES), F32)
    narrow_spec = pl.BlockSpec((1, tb, LANES), lambda b, t: (b, t, 0))
    flat = jax.ShapeDtypeStruct((batch, n_gate, t_len), F32)
    flat_spec = pl.BlockSpec((1, n_gate, tb), lambda b, t: (b, 0, t))
    return pl.pallas_call(
        functools.partial(_ml_feat_kernel, n_ctx_blocks=n_ctx_blocks),
        out_shape=[half, half, half, jax.ShapeDtypeStruct((batch, t_len, inner), F32), narrow, narrow,
                   flat, flat],
        grid=(batch, t_len // tb),
        in_specs=[wide,
                  pl.BlockSpec((1, GRID_W, inner), lambda b, t: (b, jnp.maximum(t * hb - 1, 0), 0)),
                  pl.BlockSpec((1, GRID_W, inner), lambda b, t: (b, jnp.minimum((t + 1) * hb, n_halo - 1), 0)),
                  ] + [_const_spec(c.shape) for c in consts],
        out_specs=[wide, wide, wide, wide, narrow_spec, narrow_spec, flat_spec, flat_spec],
        compiler_params=_params(("parallel", "parallel")),
        name="mlstm_features",
    )(xm, xm, xm, *consts)


def _ml_scan_kernel(q_ref, k_ref, v_ref, gc_ref, sc_ref, gr_ref, sr_ref, h_ref, c_ref, n_ref, m_ref, *, scale):
    d = pl.program_id(0)
    hd = pl.program_id(2)
    c = pl.program_id(3)

    @pl.when(c == 0)
    def _():
        c_ref[...] = jnp.zeros_like(c_ref)
        n_ref[...] = jnp.zeros_like(n_ref)
        m_ref[...] = jnp.zeros_like(m_ref)

    q = q_ref[0]
    k = k_ref[0]
    v = v_ref[0]
    ln = q.shape[0]
    sgn = jnp.where(d == 0, 1, -1)
    ri = lax.broadcasted_iota(I32, (ln, ln), 0)
    ci = lax.broadcasted_iota(I32, (ln, ln), 1)
    allowed = (ci - ri) * sgn <= 0
    gcol = gc_ref[0]
    grow = gr_ref[0]
    i_idx = d * ML_HEADS + hd
    f_idx = 2 * ML_HEADS + i_idx
    lane_c = lax.broadcasted_iota(I32, gcol.shape, 1)
    sub_r = lax.broadcasted_iota(I32, grow.shape, 0)
    b_col = jnp.sum(jnp.where(lane_c == f_idx, sc_ref[0], 0.0), axis=1, keepdims=True)
    f_col = jnp.sum(jnp.where(lane_c == f_idx, gcol, 0.0), axis=1, keepdims=True)
    i_col = jnp.sum(jnp.where(lane_c == i_idx, gcol, 0.0), axis=1, keepdims=True)
    b_row = jnp.sum(jnp.where(sub_r == f_idx, sr_ref[0], 0.0), axis=0, keepdims=True)
    i_row = jnp.sum(jnp.where(sub_r == i_idx, grow, 0.0), axis=0, keepdims=True)
    m_prev = m_ref[0:1, 0:1]
    log_inter = b_col + m_prev
    log_intra = jnp.where(allowed, b_col - b_row + i_row, -jnp.inf)
    m_t = jnp.maximum(log_inter, jnp.max(log_intra, axis=1, keepdims=True))
    w_intra = jnp.exp(log_intra - m_t)
    w_inter = jnp.exp(log_inter - m_t)
    s = _dot_nt(q, k) * scale * w_intra
    ct = c_ref[...]
    nrow = n_ref[0:1, :]
    qf = q.astype(F32)
    num = _dot(s.astype(BF16), v) + w_inter * _dot(q, ct.astype(BF16))
    den = jnp.sum(s, axis=1, keepdims=True) + w_inter * jnp.sum(qf * nrow, axis=1, keepdims=True)
    h_ref[0, 0] = num / jnp.maximum(jnp.abs(den), jnp.exp(-m_t))
    gsum = jnp.sum(f_col, axis=0, keepdims=True)
    log_loc = gsum - b_col + i_col
    m_new = jnp.maximum(gsum + m_prev, jnp.max(log_loc, axis=0, keepdims=True))
    w_loc = jnp.exp(log_loc - m_new)
    w_old = jnp.exp(gsum + m_prev - m_new)
    kw = k.astype(F32) * (scale * w_loc)
    c_ref[...] = w_old * ct + _dot_tn(kw.astype(BF16), v)
    n_ref[0:1, :] = w_old * nrow + jnp.sum(kw, axis=0, keepdims=True)
    m_ref[...] = jnp.zeros_like(m_ref) + m_new


def _ml_scan(q, k, v, gates, gate_sums, gates_t, sums_t, tb, n_ctx_blocks):
    batch, t_len, inner = q.shape
    dh = inner // ML_HEADS
    n_blocks = t_len // tb
    n_gate = gates_t.shape[1]

    def head_map(d, b, h, c):
        return (b, _time_block_order(d, c, n_blocks, n_ctx_blocks), h)

    def col_map(d, b, h, c):
        return (b, _time_block_order(d, c, n_blocks, n_ctx_blocks), 0)

    def row_map(d, b, h, c):
        return (b, 0, _time_block_order(d, c, n_blocks, n_ctx_blocks))

    def out_map(d, b, h, c):
        return (d, b, _time_block_order(d, c, n_blocks, n_ctx_blocks), h)

    head = pl.BlockSpec((1, tb, dh), head_map)
    col = pl.BlockSpec((1, tb, LANES), col_map)
    row = pl.BlockSpec((1, n_gate, tb), row_map)
    return pl.pallas_call(
        functools.partial(_ml_scan_kernel, scale=float(dh) ** -0.5),
        out_shape=jax.ShapeDtypeStruct((2, batch, t_len, inner), F32),
        grid=(2, batch, ML_HEADS, n_blocks),
        in_specs=[head, head, head, col, col, row, row],
        out_specs=pl.BlockSpec((1, 1, tb, dh), out_map),
        scratch_shapes=[pltpu.VMEM((dh, dh), F32), pltpu.VMEM((8, dh), F32), pltpu.VMEM((8, LANES), F32)],
        compiler_params=_params(("parallel", "parallel", "parallel", "arbitrary")),
        name="mlstm_scan",
    )(q, k, v, gates, gate_sums, gates_t, sums_t)


def _ml_out_kernel(hs_ref, xc_ref, z_ref, x_ref, mod_ref, ng_ref, sk_ref, wd_ref, g2_ref, wr_ref, br_ref,
                   xo_ref, h2_ref, ti_ref, gt_ref):
    hs = hs_ref[0, 0] + hs_ref[1, 0]
    inner = hs.shape[1]
    dh = inner // ML_HEADS
    parts = []
    for hd in range(ML_HEADS):
        blk = hs[:, hd * dh:(hd + 1) * dh]
        mean = jnp.mean(blk, axis=1, keepdims=True)
        cen = blk - mean
        var = jnp.mean(cen * cen, axis=1, keepdims=True)
        parts.append(cen * lax.rsqrt(var + ML_LN_EPS))
    hn = jnp.concatenate(parts, axis=1) * ng_ref[...]
    y = ((hn + sk_ref[...] * xc_ref[0]) * _silu(z_ref[0])).astype(BF16)
    _post_mixer(x_ref[0], _dot(y, wd_ref[...]), mod_ref[0], g2_ref[...], wr_ref[...], br_ref[...],
                xo_ref, h2_ref, ti_ref, gt_ref)


def _ml_out(hs, xc, z, x, mod_l, p, norm2_g, moe_wr, moe_br, tb, n_ctx_blocks):
    batch, t_len, d_model = x.shape
    inner = xc.shape[-1]
    tok = pl.BlockSpec((1, tb, d_model), lambda b, t: (b, t, 0))
    wide = pl.BlockSpec((1, tb, inner), lambda b, t: (b, t, 0))
    wide2 = pl.BlockSpec((2, 1, tb, inner), lambda b, t: (0, b, t, 0))
    mod_spec = pl.BlockSpec((1, N_MOD, d_model), lambda b, t: (jnp.where(t < n_ctx_blocks, batch, b), 0, 0))
    consts = [p["norm_g"].reshape(1, inner), p["skip"].reshape(1, inner), p["w_down"].astype(BF16),
              norm2_g.reshape(1, d_model), *_router_params(moe_wr, moe_br)]
    shapes, specs = _post_specs(batch, t_len, d_model, tb)
    return pl.pallas_call(
        _ml_out_kernel,
        out_shape=shapes,
        grid=(batch, t_len // tb),
        in_specs=[wide2, wide, wide, tok, mod_spec] + [_const_spec(c.shape) for c in consts],
        out_specs=specs,
        compiler_params=_params(("parallel", "parallel")),
        name="mlstm_readout",
    )(hs, xc, z, x, mod_l, *consts)


def _moe_plan(top_i, tok_rows, n_rows, n_exp, bm):
    n_asg = top_i.size
    assert n_rows < (1 << 16) and MOE_OUT_BUFFERS * bm <= n_rows
    flat_e = top_i.reshape(n_asg)
    onehot = (flat_e[:, None] == jnp.arange(n_exp, dtype=I32)[None, :]).astype(I32)
    csum = jnp.cumsum(onehot, axis=0)
    counts = csum[-1]
    padded = (counts + bm - 1) // bm * bm
    pad_end = jnp.cumsum(padded)
    pad_start = pad_end - padded
    dest = jnp.sum(onehot * (csum - 1 + pad_start[None, :]), axis=1)
    n_blk = n_asg // bm + n_exp
    choice = jnp.arange(TOP_K, dtype=I32)
    real = jnp.bitwise_or(tok_rows[:, None], jnp.left_shift(choice, 16)[None, :]).reshape(n_asg)
    slots = jnp.arange((n_blk + MOE_LEAD_BLOCKS + 1) * bm, dtype=I32)
    buf = (slots // bm - MOE_LEAD_BLOCKS) % MOE_OUT_BUFFERS
    pad = jnp.bitwise_or(buf * bm + slots % bm, TOP_K << 16)
    table = pad.at[MOE_LEAD_BLOCKS * bm + dest].set(real)
    starts = jnp.arange(n_blk, dtype=I32) * bm
    block_e = jnp.minimum(jnp.sum((pad_end[None, :] <= starts[:, None]).astype(I32), axis=1), n_exp - 1)
    return block_e, table


def _moe_expert_kernel(be_ref, src_0, src_p1, dst_m2, dst_m1, dst_0, h_hbm, wgu_ref, bgu_ref, wd_ref, bd_ref,
                       y_hbm, xbuf, obuf, wgu_bf, wd_bf, gsem, ssem):
    i = pl.program_id(0)
    n_blk = pl.num_programs(0)
    n_groups = xbuf.shape[1]
    x_cur = lax.rem(i, 2)
    x_next = 1 - x_cur
    o_cur = lax.rem(i, MOE_OUT_BUFFERS)
    o_prev = lax.rem(i + MOE_OUT_BUFFERS - 1, MOE_OUT_BUFFERS)
    o_prev2 = lax.rem(i + MOE_OUT_BUFFERS - 2, MOE_OUT_BUFFERS)
    src_tables = {0: src_0, 1: src_p1}
    dst_tables = {-2: dst_m2, -1: dst_m1, 0: dst_0}

    def gather_copy(rel, buf, g, k):
        row = src_tables[rel][0, 0, g * SUBLANES + k]
        src = h_hbm.at[jnp.right_shift(row, SUBLANE_SHIFT), pl.ds(jnp.bitwise_and(row, SUBLANES - 1), 1)]
        return pltpu.make_async_copy(src, xbuf.at[buf, g, pl.ds(k, 1)], gsem.at[buf])

    def scatter_copy(rel, buf, g, k):
        row = dst_tables[rel][0, 0, g * SUBLANES + k]
        dst = y_hbm.at[jnp.right_shift(row, SUBLANE_SHIFT), pl.ds(jnp.bitwise_and(row, SUBLANES - 1), 1)]
        return pltpu.make_async_copy(obuf.at[buf, g, pl.ds(k, 1)], dst, ssem.at[buf])

    def for_rows(fn):
        def body(g, carry):
            for k in range(SUBLANES):
                fn(g, k)
            return carry
        lax.fori_loop(0, n_groups, body, 0)

    @pl.when(i == 0)
    def _():
        obuf[...] = jnp.zeros_like(obuf)
        for_rows(lambda g, k: gather_copy(0, x_cur, g, k).start())
        for_rows(lambda g, k: scatter_copy(-2, o_prev2, g, k).start())

    changed = jnp.logical_or(i == 0, be_ref[i] != be_ref[jnp.maximum(i - 1, 0)])

    @pl.when(changed)
    def _():
        wgu_bf[...] = wgu_ref[0, 0].astype(BF16)
        wd_bf[...] = wd_ref[0, 0].astype(BF16)

    for_rows(lambda g, k: gather_copy(1, x_next, g, k).start())
    for_rows(lambda g, k: scatter_copy(-1, o_prev, g, k).start())
    for_rows(lambda g, k: scatter_copy(-2, o_prev2, g, k).wait())
    for_rows(lambda g, k: gather_copy(0, x_cur, g, k).wait())

    d_model = xbuf.shape[-1]
    x = xbuf[x_cur].reshape(n_groups * SUBLANES, d_model).astype(BF16)
    gu = _dot(x, wgu_bf[...]) + bgu_ref[0, 0]
    ff = gu.shape[1] // 2
    gate = jnp.minimum(gu[:, :ff], SWIGLU_LIMIT)
    up = jnp.clip(gu[:, ff:], -SWIGLU_LIMIT, SWIGLU_LIMIT)
    act = gate * _sigmoid(SWIGLU_ALPHA * gate) * (up + 1.0)
    out = _dot(act.astype(BF16), wd_bf[...]) + bd_ref[0, 0]
    obuf[o_cur] = out.reshape(n_groups, SUBLANES, d_model)

    @pl.when(i == n_blk - 1)
    def _():
        for_rows(lambda g, k: scatter_copy(0, o_cur, g, k).start())
        for_rows(lambda g, k: scatter_copy(-1, o_prev, g, k).wait())
        for_rows(lambda g, k: scatter_copy(0, o_cur, g, k).wait())
        for_rows(lambda g, k: gather_copy(1, x_next, g, k).wait())


def _moe_experts(h2, top_i, tok_rows, layer, w_gu, b_gu, w_d, b_d):
    n_tok, d_model = h2.shape
    depth, n_exp, _, ff2 = w_gu.shape
    bm = MOE_BLOCK
    ff = ff2 // 2
    n_out = TOP_K * n_tok + MOE_OUT_BUFFERS * bm
    assert top_i.size % bm == 0 and bm % SUBLANES == 0 and n_tok % SUBLANES == 0 and n_out % SUBLANES == 0
    block_e, table = _moe_plan(top_i, tok_rows, n_tok, n_exp, bm)
    n_blk = block_e.shape[0]
    table = table.reshape(n_blk + MOE_LEAD_BLOCKS + 1, 1, bm)
    src_rows = jnp.bitwise_and(table, 0xFFFF)
    dst_rows = jnp.right_shift(table, 16) * n_tok + src_rows

    def table_spec(rel):
        return pl.BlockSpec((1, 1, bm), lambda i, be: (i + MOE_LEAD_BLOCKS + rel, 0, 0),
                            memory_space=pltpu.SMEM)

    grid_spec = pltpu.PrefetchScalarGridSpec(
        num_scalar_prefetch=1,
        grid=(n_blk,),
        in_specs=[
            table_spec(0), table_spec(1), table_spec(-2), table_spec(-1), table_spec(0),
            pl.BlockSpec(memory_space=pl.ANY),
            pl.BlockSpec((1, 1, d_model, ff2), lambda i, be: (layer, be[i], 0, 0)),
            pl.BlockSpec((1, 1, 1, ff2), lambda i, be: (layer, be[i], 0, 0)),
            pl.BlockSpec((1, 1, ff, d_model), lambda i, be: (layer, be[i], 0, 0)),
            pl.BlockSpec((1, 1, 1, d_model), lambda i, be: (layer, be[i], 0, 0)),
        ],
        out_specs=pl.BlockSpec(memory_space=pl.ANY),
        scratch_shapes=[
            pltpu.VMEM((2, bm // SUBLANES, SUBLANES, d_model), F32),
            pltpu.VMEM((MOE_OUT_BUFFERS, bm // SUBLANES, SUBLANES, d_model), F32),
            pltpu.VMEM((d_model, ff2), BF16),
            pltpu.VMEM((ff, d_model), BF16),
            pltpu.SemaphoreType.DMA((2,)),
            pltpu.SemaphoreType.DMA((MOE_OUT_BUFFERS,)),
        ],
    )
    y4 = pl.pallas_call(
        _moe_expert_kernel,
        out_shape=jax.ShapeDtypeStruct((n_out // SUBLANES, SUBLANES, d_model), F32),
        grid_spec=grid_spec,
        compiler_params=_params(("arbitrary",)),
        name="moe_experts",
    )(block_e, src_rows, src_rows, dst_rows, dst_rows, dst_rows,
      h2.reshape(n_tok // SUBLANES, SUBLANES, d_model), w_gu, b_gu.reshape(depth, n_exp, 1, ff2), w_d,
      b_d.reshape(depth, n_exp, 1, d_model))
    return y4.reshape(n_out, d_model)


def _moe_combine_kernel(x_ref, gt_ref, mod_ref, fg_ref, *rest, final):
    y_refs, o_ref = rest[:TOP_K], rest[TOP_K]
    gates = gt_ref[0]
    f = jnp.zeros(x_ref.shape[1:], F32)
    for j in range(TOP_K):
        f = f + gates[:, j:j + 1] * y_refs[j][...]
    xn = x_ref[0] + mod_ref[0][5:6] * f
    if final:
        xn = xn * lax.rsqrt(jnp.mean(xn * xn, axis=-1, keepdims=True) + NORM_EPS) * fg_ref[...]
    o_ref[0] = xn


def _moe_combine(x, y4, gates, mod_l, final_g, tb, n_ctx_blocks, final):
    batch, t_len, d_model = x.shape
    off = n_ctx_blocks if final else 0
    t_out = t_len - off * tb
    per_b = t_len // tb
    per_j = batch * per_b
    mod_spec = pl.BlockSpec((1, N_MOD, d_model),
                            lambda b, t: (jnp.where(t + off < n_ctx_blocks, batch, b), 0, 0))

    def choice_spec(j):
        return pl.BlockSpec((tb, d_model), lambda b, t: (j * per_j + b * per_b + t + off, 0))

    return pl.pallas_call(
        functools.partial(_moe_combine_kernel, final=final),
        out_shape=jax.ShapeDtypeStruct((batch, t_out, d_model), F32),
        grid=(batch, t_out // tb),
        in_specs=[pl.BlockSpec((1, tb, d_model), lambda b, t: (b, t + off, 0)),
                  pl.BlockSpec((1, tb, TOP_K), lambda b, t: (b, t + off, 0)),
                  mod_spec, _const_spec((1, d_model))] + [choice_spec(j) for j in range(TOP_K)],
        out_specs=pl.BlockSpec((1, tb, d_model), lambda b, t: (b, t, 0)),
        compiler_params=_params(("parallel", "parallel")),
        name="moe_combine",
    )(x, gates, mod_l, final_g.reshape(1, d_model), *([y4] * TOP_K))


def _moe_layer(x, h2, top_i, gates, mod_l, layer, moe_p, final_g, tb, n_ctx_blocks, last):
    batch, t_len, d_model = x.shape
    tok_rows = jnp.arange(batch * t_len, dtype=I32).reshape(batch, t_len)
    if last:
        n_ctx = n_ctx_blocks * tb
        top_i, tok_rows = top_i[:, n_ctx:], tok_rows[:, n_ctx:]
    y4 = _moe_experts(h2.reshape(batch * t_len, d_model), top_i.reshape(-1, TOP_K), tok_rows.reshape(-1),
                      layer, *moe_p)
    return _moe_combine(x, y4, gates, mod_l, final_g, tb, n_ctx_blocks, last)


def kernel(x, c, ctx, c_ctx, ada_w, ada_b, norm1_g, norm2_g, final_g, rw_mu, rw_wr, rw_wk, rw_wv, rw_wo, rw_w0, rw_w1, rw_w2, rw_a0, rw_a1, rw_a2, rw_g1, rw_g2, rw_kk, rw_ka, rw_rk, rw_lnx_g, rw_lnx_b, rw_v0, rw_v1, rw_v2, ml_w_up, ml_conv_w, ml_conv_b, ml_wq, ml_wk, ml_wv, ml_wi, ml_bi, ml_wf, ml_bf, ml_norm_g, ml_skip, ml_w_down, moe_wr, moe_br, moe_w_gu, moe_b_gu, moe_w_d, moe_b_d):
    batch, seq, d_model = x.shape
    n_ctx = ctx.shape[1]
    depth = ada_w.shape[0]
    tb = math.gcd(MAX_TIME_BLOCK, n_ctx, seq)
    assert tb % GRID_W == 0 and n_ctx == tb and d_model % (2 * RW_HEAD_DIM) == 0
    n_ctx_blocks = n_ctx // tb

    mod = _modulation(c, c_ctx, ada_w, ada_b)
    xs = jnp.concatenate([ctx, x], axis=1)
    v_first = None
    for i in range(depth):
        last = i == depth - 1
        j = i // 2
        mod_l = mod[i]
        if i % 2 == 0:
            p = dict(mu=rw_mu[j], wr=rw_wr[j], wk=rw_wk[j], wv=rw_wv[j], wo=rw_wo[j], w0=rw_w0[j],
                     w1=rw_w1[j], w2=rw_w2[j], a0=rw_a0[j], a1=rw_a1[j], a2=rw_a2[j], g1=rw_g1[j],
                     g2=rw_g2[j], kk=rw_kk[j], ka=rw_ka[j], rk=rw_rk[j], lnx_g=rw_lnx_g[j],
                     lnx_b=rw_lnx_b[j])
            vres = None if j == 0 else (rw_v0[j - 1], rw_v1[j - 1], rw_v2[j - 1])
            r, v, gate, bonus, lp, ap, kd, bb = _rwkv_pre(xs, mod_l, norm1_g[i], p, vres, v_first,
                                                          tb, n_ctx_blocks)
            if j == 0:
                v_first = v
            y = _rwkv_scan(r, v, lp, ap, kd, bb, tb, n_ctx_blocks)
            xs, h2, top_i, gates = _rwkv_out(y, bonus, gate, xs, mod_l, p, norm2_g[i],
                                             moe_wr[i], moe_br[i], tb, n_ctx_blocks)
        else:
            p = dict(conv_w=ml_conv_w[j], conv_b=ml_conv_b[j], wq=ml_wq[j], wk=ml_wk[j], wv=ml_wv[j],
                     wi=ml_wi[j], bi=ml_bi[j], wf=ml_wf[j], bf=ml_bf[j], norm_g=ml_norm_g[j],
                     skip=ml_skip[j], w_down=ml_w_down[j])
            xm, z = _ml_up(xs, mod_l, norm1_g[i], ml_w_up[j], tb, n_ctx_blocks)
            q, k, v, xc, gts, gsums, gts_t, gsums_t = _ml_feat(xm, p, tb, n_ctx_blocks)
            hs = _ml_scan(q, k, v, gts, gsums, gts_t, gsums_t, tb, n_ctx_blocks)
            xs, h2, top_i, gates = _ml_out(hs, xc, z, xs, mod_l, p, norm2_g[i],
                                           moe_wr[i], moe_br[i], tb, n_ctx_blocks)
        moe_p = (moe_w_gu, moe_b_gu, moe_w_d, moe_b_d)
        xs = _moe_layer(xs, h2, top_i, gates, mod_l, i, moe_p, final_g, tb, n_ctx_blocks, last)
    return xs
```

```python
import functools
import math

import jax
import jax.numpy as jnp
from jax import lax
from jax.experimental import pallas as pl
from jax.experimental.pallas import tpu as pltpu

F32 = jnp.float32
BF16 = jnp.bfloat16
I32 = jnp.int32
HI = lax.Precision.HIGHEST

GRID_W = 64
N_MOD = 6
NORM_EPS = 1e-6
RW_HEAD_DIM = 64
RW_GN_EPS = 64e-5
RW_CHUNK = 64
ML_HEADS = 4
ML_QKV_BLOCK = 4
ML_LN_EPS = 1e-6
TOP_K = 4
SWIGLU_LIMIT = 7.0
SWIGLU_ALPHA = 1.702
MOE_BLOCK = 256
MOE_OUT_BUFFERS = 3
MOE_LEAD_BLOCKS = 2
SUBLANES = 8
SUBLANE_SHIFT = SUBLANES.bit_length() - 1
LANES = 128
MXU_WIDTH = 256
MAX_TIME_BLOCK = 256
VMEM_LIMIT = 56 * 1024 * 1024


def _dot(a, b, precision=None):
    return jnp.dot(a, b, preferred_element_type=F32, precision=precision)


def _dot_nt(a, b):
    return lax.dot_general(a, b, (((1,), (1,)), ((), ())), preferred_element_type=F32)


def _dot_tn(a, b, precision=None):
    return lax.dot_general(a, b, (((0,), (0,)), ((), ())), preferred_element_type=F32,
                           precision=precision)


def _sigmoid(x):
    return 1.0 / (1.0 + jnp.exp(-x))


def _softplus(x):
    return jnp.maximum(x, 0.0) + jnp.log(1.0 + jnp.exp(-jnp.abs(x)))


def _silu(x):
    return x * _sigmoid(x)


def _adaln(x, g, shift, scale):
    y = x * lax.rsqrt(jnp.mean(x * x, axis=-1, keepdims=True) + NORM_EPS) * g
    return y * (1.0 + scale) + shift


def _const_spec(shape):
    nd = len(shape)
    return pl.BlockSpec(shape, lambda *_: (0,) * nd, pipeline_mode=pl.Buffered(1))


def _params(sem, vmem=VMEM_LIMIT):
    return pltpu.CompilerParams(dimension_semantics=sem, vmem_limit_bytes=vmem)


def _time_block_order(d, c, n_blocks, n_ctx_blocks):
    rev = jnp.where(c < n_ctx_blocks, n_ctx_blocks - 1 - c, n_blocks - 1 - (c - n_ctx_blocks))
    return jnp.where(d == 0, c, rev)


def _mod_kernel(c_ref, w_ref, b_ref, o_ref):
    o_ref[0] = _dot(_silu(c_ref[...]), w_ref[0], HI) + b_ref[0]


def _modulation(c, c_ctx, ada_w, ada_b):
    depth, d_model, _ = ada_w.shape
    batch = c.shape[0]
    rows = -(-(batch + 1) // 8) * 8
    cc = jnp.concatenate([c, c_ctx[None], jnp.zeros((rows - batch - 1, d_model), F32)], axis=0)
    out = pl.pallas_call(
        _mod_kernel,
        out_shape=jax.ShapeDtypeStruct((depth, rows, N_MOD * d_model), F32),
        grid=(depth, N_MOD),
        in_specs=[
            pl.BlockSpec((rows, d_model), lambda l, j: (0, 0)),
            pl.BlockSpec((1, d_model, d_model), lambda l, j: (l, 0, j)),
            pl.BlockSpec((1, 1, d_model), lambda l, j: (l, 0, j)),
        ],
        out_specs=pl.BlockSpec((1, rows, d_model), lambda l, j: (l, 0, j)),
        compiler_params=_params(("parallel", "parallel")),
        name="adaln_modulation",
    )(cc, ada_w, ada_b.reshape(depth, 1, N_MOD * d_model))
    return out.reshape(depth, rows, N_MOD, d_model)


def _neighbours(h, h_prev, h_next, is_ctx, first_lat, last_lat):
    tb = h.shape[0]
    row = lax.broadcasted_iota(I32, (tb, 1), 0)
    col = jnp.bitwise_and(row, GRID_W - 1)
    left_ok = jnp.where(is_ctx, row, col) != 0
    right_ok = jnp.where(is_ctx, row - (tb - 1), col - (GRID_W - 1)) != 0
    left = jnp.where(left_ok, pltpu.roll(h, 1, axis=0), 0.0)
    right = jnp.where(right_ok, pltpu.roll(h, tb - 1, axis=0), 0.0)
    up = jnp.concatenate([h_prev, h[:tb - GRID_W]], axis=0)
    down = jnp.concatenate([h[GRID_W:], h_next], axis=0)
    up_ok = jnp.where(first_lat, row, tb) >= GRID_W
    down_ok = jnp.where(last_lat, row, 0) < tb - GRID_W
    up = jnp.where(up_ok, up, 0.0)
    down = jnp.where(down_ok, down, 0.0)
    return left, right, up, down


def _chunk_cumsum(x, reverse):
    n = x.shape[0]
    pos = jnp.bitwise_and(lax.broadcasted_iota(I32, (n, 1), 0), RW_CHUNK - 1)
    step = 1
    while step < RW_CHUNK:
        if reverse:
            x = x + jnp.where(pos < RW_CHUNK - step, pltpu.roll(x, n - step, axis=0), 0.0)
        else:
            x = x + jnp.where(pos >= step, pltpu.roll(x, step, axis=0), 0.0)
        step *= 2
    return x


def _segment_allsum(x, seg):
    width = x.shape[-1]
    shift = seg.bit_length() - 1
    assert seg == 1 << shift and MXU_WIDTH % seg == 0 and width % MXU_WIDTH == 0
    r = jnp.right_shift(lax.broadcasted_iota(I32, (MXU_WIDTH, MXU_WIDTH), 0), shift)
    c = jnp.right_shift(lax.broadcasted_iota(I32, (MXU_WIDTH, MXU_WIDTH), 1), shift)
    ones = jnp.where(r == c, 1.0, 0.0).astype(BF16)
    hi = x.astype(BF16)
    rest = x - hi.astype(F32)
    mid = rest.astype(BF16)
    lo = (rest - mid.astype(F32)).astype(BF16)
    groups = []
    for g in range(width // MXU_WIDTH):
        cols = slice(g * MXU_WIDTH, (g + 1) * MXU_WIDTH)
        groups.append(_dot(hi[:, cols], ones) + _dot(mid[:, cols], ones) + _dot(lo[:, cols], ones))
    return jnp.concatenate(groups, axis=-1)


def _route_top_k(logits):
    n, n_exp = logits.shape
    lane = lax.broadcasted_iota(I32, (n, n_exp), 1).astype(F32)
    lane_k = lax.broadcasted_iota(I32, (n, TOP_K), 1)
    work = logits
    vals, idxs = [], []
    for _ in range(TOP_K):
        m = jnp.max(work, axis=-1, keepdims=True)
        idx = jnp.min(jnp.where(work == m, lane, float(n_exp)), axis=-1, keepdims=True)
        vals.append(m)
        idxs.append(idx)
        work = jnp.where(lane == idx, -jnp.inf, work)
    es = [jnp.exp(v - vals[0]) for v in vals]
    den = es[0] + es[1] + es[2] + es[3]
    top_i = jnp.zeros((n, TOP_K), I32)
    gates = jnp.zeros((n, TOP_K), F32)
    for j in range(TOP_K):
        top_i = jnp.where(lane_k == j, idxs[j].astype(I32), top_i)
        gates = jnp.where(lane_k == j, es[j] / den, gates)
    return top_i, gates


def _router_params(moe_wr, moe_br):
    n_exp = moe_wr.shape[-1]
    assert n_exp <= LANES
    wr = jnp.pad(moe_wr, ((0, 0), (0, LANES - n_exp)))
    br = jnp.pad(moe_br, (0, LANES - n_exp), constant_values=-1e30)
    return wr, br.reshape(1, LANES)


def _post_mixer(x, y, mod, g2, wr, br, xo_ref, h2_ref, ti_ref, gt_ref):
    xn = x + mod[2:3] * y
    xo_ref[0] = xn
    h2 = _adaln(xn, g2, mod[3:4], mod[4:5])
    h2_ref[0] = h2
    top_i, gates = _route_top_k(_dot(h2, wr, HI) + br)
    ti_ref[0] = top_i
    gt_ref[0] = gates


def _rwkv_pre_kernel(*refs, has_vres, n_ctx_blocks):
    (x_ref, xp_ref, xn_ref, mod_ref, g_ref, mu_ref, wr_ref, wk_ref, wv_ref, wlw_ref, wla_ref,
     wg1_ref, w2_ref, a2_ref, g2_ref, w0_ref, a0_ref, kkp_ref, kap_ref, rk_ref) = refs[:20]
    if has_vres:
        vf_ref, v0_ref, v1_ref, v2_ref = refs[20:24]
        outs = refs[24:]
    else:
        outs = refs[20:]
    r_ref, v_ref, gg_ref, bo_ref, lp_ref, ap_ref, kd_ref, bb_ref = outs

    t = pl.program_id(1)
    n_blocks = pl.num_programs(1)
    is_ctx = t < n_ctx_blocks
    mod = mod_ref[0]
    g = g_ref[...]
    h = _adaln(x_ref[0], g, mod[0:1], mod[1:2])
    hp = _adaln(xp_ref[0], g, mod[0:1], mod[1:2])
    hn = _adaln(xn_ref[0], g, mod[0:1], mod[1:2])
    left, right, up, down = _neighbours(h, hp, hn, is_ctx, t == n_ctx_blocks, t == n_blocks - 1)
    d_model = h.shape[1]
    q = d_model // 4
    lane = lax.broadcasted_iota(I32, (1, d_model), 1)
    q2 = jnp.where(is_ctx, left, up)
    q3 = jnp.where(is_ctx, right, down)
    shifted = jnp.where(lane < q, left, jnp.where(lane < 2 * q, right, jnp.where(lane < 3 * q, q2, q3)))
    xx = shifted - h
    mu = mu_ref[...]

    def mix(j):
        return (h + xx * mu[j:j + 1]).astype(BF16)

    r = _dot(mix(0), wr_ref[...])
    k = _dot(mix(2), wk_ref[...])
    xv = mix(3)
    v = _dot(xv, wv_ref[...])
    if has_vres:
        lora = _dot(_dot(xv, v1_ref[...]).astype(BF16), v2_ref[...])
        v = v + (vf_ref[0] - v) * _sigmoid(v0_ref[...] + lora)
    gate = _dot(_sigmoid(_dot(mix(5), wg1_ref[...])).astype(BF16), g2_ref[...])
    kx = k * kkp_ref[...]
    kk = kx / jnp.maximum(jnp.sqrt(_segment_allsum(kx * kx, RW_HEAD_DIM)), 1e-12)
    r_ref[0] = r
    v_ref[0] = v
    gg_ref[0] = gate
    tw = jnp.tanh(_dot(mix(1), wlw_ref[...])).astype(BF16)
    la = _dot(mix(4), wla_ref[...]).astype(BF16)
    kap = kap_ref[...]
    kd_sum = None
    for d in range(2):
        logw = -_softplus(-(w0_ref[d:d + 1] + _dot(tw, w2_ref[d]))) - 0.5
        decay_rate = jnp.exp(logw)
        lp_ref[d, 0] = _chunk_cumsum(-decay_rate, reverse=d == 1)
        ap_ref[d, 0] = -kk * jnp.exp(decay_rate)
        a = _sigmoid(a0_ref[d:d + 1] + _dot(la, a2_ref[d]))
        kd = k * (1.0 + (a - 1.0) * kap)
        kd_ref[d, 0] = kd
        bb_ref[d, 0] = kk * a
        kd_sum = kd if kd_sum is None else kd_sum + kd
    bo_ref[0] = _segment_allsum(r * rk_ref[...] * kd_sum, RW_HEAD_DIM) * v


def _rwkv_pre(x, mod_l, norm_g, p, vres, v_first, tb, n_ctx_blocks):
    batch, t_len, d_model = x.shape
    n_blocks = t_len // tb
    hb = tb // GRID_W
    n_halo = t_len // GRID_W
    lora = p["w1"].shape[-1]
    assert 2 * lora == LANES and p["g1"].shape[-1] == LANES
    zeros = jnp.zeros((lora, d_model), F32)
    w2p = jnp.stack([jnp.concatenate([p["w2"][0], zeros]), jnp.concatenate([zeros, p["w2"][1]])])
    a2p = jnp.stack([jnp.concatenate([p["a2"][0], zeros]), jnp.concatenate([zeros, p["a2"][1]])])
    has_vres = vres is not None

    tok = pl.BlockSpec((1, tb, d_model), lambda b, t: (b, t, 0))
    tok2 = pl.BlockSpec((2, 1, tb, d_model), lambda b, t: (0, b, t, 0))
    in_specs = [
        tok,
        pl.BlockSpec((1, GRID_W, d_model), lambda b, t: (b, jnp.maximum(t * hb - 1, 0), 0)),
        pl.BlockSpec((1, GRID_W, d_model), lambda b, t: (b, jnp.minimum((t + 1) * hb, n_halo - 1), 0)),
        pl.BlockSpec((1, N_MOD, d_model), lambda b, t: (jnp.where(t < n_ctx_blocks, batch, b), 0, 0)),
    ]
    args = [x, x, x, mod_l]
    consts = [
        norm_g.reshape(1, d_model), p["mu"],
        p["wr"].astype(BF16), p["wk"].astype(BF16), p["wv"].astype(BF16),
        jnp.concatenate([p["w1"][0], p["w1"][1]], axis=1).astype(BF16),
        jnp.concatenate([p["a1"][0], p["a1"][1]], axis=1).astype(BF16),
        p["g1"].astype(BF16), w2p.astype(BF16), a2p.astype(BF16), p["g2"].astype(BF16),
        p["w0"], p["a0"], p["kk"].reshape(1, d_model), p["ka"].reshape(1, d_model),
        p["rk"].reshape(1, d_model),
    ]
    in_specs += [_const_spec(c.shape) for c in consts]
    args += consts
    if has_vres:
        v0, v1, v2 = vres
        extra = [v0.reshape(1, d_model), v1.astype(BF16), v2.astype(BF16)]
        in_specs += [tok] + [_const_spec(c.shape) for c in extra]
        args += [v_first] + extra
    one = jax.ShapeDtypeStruct((batch, t_len, d_model), F32)
    two = jax.ShapeDtypeStruct((2, batch, t_len, d_model), F32)
    return pl.pallas_call(
        functools.partial(_rwkv_pre_kernel, has_vres=has_vres, n_ctx_blocks=n_ctx_blocks),
        out_shape=[one, one, one, one, two, two, two, two],
        grid=(batch, n_blocks),
        in_specs=in_specs,
        out_specs=[tok, tok, tok, tok, tok2, tok2, tok2, tok2],
        compiler_params=_params(("parallel", "parallel")),
        name="rwkv_features",
    )(*args)


def _rwkv_scan_kernel(r_ref, v_ref, lp_ref, ap_ref, kd_ref, bb_ref, y_ref, s_ref, *, n_pairs, tb):
    d = pl.program_id(0)
    c = pl.program_id(3)
    ch = RW_CHUNK
    two = 2 * ch

    @pl.when(c == 0)
    def _():
        s_ref[...] = jnp.zeros_like(s_ref)

    sgn = jnp.where(d == 0, 1, -1)
    r2 = lax.broadcasted_iota(I32, (two, two), 0)
    c2 = lax.broadcasted_iota(I32, (two, two), 1)
    same = jnp.right_shift(r2, 6) == jnp.right_shift(c2, 6)
    delta = (jnp.bitwise_and(c2, ch - 1) - jnp.bitwise_and(r2, ch - 1)) * sgn
    before = jnp.logical_and(same, delta < 0)
    before_eq = jnp.logical_and(same, delta <= 0)
    head0 = lax.broadcasted_iota(I32, (ch, LANES), 1) < RW_HEAD_DIM
    head0w = lax.broadcasted_iota(I32, (ch, 2 * LANES), 1)
    head0w = jnp.bitwise_and(head0w, LANES - 1) < RW_HEAD_DIM
    n_sub = tb // ch
    pairs = range(n_pairs)

    def stack2(x):
        return jnp.concatenate([x, x], axis=0)

    def pick(x, mask):
        return jnp.where(mask, x[:ch], x[ch:])

    def sub_chunk(j, carry):
        jj = jnp.where(d == 0, j, n_sub - 1 - j)
        start = pl.multiple_of(jj * ch, ch)
        rows = pl.ds(start, ch)
        last = pl.ds(pl.multiple_of(start + jnp.where(d == 0, ch - 8, 0), 8), 8)
        cols = [slice(p * LANES, (p + 1) * LANES) for p in pairs]

        v, vv, r_t, a_t, b_p, k_p, m_b, m_k, p_end = [], [], [], [], [], [], [], [], []
        for p in pairs:
            lp = lp_ref[0, 0, rows, cols[p]]
            e_in = jnp.exp(lp)
            e_out = jnp.exp(-lp)
            lp_edge = lp_ref[0, 0, last, cols[p]]
            p_last = jnp.exp(jnp.where(d == 0, lp_edge[7:8], lp_edge[0:1]))
            p_end.append(p_last)
            at = ap_ref[0, 0, rows, cols[p]] * e_in
            rt = r_ref[0, rows, cols[p]] * e_in
            bt = bb_ref[0, 0, rows, cols[p]] * e_out
            kt = kd_ref[0, 0, rows, cols[p]] * e_out
            vp = v_ref[0, rows, cols[p]]
            lhs = jnp.concatenate([jnp.where(head0, at, 0.0), jnp.where(head0, 0.0, at),
                                   jnp.where(head0, rt, 0.0), jnp.where(head0, 0.0, rt)],
                                  axis=0).astype(BF16)
            m_bk = _dot_nt(lhs, jnp.concatenate([bt, bt, kt, kt], axis=0).astype(BF16))
            m_b.append(m_bk[:, :two])
            m_k.append(m_bk[:, two:])
            v.append(vp)
            vv.append(stack2(vp).astype(BF16))
            r_t.append(rt)
            a_t.append(at)
            b_p.append(bt * p_last)
            k_p.append(kt * p_last)
        n_pow = [jnp.where(before, m_b[p][:two], 0.0).astype(BF16) for p in pairs]
        a_k = [jnp.where(before, m_k[p][:two], 0.0).astype(BF16) for p in pairs]
        r_b = [jnp.where(before_eq, m_b[p][two:], 0.0).astype(BF16) for p in pairs]
        r_k = [jnp.where(before_eq, m_k[p][two:], 0.0).astype(BF16) for p in pairs]
        z = [jnp.concatenate([stack2(a_t[p]), _dot(a_k[p], vv[p])], axis=1) for p in pairs]
        z = [z[p] + _dot(n_pow[p], z[p].astype(BF16)) for p in pairs]
        for _ in range(int(math.log2(ch)) - 1):
            n_pow = [_dot(n_pow[p], n_pow[p]).astype(BF16) for p in pairs]
            z = [z[p] + _dot(n_pow[p], z[p].astype(BF16)) for p in pairs]
        q = [_dot(r_b[p], z[p].astype(BF16)) for p in pairs]
        y_k = [_dot(r_k[p], vv[p]) for p in pairs]
        w_u = [pick(z[p], head0w) for p in pairs]
        r_y = [pick(q[p], head0w) for p in pairs]
        lhs_s = [jnp.concatenate([w_u[p][:, :LANES], r_t[p] + r_y[p][:, :LANES]], axis=0).astype(BF16)
                 for p in pairs]
        u_0 = [w_u[p][:, LANES:] for p in pairs]
        y_0 = [r_y[p][:, LANES:] + pick(y_k[p], head0) for p in pairs]
        bk_p = [jnp.concatenate([b_p[p], k_p[p]], axis=0).astype(BF16) for p in pairs]

        s_old = [s_ref[p] for p in pairs]
        g = [_dot_nt(lhs_s[p], s_old[p].astype(BF16)) for p in pairs]
        u = [g[p][:ch] + u_0[p] for p in pairs]
        for p in pairs:
            y_ref[0, 0, rows, cols[p]] = (g[p][ch:] + y_0[p]).astype(y_ref.dtype)
        d_s = [_dot_tn(jnp.concatenate([u[p], v[p]], axis=0).astype(BF16), bk_p[p]) for p in pairs]
        for p in pairs:
            s_ref[p] = s_old[p] * p_end[p] + jnp.where(same, d_s[p], 0.0)
        return carry

    lax.fori_loop(0, n_sub, sub_chunk, 0)


def _rwkv_scan(r, v, lp, ap, kd, bb, tb, n_ctx_blocks):
    batch, t_len, d_model = r.shape
    n_blocks = t_len // tb
    lanes = min(8 * LANES, d_model)
    n_pairs = lanes // LANES

    def one_map(d, b, g, c):
        return (b, _time_block_order(d, c, n_blocks, n_ctx_blocks), g)

    def two_map(d, b, g, c):
        return (d, b, _time_block_order(d, c, n_blocks, n_ctx_blocks), g)

    one = pl.BlockSpec((1, tb, lanes), one_map)
    two = pl.BlockSpec((1, 1, tb, lanes), two_map)
    return pl.pallas_call(
        functools.partial(_rwkv_scan_kernel, n_pairs=n_pairs, tb=tb),
        out_shape=jax.ShapeDtypeStruct((2, batch, t_len, d_model), BF16),
        grid=(2, batch, d_model // lanes, n_blocks),
        in_specs=[one, one, two, two, two, two],
        out_specs=two,
        scratch_shapes=[pltpu.VMEM((n_pairs, LANES, LANES), F32)],
        compiler_params=_params(("parallel", "parallel", "parallel", "arbitrary")),
        name="rwkv_scan",
    )(r, v, lp, ap, kd, bb)


def _rwkv_out_kernel(y_ref, bo_ref, gg_ref, x_ref, mod_ref, lg_ref, lb_ref,
                     wo_ref, g2_ref, wr_ref, br_ref,
                     xo_ref, h2_ref, ti_ref, gt_ref):
    y = y_ref[0, 0].astype(F32) + y_ref[1, 0].astype(F32)
    inv_n = 1.0 / RW_HEAD_DIM
    yc = y - _segment_allsum(y, RW_HEAD_DIM) * inv_n
    var = _segment_allsum(yc * yc, RW_HEAD_DIM) * inv_n
    yn = yc * lax.rsqrt(var + RW_GN_EPS) * lg_ref[...] + lb_ref[...]
    out = ((yn + bo_ref[0]) * gg_ref[0]).astype(BF16)
    _post_mixer(x_ref[0], _dot(out, wo_ref[...]), mod_ref[0], g2_ref[...], wr_ref[...], br_ref[...],
                xo_ref, h2_ref, ti_ref, gt_ref)


def _post_specs(batch, t_len, d_model, tb):
    tok = pl.BlockSpec((1, tb, d_model), lambda b, t: (b, t, 0))
    nar = pl.BlockSpec((1, tb, TOP_K), lambda b, t: (b, t, 0))
    shapes = [jax.ShapeDtypeStruct((batch, t_len, d_model), F32),
              jax.ShapeDtypeStruct((batch, t_len, d_model), F32),
              jax.ShapeDtypeStruct((batch, t_len, TOP_K), I32),
              jax.ShapeDtypeStruct((batch, t_len, TOP_K), F32)]
    return shapes, [tok, tok, nar, nar]


def _rwkv_out(y, bonus, gate, x, mod_l, p, norm2_g, moe_wr, moe_br, tb, n_ctx_blocks):
    batch, t_len, d_model = x.shape
    tok = pl.BlockSpec((1, tb, d_model), lambda b, t: (b, t, 0))
    tok2 = pl.BlockSpec((2, 1, tb, d_model), lambda b, t: (0, b, t, 0))
    mod_spec = pl.BlockSpec((1, N_MOD, d_model), lambda b, t: (jnp.where(t < n_ctx_blocks, batch, b), 0, 0))
    consts = [p["lnx_g"].reshape(1, d_model), p["lnx_b"].reshape(1, d_model), p["wo"].astype(BF16), norm2_g.reshape(1, d_model), *_router_params(moe_wr, moe_br)]
    shapes, specs = _post_specs(batch, t_len, d_model, tb)
    return pl.pallas_call(
        _rwkv_out_kernel,
        out_shape=shapes,
        grid=(batch, t_len // tb),
        in_specs=[tok2, tok, tok, tok, mod_spec] + [_const_spec(c.shape) for c in consts],
        out_specs=specs,
        compiler_params=_params(("parallel", "parallel")),
        name="rwkv_readout",
    )(y, bonus, gate, x, mod_l, *consts)


def _ml_up_kernel(x_ref, mod_ref, g_ref, w_ref, xm_ref, z_ref):
    mod = mod_ref[0]
    h = _adaln(x_ref[0], g_ref[...], mod[0:1], mod[1:2]).astype(BF16)
    up = _dot(h, w_ref[...])
    inner = xm_ref.shape[-1]
    xm_ref[0] = up[:, :inner]
    z_ref[0] = up[:, inner:].astype(z_ref.dtype)


def _ml_up(x, mod_l, norm_g, w_up, tb, n_ctx_blocks):
    batch, t_len, d_model = x.shape
    inner = w_up.shape[1] // 2
    tok = pl.BlockSpec((1, tb, d_model), lambda b, t: (b, t, 0))
    wide = pl.BlockSpec((1, tb, inner), lambda b, t: (b, t, 0))
    mod_spec = pl.BlockSpec((1, N_MOD, d_model), lambda b, t: (jnp.where(t < n_ctx_blocks, batch, b), 0, 0))
    return pl.pallas_call(
        _ml_up_kernel,
        out_shape=[jax.ShapeDtypeStruct((batch, t_len, inner), F32),
                   jax.ShapeDtypeStruct((batch, t_len, inner), BF16)],
        grid=(batch, t_len // tb),
        in_specs=[tok, mod_spec, _const_spec((1, d_model)), _const_spec(w_up.shape)],
        out_specs=[wide, wide],
        compiler_params=_params(("parallel", "parallel")),
        name="mlstm_up",
    )(x, mod_l, norm_g.reshape(1, d_model), w_up.astype(BF16))


def _ml_feat_kernel(xm_ref, xp_ref, xn_ref, cw_ref, cb_ref, wq_ref, wk_ref, wv_ref, wg_ref, bg_ref,
                    q_ref, k_ref, v_ref, xc_ref, gt_ref, gs_ref, gtt_ref, gst_ref, *, n_ctx_blocks):
    t = pl.program_id(1)
    n_blocks = pl.num_programs(1)
    is_ctx = t < n_ctx_blocks
    xm = xm_ref[0]
    tb, inner = xm.shape
    row = lax.broadcasted_iota(I32, (tb, 1), 0)
    col = jnp.bitwise_and(row, GRID_W - 1)
    left_ok = jnp.where(is_ctx, row, col) != 0
    right_ok = jnp.where(is_ctx, row - (tb - 1), col - (GRID_W - 1)) != 0
    up = jnp.concatenate([xp_ref[0], xm[:tb - GRID_W]], axis=0)
    down = jnp.concatenate([xm[GRID_W:], xn_ref[0]], axis=0)
    up_ok = jnp.where(is_ctx, -1, jnp.where(t == n_ctx_blocks, row, tb)) >= GRID_W
    down_ok = jnp.where(is_ctx, tb, jnp.where(t == n_blocks - 1, row, 0)) < tb - GRID_W
    up = jnp.where(up_ok, up, 0.0)
    down = jnp.where(down_ok, down, 0.0)
    cw = cw_ref[...]
    acc = jnp.zeros((tb, inner), F32) + cb_ref[...]
    for dy, src in enumerate((up, xm, down)):
        acc = acc + jnp.where(left_ok, pltpu.roll(src, 1, axis=0), 0.0) * cw[3 * dy:3 * dy + 1]
        acc = acc + src * cw[3 * dy + 1:3 * dy + 2]
        acc = acc + jnp.where(right_ok, pltpu.roll(src, tb - 1, axis=0), 0.0) * cw[3 * dy + 2:3 * dy + 3]
    xc = _silu(acc)
    xc_ref[0] = xc.astype(xc_ref.dtype)
    xcb = xc.astype(BF16)
    xmb = xm.astype(BF16)
    for gidx in range(inner // LANES):
        cols = slice(gidx * LANES, (gidx + 1) * LANES)
        q_ref[0, :, cols] = _dot(xcb[:, cols], wq_ref[gidx]).astype(BF16)
        k_ref[0, :, cols] = _dot(xcb[:, cols], wk_ref[gidx]).astype(BF16)
        v_ref[0, :, cols] = _dot(xmb[:, cols], wv_ref[gidx]).astype(BF16)
    pre = (_dot(q_ref[0], wg_ref[0]) + _dot(k_ref[0], wg_ref[1]) + _dot(v_ref[0], wg_ref[2]) + bg_ref[...])
    lane = lax.broadcasted_iota(I32, pre.shape, 1)
    gates = jnp.where(lane < 2 * ML_HEADS, pre, -_softplus(-pre))
    gt_ref[0] = gates
    fwd = bwd = gates
    step = 1
    while step < tb:
        fwd = fwd + jnp.where(row >= step, pltpu.roll(fwd, step, axis=0), 0.0)
        bwd = bwd + jnp.where(row < tb - step, pltpu.roll(bwd, tb - step, axis=0), 0.0)
        step *= 2
    backward_lane = jnp.bitwise_and(jnp.right_shift(lane, 2), 1) == 1
    sums = jnp.where(backward_lane, bwd, fwd)
    gs_ref[0] = sums
    n_gate = gtt_ref.shape[1]
    gtt_ref[0] = jnp.transpose(gates)[:n_gate]
    gst_ref[0] = jnp.transpose(sums)[:n_gate]


def _block_diag_lanes(w):
    n_blk, blk, _ = w.shape
    per = LANES // blk
    w = w.reshape(n_blk // per, per, blk, blk)
    eye = jnp.eye(per, dtype=w.dtype)
    return jnp.einsum("gaio,ab->gaibo", w, eye).reshape(n_blk // per, LANES, LANES)


def _ml_feat(xm, p, tb, n_ctx_blocks):
    batch, t_len, inner = xm.shape
    hb = tb // GRID_W
    n_halo = t_len // GRID_W
    wide = pl.BlockSpec((1, tb, inner), lambda b, t: (b, t, 0))
    gate_w = jnp.concatenate([p["wi"][0], p["wi"][1], p["wf"][0], p["wf"][1]], axis=1)
    n_gate = gate_w.shape[1]
    gate_w = jnp.pad(gate_w, ((0, 0), (0, LANES - n_gate))).reshape(3, inner, LANES).astype(BF16)
    gate_b = jnp.pad(jnp.concatenate([p["bi"][0], p["bi"][1], p["bf"][0], p["bf"][1]]),
                     (0, LANES - n_gate)).reshape(1, LANES)
    consts = [p["conv_w"].reshape(9, inner), p["conv_b"].reshape(1, inner),
              _block_diag_lanes(p["wq"]).astype(BF16), _block_diag_lanes(p["wk"]).astype(BF16),
              _block_diag_lanes(p["wv"]).astype(BF16), gate_w, gate_b]
    half = jax.ShapeDtypeStruct((batch, t_len, inner), BF16)
    narrow = jax.ShapeDtypeStruct((batch, t_len, LANES), F32)
    narrow_spec = pl.BlockSpec((1, tb, LANES), lambda b, t: (b, t, 0))
    flat = jax.ShapeDtypeStruct((batch, n_gate, t_len), F32)
    flat_spec = pl.BlockSpec((1, n_gate, tb), lambda b, t: (b, 0, t))
    return pl.pallas_call(
        functools.partial(_ml_feat_kernel, n_ctx_blocks=n_ctx_blocks),
        out_shape=[half, half, half, half, narrow, narrow,
                   flat, flat],
        grid=(batch, t_len // tb),
        in_specs=[wide,
                  pl.BlockSpec((1, GRID_W, inner), lambda b, t: (b, jnp.maximum(t * hb - 1, 0), 0)),
                  pl.BlockSpec((1, GRID_W, inner), lambda b, t: (b, jnp.minimum((t + 1) * hb, n_halo - 1), 0)),
                  ] + [_const_spec(c.shape) for c in consts],
        out_specs=[wide, wide, wide, wide, narrow_spec, narrow_spec, flat_spec, flat_spec],
        compiler_params=_params(("parallel", "parallel")),
        name="mlstm_features",
    )(xm, xm, xm, *consts)


def _ml_scan_kernel(q_ref, k_ref, v_ref, gc_ref, sc_ref, gr_ref, sr_ref, h_ref, c_ref, n_ref, m_ref, *, scale):
    d = pl.program_id(0)
    hd = pl.program_id(2)
    c = pl.program_id(3)

    @pl.when(c == 0)
    def _():
        c_ref[...] = jnp.zeros_like(c_ref)
        n_ref[...] = jnp.zeros_like(n_ref)
        m_ref[...] = jnp.zeros_like(m_ref)

    q = q_ref[0]
    k = k_ref[0]
    v = v_ref[0]
    ln = q.shape[0]
    sgn = jnp.where(d == 0, 1, -1)
    ri = lax.broadcasted_iota(I32, (ln, ln), 0)
    ci = lax.broadcasted_iota(I32, (ln, ln), 1)
    allowed = (ci - ri) * sgn <= 0
    gcol = gc_ref[0]
    grow = gr_ref[0]
    i_idx = d * ML_HEADS + hd
    f_idx = 2 * ML_HEADS + i_idx
    lane_c = lax.broadcasted_iota(I32, gcol.shape, 1)
    sub_r = lax.broadcasted_iota(I32, grow.shape, 0)
    b_col = jnp.sum(jnp.where(lane_c == f_idx, sc_ref[0], 0.0), axis=1, keepdims=True)
    f_col = jnp.sum(jnp.where(lane_c == f_idx, gcol, 0.0), axis=1, keepdims=True)
    i_col = jnp.sum(jnp.where(lane_c == i_idx, gcol, 0.0), axis=1, keepdims=True)
    b_row = jnp.sum(jnp.where(sub_r == f_idx, sr_ref[0], 0.0), axis=0, keepdims=True)
    i_row = jnp.sum(jnp.where(sub_r == i_idx, grow, 0.0), axis=0, keepdims=True)
    m_prev = m_ref[0:1, 0:1]
    log_inter = b_col + m_prev
    log_intra = jnp.where(allowed, b_col - b_row + i_row, -jnp.inf)
    m_t = jnp.maximum(log_inter, jnp.max(log_intra, axis=1, keepdims=True))
    w_intra = jnp.exp(log_intra - m_t)
    w_inter = jnp.exp(log_inter - m_t)
    s = _dot_nt(q, k) * scale * w_intra
    ct = c_ref[...]
    nrow = n_ref[0:1, :]
    qf = q.astype(F32)
    num = _dot(s.astype(BF16), v) + w_inter * _dot(q, ct.astype(BF16))
    den = jnp.sum(s, axis=1, keepdims=True) + w_inter * jnp.sum(qf * nrow, axis=1, keepdims=True)
    h_ref[0, 0] = (num / jnp.maximum(jnp.abs(den), jnp.exp(-m_t))).astype(h_ref.dtype)
    gsum = jnp.sum(f_col, axis=0, keepdims=True)
    log_loc = gsum - b_col + i_col
    m_new = jnp.maximum(gsum + m_prev, jnp.max(log_loc, axis=0, keepdims=True))
    w_loc = jnp.exp(log_loc - m_new)
    w_old = jnp.exp(gsum + m_prev - m_new)
    kw = k.astype(F32) * (scale * w_loc)
    c_ref[...] = w_old * ct + _dot_tn(kw.astype(BF16), v)
    n_ref[0:1, :] = w_old * nrow + jnp.sum(kw, axis=0, keepdims=True)
    m_ref[...] = jnp.zeros_like(m_ref) + m_new


def _ml_scan(q, k, v, gates, gate_sums, gates_t, sums_t, tb, n_ctx_blocks):
    batch, t_len, inner = q.shape
    dh = inner // ML_HEADS
    n_blocks = t_len // tb
    n_gate = gates_t.shape[1]

    def head_map(d, b, h, c):
        return (b, _time_block_order(d, c, n_blocks, n_ctx_blocks), h)

    def col_map(d, b, h, c):
        return (b, _time_block_order(d, c, n_blocks, n_ctx_blocks), 0)

    def row_map(d, b, h, c):
        return (b, 0, _time_block_order(d, c, n_blocks, n_ctx_blocks))

    def out_map(d, b, h, c):
        return (d, b, _time_block_order(d, c, n_blocks, n_ctx_blocks), h)

    head = pl.BlockSpec((1, tb, dh), head_map)
    col = pl.BlockSpec((1, tb, LANES), col_map)
    row = pl.BlockSpec((1, n_gate, tb), row_map)
    return pl.pallas_call(
        functools.partial(_ml_scan_kernel, scale=float(dh) ** -0.5),
        out_shape=jax.ShapeDtypeStruct((2, batch, t_len, inner), BF16),
        grid=(2, batch, ML_HEADS, n_blocks),
        in_specs=[head, head, head, col, col, row, row],
        out_specs=pl.BlockSpec((1, 1, tb, dh), out_map),
        scratch_shapes=[pltpu.VMEM((dh, dh), F32), pltpu.VMEM((8, dh), F32), pltpu.VMEM((8, LANES), F32)],
        compiler_params=_params(("parallel", "parallel", "parallel", "arbitrary")),
        name="mlstm_scan",
    )(q, k, v, gates, gate_sums, gates_t, sums_t)


def _ml_out_kernel(hs_ref, xc_ref, z_ref, x_ref, mod_ref, ng_ref, sk_ref, wd_ref, g2_ref, wr_ref, br_ref,
                   xo_ref, h2_ref, ti_ref, gt_ref):
    hs = hs_ref[0, 0].astype(F32) + hs_ref[1, 0].astype(F32)
    inner = hs.shape[1]
    dh = inner // ML_HEADS
    parts = []
    for hd in range(ML_HEADS):
        blk = hs[:, hd * dh:(hd + 1) * dh]
        mean = jnp.mean(blk, axis=1, keepdims=True)
        cen = blk - mean
        var = jnp.mean(cen * cen, axis=1, keepdims=True)
        parts.append(cen * lax.rsqrt(var + ML_LN_EPS))
    hn = jnp.concatenate(parts, axis=1) * ng_ref[...]
    y = ((hn + sk_ref[...] * xc_ref[0].astype(F32)) * _silu(z_ref[0].astype(F32))).astype(BF16)
    _post_mixer(x_ref[0], _dot(y, wd_ref[...]), mod_ref[0], g2_ref[...], wr_ref[...], br_ref[...],
                xo_ref, h2_ref, ti_ref, gt_ref)


def _ml_out(hs, xc, z, x, mod_l, p, norm2_g, moe_wr, moe_br, tb, n_ctx_blocks):
    batch, t_len, d_model = x.shape
    inner = xc.shape[-1]
    tok = pl.BlockSpec((1, tb, d_model), lambda b, t: (b, t, 0))
    wide = pl.BlockSpec((1, tb, inner), lambda b, t: (b, t, 0))
    wide2 = pl.BlockSpec((2, 1, tb, inner), lambda b, t: (0, b, t, 0))
    mod_spec = pl.BlockSpec((1, N_MOD, d_model), lambda b, t: (jnp.where(t < n_ctx_blocks, batch, b), 0, 0))
    consts = [p["norm_g"].reshape(1, inner), p["skip"].reshape(1, inner), p["w_down"].astype(BF16),
              norm2_g.reshape(1, d_model), *_router_params(moe_wr, moe_br)]
    shapes, specs = _post_specs(batch, t_len, d_model, tb)
    return pl.pallas_call(
        _ml_out_kernel,
        out_shape=shapes,
        grid=(batch, t_len // tb),
        in_specs=[wide2, wide, wide, tok, mod_spec] + [_const_spec(c.shape) for c in consts],
        out_specs=specs,
        compiler_params=_params(("parallel", "parallel")),
        name="mlstm_readout",
    )(hs, xc, z, x, mod_l, *consts)


def _moe_plan(top_i, tok_rows, n_rows, n_exp, bm):
    n_asg = top_i.size
    assert n_rows < (1 << 16) and MOE_OUT_BUFFERS * bm <= n_rows
    flat_e = top_i.reshape(n_asg)
    onehot = (flat_e[:, None] == jnp.arange(n_exp, dtype=I32)[None, :]).astype(I32)
    csum = jnp.cumsum(onehot, axis=0)
    counts = csum[-1]
    padded = (counts + bm - 1) // bm * bm
    pad_end = jnp.cumsum(padded)
    pad_start = pad_end - padded
    dest = jnp.sum(onehot * (csum - 1 + pad_start[None, :]), axis=1)
    n_blk = n_asg // bm + n_exp
    choice = jnp.arange(TOP_K, dtype=I32)
    real = jnp.bitwise_or(tok_rows[:, None], jnp.left_shift(choice, 16)[None, :]).reshape(n_asg)
    slots = jnp.arange((n_blk + MOE_LEAD_BLOCKS + 1) * bm, dtype=I32)
    buf = (slots // bm - MOE_LEAD_BLOCKS) % MOE_OUT_BUFFERS
    pad = jnp.bitwise_or(buf * bm + slots % bm, TOP_K << 16)
    table = pad.at[MOE_LEAD_BLOCKS * bm + dest].set(real)
    starts = jnp.arange(n_blk, dtype=I32) * bm
    block_e = jnp.minimum(jnp.sum((pad_end[None, :] <= starts[:, None]).astype(I32), axis=1), n_exp - 1)
    return block_e, table


def _moe_expert_kernel(be_ref, src_0, src_p1, dst_m2, dst_m1, dst_0, h_hbm, wgu_ref, bgu_ref, wd_ref, bd_ref,
                       y_hbm, xbuf, obuf, wgu_bf, wd_bf, gsem, ssem):
    i = pl.program_id(0)
    n_blk = pl.num_programs(0)
    n_groups = xbuf.shape[1]
    x_cur = lax.rem(i, 2)
    x_next = 1 - x_cur
    o_cur = lax.rem(i, MOE_OUT_BUFFERS)
    o_prev = lax.rem(i + MOE_OUT_BUFFERS - 1, MOE_OUT_BUFFERS)
    o_prev2 = lax.rem(i + MOE_OUT_BUFFERS - 2, MOE_OUT_BUFFERS)
    src_tables = {0: src_0, 1: src_p1}
    dst_tables = {-2: dst_m2, -1: dst_m1, 0: dst_0}

    def gather_copy(rel, buf, g, k):
        row = src_tables[rel][0, 0, g * SUBLANES + k]
        src = h_hbm.at[jnp.right_shift(row, SUBLANE_SHIFT), pl.ds(jnp.bitwise_and(row, SUBLANES - 1), 1)]
        return pltpu.make_async_copy(src, xbuf.at[buf, g, pl.ds(k, 1)], gsem.at[buf])

    def scatter_copy(rel, buf, g, k):
        row = dst_tables[rel][0, 0, g * SUBLANES + k]
        dst = y_hbm.at[jnp.right_shift(row, SUBLANE_SHIFT), pl.ds(jnp.bitwise_and(row, SUBLANES - 1), 1)]
        return pltpu.make_async_copy(obuf.at[buf, g, pl.ds(k, 1)], dst, ssem.at[buf])

    def for_rows(fn):
        def body(g, carry):
            for k in range(SUBLANES):
                fn(g, k)
            return carry
        lax.fori_loop(0, n_groups, body, 0)

    @pl.when(i == 0)
    def _():
        obuf[...] = jnp.zeros_like(obuf)
        for_rows(lambda g, k: gather_copy(0, x_cur, g, k).start())
        for_rows(lambda g, k: scatter_copy(-2, o_prev2, g, k).start())

    changed = jnp.logical_or(i == 0, be_ref[i] != be_ref[jnp.maximum(i - 1, 0)])

    @pl.when(changed)
    def _():
        wgu_bf[...] = wgu_ref[0, 0].astype(BF16)
        wd_bf[...] = wd_ref[0, 0].astype(BF16)

    for_rows(lambda g, k: gather_copy(1, x_next, g, k).start())
    for_rows(lambda g, k: scatter_copy(-1, o_prev, g, k).start())
    for_rows(lambda g, k: scatter_copy(-2, o_prev2, g, k).wait())
    for_rows(lambda g, k: gather_copy(0, x_cur, g, k).wait())

    d_model = xbuf.shape[-1]
    x = xbuf[x_cur].reshape(n_groups * SUBLANES, d_model).astype(BF16)
    gu = _dot(x, wgu_bf[...]) + bgu_ref[0, 0]
    ff = gu.shape[1] // 2
    gate = jnp.minimum(gu[:, :ff], SWIGLU_LIMIT)
    up = jnp.clip(gu[:, ff:], -SWIGLU_LIMIT, SWIGLU_LIMIT)
    act = gate * _sigmoid(SWIGLU_ALPHA * gate) * (up + 1.0)
    out = _dot(act.astype(BF16), wd_bf[...]) + bd_ref[0, 0]
    obuf[o_cur] = out.reshape(n_groups, SUBLANES, d_model)

    @pl.when(i == n_blk - 1)
    def _():
        for_rows(lambda g, k: scatter_copy(0, o_cur, g, k).start())
        for_rows(lambda g, k: scatter_copy(-1, o_prev, g, k).wait())
        for_rows(lambda g, k: scatter_copy(0, o_cur, g, k).wait())
        for_rows(lambda g, k: gather_copy(1, x_next, g, k).wait())


def _moe_experts(h2, top_i, tok_rows, layer, w_gu, b_gu, w_d, b_d):
    n_tok, d_model = h2.shape
    depth, n_exp, _, ff2 = w_gu.shape
    bm = MOE_BLOCK
    ff = ff2 // 2
    n_out = TOP_K * n_tok + MOE_OUT_BUFFERS * bm
    assert top_i.size % bm == 0 and bm % SUBLANES == 0 and n_tok % SUBLANES == 0 and n_out % SUBLANES == 0
    block_e, table = _moe_plan(top_i, tok_rows, n_tok, n_exp, bm)
    n_blk = block_e.shape[0]
    table = table.reshape(n_blk + MOE_LEAD_BLOCKS + 1, 1, bm)
    src_rows = jnp.bitwise_and(table, 0xFFFF)
    dst_rows = jnp.right_shift(table, 16) * n_tok + src_rows

    def table_spec(rel):
        return pl.BlockSpec((1, 1, bm), lambda i, be: (i + MOE_LEAD_BLOCKS + rel, 0, 0),
                            memory_space=pltpu.SMEM)

    grid_spec = pltpu.PrefetchScalarGridSpec(
        num_scalar_prefetch=1,
        grid=(n_blk,),
        in_specs=[
            table_spec(0), table_spec(1), table_spec(-2), table_spec(-1), table_spec(0),
            pl.BlockSpec(memory_space=pl.ANY),
            pl.BlockSpec((1, 1, d_model, ff2), lambda i, be: (layer, be[i], 0, 0)),
            pl.BlockSpec((1, 1, 1, ff2), lambda i, be: (layer, be[i], 0, 0)),
            pl.BlockSpec((1, 1, ff, d_model), lambda i, be: (layer, be[i], 0, 0)),
            pl.BlockSpec((1, 1, 1, d_model), lambda i, be: (layer, be[i], 0, 0)),
        ],
        out_specs=pl.BlockSpec(memory_space=pl.ANY),
        scratch_shapes=[
            pltpu.VMEM((2, bm // SUBLANES, SUBLANES, d_model), F32),
            pltpu.VMEM((MOE_OUT_BUFFERS, bm // SUBLANES, SUBLANES, d_model), F32),
            pltpu.VMEM((d_model, ff2), BF16),
            pltpu.VMEM((ff, d_model), BF16),
            pltpu.SemaphoreType.DMA((2,)),
            pltpu.SemaphoreType.DMA((MOE_OUT_BUFFERS,)),
        ],
    )
    y4 = pl.pallas_call(
        _moe_expert_kernel,
        out_shape=jax.ShapeDtypeStruct((n_out // SUBLANES, SUBLANES, d_model), F32),
        grid_spec=grid_spec,
        compiler_params=_params(("arbitrary",)),
        name="moe_experts",
    )(block_e, src_rows, src_rows, dst_rows, dst_rows, dst_rows,
      h2.reshape(n_tok // SUBLANES, SUBLANES, d_model), w_gu, b_gu.reshape(depth, n_exp, 1, ff2), w_d,
      b_d.reshape(depth, n_exp, 1, d_model))
    return y4.reshape(n_out, d_model)


def _moe_combine_kernel(x_ref, gt_ref, mod_ref, fg_ref, *rest, final):
    y_refs, o_ref = rest[:TOP_K], rest[TOP_K]
    gates = gt_ref[0]
    f = jnp.zeros(x_ref.shape[1:], F32)
    for j in range(TOP_K):
        f = f + gates[:, j:j + 1] * y_refs[j][...]
    xn = x_ref[0] + mod_ref[0][5:6] * f
    if final:
        xn = xn * lax.rsqrt(jnp.mean(xn * xn, axis=-1, keepdims=True) + NORM_EPS) * fg_ref[...]
    o_ref[0] = xn


def _moe_combine(x, y4, gates, mod_l, final_g, tb, n_ctx_blocks, final):
    batch, t_len, d_model = x.shape
    off = n_ctx_blocks if final else 0
    t_out = t_len - off * tb
    per_b = t_len // tb
    per_j = batch * per_b
    mod_spec = pl.BlockSpec((1, N_MOD, d_model),
                            lambda b, t: (jnp.where(t + off < n_ctx_blocks, batch, b), 0, 0))

    def choice_spec(j):
        return pl.BlockSpec((tb, d_model), lambda b, t: (j * per_j + b * per_b + t + off, 0))

    return pl.pallas_call(
        functools.partial(_moe_combine_kernel, final=final),
        out_shape=jax.ShapeDtypeStruct((batch, t_out, d_model), F32),
        grid=(batch, t_out // tb),
        in_specs=[pl.BlockSpec((1, tb, d_model), lambda b, t: (b, t + off, 0)),
                  pl.BlockSpec((1, tb, TOP_K), lambda b, t: (b, t + off, 0)),
                  mod_spec, _const_spec((1, d_model))] + [choice_spec(j) for j in range(TOP_K)],
        out_specs=pl.BlockSpec((1, tb, d_model), lambda b, t: (b, t, 0)),
        compiler_params=_params(("parallel", "parallel")),
        name="moe_combine",
    )(x, gates, mod_l, final_g.reshape(1, d_model), *([y4] * TOP_K))


def _moe_layer(x, h2, top_i, gates, mod_l, layer, moe_p, final_g, tb, n_ctx_blocks, last):
    batch, t_len, d_model = x.shape
    tok_rows = jnp.arange(batch * t_len, dtype=I32).reshape(batch, t_len)
    if last:
        n_ctx = n_ctx_blocks * tb
        top_i, tok_rows = top_i[:, n_ctx:], tok_rows[:, n_ctx:]
    y4 = _moe_experts(h2.reshape(batch * t_len, d_model), top_i.reshape(-1, TOP_K), tok_rows.reshape(-1),
                      layer, *moe_p)
    return _moe_combine(x, y4, gates, mod_l, final_g, tb, n_ctx_blocks, last)


def kernel(x, c, ctx, c_ctx, ada_w, ada_b, norm1_g, norm2_g, final_g, rw_mu, rw_wr, rw_wk, rw_wv, rw_wo, rw_w0, rw_w1, rw_w2, rw_a0, rw_a1, rw_a2, rw_g1, rw_g2, rw_kk, rw_ka, rw_rk, rw_lnx_g, rw_lnx_b, rw_v0, rw_v1, rw_v2, ml_w_up, ml_conv_w, ml_conv_b, ml_wq, ml_wk, ml_wv, ml_wi, ml_bi, ml_wf, ml_bf, ml_norm_g, ml_skip, ml_w_down, moe_wr, moe_br, moe_w_gu, moe_b_gu, moe_w_d, moe_b_d):
    batch, seq, d_model = x.shape
    n_ctx = ctx.shape[1]
    depth = ada_w.shape[0]
    tb = math.gcd(MAX_TIME_BLOCK, n_ctx, seq)
    assert tb % GRID_W == 0 and n_ctx == tb and d_model % (2 * RW_HEAD_DIM) == 0
    n_ctx_blocks = n_ctx // tb

    mod = _modulation(c, c_ctx, ada_w, ada_b)
    xs = jnp.concatenate([ctx, x], axis=1)
    v_first = None
    for i in range(depth):
        last = i == depth - 1
        j = i // 2
        mod_l = mod[i]
        if i % 2 == 0:
            p = dict(mu=rw_mu[j], wr=rw_wr[j], wk=rw_wk[j], wv=rw_wv[j], wo=rw_wo[j], w0=rw_w0[j],
                     w1=rw_w1[j], w2=rw_w2[j], a0=rw_a0[j], a1=rw_a1[j], a2=rw_a2[j], g1=rw_g1[j],
                     g2=rw_g2[j], kk=rw_kk[j], ka=rw_ka[j], rk=rw_rk[j], lnx_g=rw_lnx_g[j],
                     lnx_b=rw_lnx_b[j])
            vres = None if j == 0 else (rw_v0[j - 1], rw_v1[j - 1], rw_v2[j - 1])
            r, v, gate, bonus, lp, ap, kd, bb = _rwkv_pre(xs, mod_l, norm1_g[i], p, vres, v_first,
                                                          tb, n_ctx_blocks)
            if j == 0:
                v_first = v
            y = _rwkv_scan(r, v, lp, ap, kd, bb, tb, n_ctx_blocks)
            xs, h2, top_i, gates = _rwkv_out(y, bonus, gate, xs, mod_l, p, norm2_g[i],
                                             moe_wr[i], moe_br[i], tb, n_ctx_blocks)
        else:
            p = dict(conv_w=ml_conv_w[j], conv_b=ml_conv_b[j], wq=ml_wq[j], wk=ml_wk[j], wv=ml_wv[j],
                     wi=ml_wi[j], bi=ml_bi[j], wf=ml_wf[j], bf=ml_bf[j], norm_g=ml_norm_g[j],
                     skip=ml_skip[j], w_down=ml_w_down[j])
            xm, z = _ml_up(xs, mod_l, norm1_g[i], ml_w_up[j], tb, n_ctx_blocks)
            q, k, v, xc, gts, gsums, gts_t, gsums_t = _ml_feat(xm, p, tb, n_ctx_blocks)
            hs = _ml_scan(q, k, v, gts, gsums, gts_t, gsums_t, tb, n_ctx_blocks)
            xs, h2, top_i, gates = _ml_out(hs, xc, z, xs, mod_l, p, norm2_g[i],
                                           moe_wr[i], moe_br[i], tb, n_ctx_blocks)
        moe_p = (moe_w_gu, moe_b_gu, moe_w_d, moe_b_d)
        xs = _moe_layer(xs, h2, top_i, gates, mod_l, i, moe_p, final_g, tb, n_ctx_blocks, last)
    return xs
```

```python
import functools
import math

import jax
import jax.numpy as jnp
from jax import lax
from jax.experimental import pallas as pl
from jax.experimental.pallas import tpu as pltpu

F32 = jnp.float32
BF16 = jnp.bfloat16
I32 = jnp.int32
HI = lax.Precision.HIGHEST

GRID_W = 64
N_MOD = 6
NORM_EPS = 1e-6
RW_HEAD_DIM = 64
RW_GN_EPS = 64e-5
RW_CHUNK = 64
ML_HEADS = 4
ML_QKV_BLOCK = 4
ML_LN_EPS = 1e-6
TOP_K = 4
SWIGLU_LIMIT = 7.0
SWIGLU_ALPHA = 1.702
MOE_BLOCK = 256
MOE_OUT_BUFFERS = 3
MOE_LEAD_BLOCKS = 2
SUBLANES = 8
SUBLANE_SHIFT = SUBLANES.bit_length() - 1
LANES = 128
MXU_WIDTH = 256
MAX_TIME_BLOCK = 256
VMEM_LIMIT = 56 * 1024 * 1024


def _dot(a, b, precision=None):
    return jnp.dot(a, b, preferred_element_type=F32, precision=precision)


def _dot_nt(a, b):
    return lax.dot_general(a, b, (((1,), (1,)), ((), ())), preferred_element_type=F32)


def _dot_tn(a, b, precision=None):
    return lax.dot_general(a, b, (((0,), (0,)), ((), ())), preferred_element_type=F32,
                           precision=precision)


def _sigmoid(x):
    return 1.0 / (1.0 + jnp.exp(-x))


def _softplus(x):
    return jnp.maximum(x, 0.0) + jnp.log(1.0 + jnp.exp(-jnp.abs(x)))


def _silu(x):
    return x * _sigmoid(x)


def _adaln(x, g, shift, scale):
    y = x * lax.rsqrt(jnp.mean(x * x, axis=-1, keepdims=True) + NORM_EPS) * g
    return y * (1.0 + scale) + shift


def _const_spec(shape):
    nd = len(shape)
    return pl.BlockSpec(shape, lambda *_: (0,) * nd, pipeline_mode=pl.Buffered(1))


def _params(sem, vmem=VMEM_LIMIT):
    return pltpu.CompilerParams(dimension_semantics=sem, vmem_limit_bytes=vmem)


def _time_block_order(d, c, n_blocks, n_ctx_blocks):
    rev = jnp.where(c < n_ctx_blocks, n_ctx_blocks - 1 - c, n_blocks - 1 - (c - n_ctx_blocks))
    return jnp.where(d == 0, c, rev)


def _mod_kernel(c_ref, w_ref, b_ref, o_ref):
    o_ref[0] = _dot(_silu(c_ref[...]), w_ref[0], HI) + b_ref[0]


def _modulation(c, c_ctx, ada_w, ada_b):
    depth, d_model, _ = ada_w.shape
    batch = c.shape[0]
    rows = -(-(batch + 1) // 8) * 8
    cc = jnp.concatenate([c, c_ctx[None], jnp.zeros((rows - batch - 1, d_model), F32)], axis=0)
    out = pl.pallas_call(
        _mod_kernel,
        out_shape=jax.ShapeDtypeStruct((depth, rows, N_MOD * d_model), F32),
        grid=(depth, N_MOD),
        in_specs=[
            pl.BlockSpec((rows, d_model), lambda l, j: (0, 0)),
            pl.BlockSpec((1, d_model, d_model), lambda l, j: (l, 0, j)),
            pl.BlockSpec((1, 1, d_model), lambda l, j: (l, 0, j)),
        ],
        out_specs=pl.BlockSpec((1, rows, d_model), lambda l, j: (l, 0, j)),
        compiler_params=_params(("parallel", "parallel")),
        name="adaln_modulation",
    )(cc, ada_w, ada_b.reshape(depth, 1, N_MOD * d_model))
    return out.reshape(depth, rows, N_MOD, d_model)


def _neighbours(h, h_prev, h_next, is_ctx, first_lat, last_lat):
    tb = h.shape[0]
    row = lax.broadcasted_iota(I32, (tb, 1), 0)
    col = jnp.bitwise_and(row, GRID_W - 1)
    left_ok = jnp.where(is_ctx, row, col) != 0
    right_ok = jnp.where(is_ctx, row - (tb - 1), col - (GRID_W - 1)) != 0
    left = jnp.where(left_ok, pltpu.roll(h, 1, axis=0), 0.0)
    right = jnp.where(right_ok, pltpu.roll(h, tb - 1, axis=0), 0.0)
    up = jnp.concatenate([h_prev, h[:tb - GRID_W]], axis=0)
    down = jnp.concatenate([h[GRID_W:], h_next], axis=0)
    up_ok = jnp.where(first_lat, row, tb) >= GRID_W
    down_ok = jnp.where(last_lat, row, 0) < tb - GRID_W
    up = jnp.where(up_ok, up, 0.0)
    down = jnp.where(down_ok, down, 0.0)
    return left, right, up, down


def _chunk_cumsum(x, reverse):
    n = x.shape[0]
    pos = jnp.bitwise_and(lax.broadcasted_iota(I32, (n, 1), 0), RW_CHUNK - 1)
    step = 1
    while step < RW_CHUNK:
        if reverse:
            x = x + jnp.where(pos < RW_CHUNK - step, pltpu.roll(x, n - step, axis=0), 0.0)
        else:
            x = x + jnp.where(pos >= step, pltpu.roll(x, step, axis=0), 0.0)
        step *= 2
    return x


def _segment_allsum(x, seg):
    width = x.shape[-1]
    shift = seg.bit_length() - 1
    assert seg == 1 << shift and MXU_WIDTH % seg == 0 and width % MXU_WIDTH == 0
    r = jnp.right_shift(lax.broadcasted_iota(I32, (MXU_WIDTH, MXU_WIDTH), 0), shift)
    c = jnp.right_shift(lax.broadcasted_iota(I32, (MXU_WIDTH, MXU_WIDTH), 1), shift)
    ones = jnp.where(r == c, 1.0, 0.0).astype(BF16)
    hi = x.astype(BF16)
    rest = x - hi.astype(F32)
    mid = rest.astype(BF16)
    lo = (rest - mid.astype(F32)).astype(BF16)
    groups = []
    for g in range(width // MXU_WIDTH):
        cols = slice(g * MXU_WIDTH, (g + 1) * MXU_WIDTH)
        groups.append(_dot(hi[:, cols], ones) + _dot(mid[:, cols], ones) + _dot(lo[:, cols], ones))
    return jnp.concatenate(groups, axis=-1)


def _route_top_k(logits):
    n, n_exp = logits.shape
    lane = lax.broadcasted_iota(I32, (n, n_exp), 1).astype(F32)
    lane_k = lax.broadcasted_iota(I32, (n, TOP_K), 1)
    work = logits
    vals, idxs = [], []
    for _ in range(TOP_K):
        m = jnp.max(work, axis=-1, keepdims=True)
        idx = jnp.min(jnp.where(work == m, lane, float(n_exp)), axis=-1, keepdims=True)
        vals.append(m)
        idxs.append(idx)
        work = jnp.where(lane == idx, -jnp.inf, work)
    es = [jnp.exp(v - vals[0]) for v in vals]
    den = es[0] + es[1] + es[2] + es[3]
    top_i = jnp.zeros((n, TOP_K), I32)
    gates = jnp.zeros((n, TOP_K), F32)
    for j in range(TOP_K):
        top_i = jnp.where(lane_k == j, idxs[j].astype(I32), top_i)
        gates = jnp.where(lane_k == j, es[j] / den, gates)
    return top_i, gates


def _router_params(moe_wr, moe_br):
    n_exp = moe_wr.shape[-1]
    assert n_exp <= LANES
    wr = jnp.pad(moe_wr, ((0, 0), (0, LANES - n_exp)))
    br = jnp.pad(moe_br, (0, LANES - n_exp), constant_values=-1e30)
    return wr, br.reshape(1, LANES)


def _post_mixer(x, y, mod, g2, wr, br, xo_ref, h2_ref, ti_ref, gt_ref):
    xn = x + mod[2:3] * y
    xo_ref[0] = xn
    h2 = _adaln(xn, g2, mod[3:4], mod[4:5])
    h2_ref[0] = h2
    top_i, gates = _route_top_k(_dot(h2, wr, HI) + br)
    ti_ref[0] = top_i
    gt_ref[0] = gates


def _rwkv_pre_kernel(*refs, has_vres, n_ctx_blocks):
    (x_ref, xp_ref, xn_ref, mod_ref, g_ref, mu_ref, wr_ref, wk_ref, wv_ref, wlw_ref, wla_ref,
     wg1_ref, w2_ref, a2_ref, g2_ref, w0_ref, a0_ref, kkp_ref, kap_ref, rk_ref) = refs[:20]
    if has_vres:
        vf_ref, v0_ref, v1_ref, v2_ref = refs[20:24]
        outs = refs[24:]
    else:
        outs = refs[20:]
    r_ref, v_ref, gg_ref, bo_ref, lp_ref, ap_ref, kd_ref, bb_ref = outs

    t = pl.program_id(1)
    n_blocks = pl.num_programs(1)
    is_ctx = t < n_ctx_blocks
    mod = mod_ref[0]
    g = g_ref[...]
    h = _adaln(x_ref[0], g, mod[0:1], mod[1:2])
    hp = _adaln(xp_ref[0], g, mod[0:1], mod[1:2])
    hn = _adaln(xn_ref[0], g, mod[0:1], mod[1:2])
    left, right, up, down = _neighbours(h, hp, hn, is_ctx, t == n_ctx_blocks, t == n_blocks - 1)
    d_model = h.shape[1]
    q = d_model // 4
    lane = lax.broadcasted_iota(I32, (1, d_model), 1)
    q2 = jnp.where(is_ctx, left, up)
    q3 = jnp.where(is_ctx, right, down)
    shifted = jnp.where(lane < q, left, jnp.where(lane < 2 * q, right, jnp.where(lane < 3 * q, q2, q3)))
    xx = shifted - h
    mu = mu_ref[...]

    def mix(j):
        return (h + xx * mu[j:j + 1]).astype(BF16)

    r = _dot(mix(0), wr_ref[...])
    k = _dot(mix(2), wk_ref[...])
    xv = mix(3)
    v = _dot(xv, wv_ref[...])
    if has_vres:
        lora = _dot(_dot(xv, v1_ref[...]).astype(BF16), v2_ref[...])
        v = v + (vf_ref[0] - v) * _sigmoid(v0_ref[...] + lora)
    gate = _dot(_sigmoid(_dot(mix(5), wg1_ref[...])).astype(BF16), g2_ref[...])
    kx = k * kkp_ref[...]
    kk = kx / jnp.maximum(jnp.sqrt(_segment_allsum(kx * kx, RW_HEAD_DIM)), 1e-12)
    r_ref[0] = r
    v_ref[0] = v
    gg_ref[0] = gate
    tw = jnp.tanh(_dot(mix(1), wlw_ref[...])).astype(BF16)
    la = _dot(mix(4), wla_ref[...]).astype(BF16)
    kap = kap_ref[...]
    kd_sum = None
    for d in range(2):
        logw = -_softplus(-(w0_ref[d:d + 1] + _dot(tw, w2_ref[d]))) - 0.5
        decay_rate = jnp.exp(logw)
        lp_ref[d, 0] = _chunk_cumsum(-decay_rate, reverse=d == 1)
        ap_ref[d, 0] = -kk * jnp.exp(decay_rate)
        a = _sigmoid(a0_ref[d:d + 1] + _dot(la, a2_ref[d]))
        kd = k * (1.0 + (a - 1.0) * kap)
        kd_ref[d, 0] = kd
        bb_ref[d, 0] = kk * a
        kd_sum = kd if kd_sum is None else kd_sum + kd
    bo_ref[0] = _segment_allsum(r * rk_ref[...] * kd_sum, RW_HEAD_DIM) * v


def _rwkv_pre(x, mod_l, norm_g, p, vres, v_first, tb, n_ctx_blocks):
    batch, t_len, d_model = x.shape
    n_blocks = t_len // tb
    hb = tb // GRID_W
    n_halo = t_len // GRID_W
    lora = p["w1"].shape[-1]
    assert 2 * lora == LANES and p["g1"].shape[-1] == LANES
    zeros = jnp.zeros((lora, d_model), F32)
    w2p = jnp.stack([jnp.concatenate([p["w2"][0], zeros]), jnp.concatenate([zeros, p["w2"][1]])])
    a2p = jnp.stack([jnp.concatenate([p["a2"][0], zeros]), jnp.concatenate([zeros, p["a2"][1]])])
    has_vres = vres is not None

    tok = pl.BlockSpec((1, tb, d_model), lambda b, t: (b, t, 0))
    tok2 = pl.BlockSpec((2, 1, tb, d_model), lambda b, t: (0, b, t, 0))
    in_specs = [
        tok,
        pl.BlockSpec((1, GRID_W, d_model), lambda b, t: (b, jnp.maximum(t * hb - 1, 0), 0)),
        pl.BlockSpec((1, GRID_W, d_model), lambda b, t: (b, jnp.minimum((t + 1) * hb, n_halo - 1), 0)),
        pl.BlockSpec((1, N_MOD, d_model), lambda b, t: (jnp.where(t < n_ctx_blocks, batch, b), 0, 0)),
    ]
    args = [x, x, x, mod_l]
    consts = [
        norm_g.reshape(1, d_model), p["mu"],
        p["wr"].astype(BF16), p["wk"].astype(BF16), p["wv"].astype(BF16),
        jnp.concatenate([p["w1"][0], p["w1"][1]], axis=1).astype(BF16),
        jnp.concatenate([p["a1"][0], p["a1"][1]], axis=1).astype(BF16),
        p["g1"].astype(BF16), w2p.astype(BF16), a2p.astype(BF16), p["g2"].astype(BF16),
        p["w0"], p["a0"], p["kk"].reshape(1, d_model), p["ka"].reshape(1, d_model),
        p["rk"].reshape(1, d_model),
    ]
    in_specs += [_const_spec(c.shape) for c in consts]
    args += consts
    if has_vres:
        v0, v1, v2 = vres
        extra = [v0.reshape(1, d_model), v1.astype(BF16), v2.astype(BF16)]
        in_specs += [tok] + [_const_spec(c.shape) for c in extra]
        args += [v_first] + extra
    one = jax.ShapeDtypeStruct((batch, t_len, d_model), F32)
    two = jax.ShapeDtypeStruct((2, batch, t_len, d_model), F32)
    return pl.pallas_call(
        functools.partial(_rwkv_pre_kernel, has_vres=has_vres, n_ctx_blocks=n_ctx_blocks),
        out_shape=[one, one, one, one, two, two, two, two],
        grid=(batch, n_blocks),
        in_specs=in_specs,
        out_specs=[tok, tok, tok, tok, tok2, tok2, tok2, tok2],
        compiler_params=_params(("parallel", "parallel")),
        name="rwkv_features",
    )(*args)


def _rwkv_scan_kernel(r_ref, v_ref, lp_ref, ap_ref, kd_ref, bb_ref, y_ref, s_ref, *, n_pairs, tb):
    d = pl.program_id(0)
    c = pl.program_id(3)
    ch = RW_CHUNK
    two = 2 * ch

    @pl.when(c == 0)
    def _():
        s_ref[...] = jnp.zeros_like(s_ref)

    sgn = jnp.where(d == 0, 1, -1)
    r2 = lax.broadcasted_iota(I32, (two, two), 0)
    c2 = lax.broadcasted_iota(I32, (two, two), 1)
    same = jnp.right_shift(r2, 6) == jnp.right_shift(c2, 6)
    delta = (jnp.bitwise_and(c2, ch - 1) - jnp.bitwise_and(r2, ch - 1)) * sgn
    before = jnp.logical_and(same, delta < 0)
    before_eq = jnp.logical_and(same, delta <= 0)
    head0 = lax.broadcasted_iota(I32, (ch, LANES), 1) < RW_HEAD_DIM
    head0w = lax.broadcasted_iota(I32, (ch, 2 * LANES), 1)
    head0w = jnp.bitwise_and(head0w, LANES - 1) < RW_HEAD_DIM
    n_sub = tb // ch
    pairs = range(n_pairs)

    def stack2(x):
        return jnp.concatenate([x, x], axis=0)

    def pick(x, mask):
        return jnp.where(mask, x[:ch], x[ch:])

    def sub_chunk(j, carry):
        jj = jnp.where(d == 0, j, n_sub - 1 - j)
        start = pl.multiple_of(jj * ch, ch)
        rows = pl.ds(start, ch)
        last = pl.ds(pl.multiple_of(start + jnp.where(d == 0, ch - 8, 0), 8), 8)
        cols = [slice(p * LANES, (p + 1) * LANES) for p in pairs]

        v, vv, r_t, a_t, b_p, k_p, m_b, m_k, p_end = [], [], [], [], [], [], [], [], []
        for p in pairs:
            lp = lp_ref[0, 0, rows, cols[p]]
            e_in = jnp.exp(lp)
            e_out = jnp.exp(-lp)
            lp_edge = lp_ref[0, 0, last, cols[p]]
            p_last = jnp.exp(jnp.where(d == 0, lp_edge[7:8], lp_edge[0:1]))
            p_end.append(p_last)
            at = ap_ref[0, 0, rows, cols[p]] * e_in
            rt = r_ref[0, rows, cols[p]] * e_in
            bt = bb_ref[0, 0, rows, cols[p]] * e_out
            kt = kd_ref[0, 0, rows, cols[p]] * e_out
            vp = v_ref[0, rows, cols[p]]
            lhs = jnp.concatenate([jnp.where(head0, at, 0.0), jnp.where(head0, 0.0, at),
                                   jnp.where(head0, rt, 0.0), jnp.where(head0, 0.0, rt)],
                                  axis=0).astype(BF16)
            m_bk = _dot_nt(lhs, jnp.concatenate([bt, bt, kt, kt], axis=0).astype(BF16))
            m_b.append(m_bk[:, :two])
            m_k.append(m_bk[:, two:])
            v.append(vp)
            vv.append(stack2(vp).astype(BF16))
            r_t.append(rt)
            a_t.append(at)
            b_p.append(bt * p_last)
            k_p.append(kt * p_last)
        n_pow = [jnp.where(before, m_b[p][:two], 0.0).astype(BF16) for p in pairs]
        a_k = [jnp.where(before, m_k[p][:two], 0.0).astype(BF16) for p in pairs]
        r_b = [jnp.where(before_eq, m_b[p][two:], 0.0).astype(BF16) for p in pairs]
        r_k = [jnp.where(before_eq, m_k[p][two:], 0.0).astype(BF16) for p in pairs]
        z = [jnp.concatenate([stack2(a_t[p]), _dot(a_k[p], vv[p])], axis=1) for p in pairs]
        z = [z[p] + _dot(n_pow[p], z[p].astype(BF16)) for p in pairs]
        for _ in range(int(math.log2(ch)) - 1):
            n_pow = [_dot(n_pow[p], n_pow[p]).astype(BF16) for p in pairs]
            z = [z[p] + _dot(n_pow[p], z[p].astype(BF16)) for p in pairs]
        q = [_dot(r_b[p], z[p].astype(BF16)) for p in pairs]
        y_k = [_dot(r_k[p], vv[p]) for p in pairs]
        w_u = [pick(z[p], head0w) for p in pairs]
        r_y = [pick(q[p], head0w) for p in pairs]
        lhs_s = [jnp.concatenate([w_u[p][:, :LANES], r_t[p] + r_y[p][:, :LANES]], axis=0).astype(BF16)
                 for p in pairs]
        u_0 = [w_u[p][:, LANES:] for p in pairs]
        y_0 = [r_y[p][:, LANES:] + pick(y_k[p], head0) for p in pairs]
        bk_p = [jnp.concatenate([b_p[p], k_p[p]], axis=0).astype(BF16) for p in pairs]

        s_old = [s_ref[p] for p in pairs]
        g = [_dot_nt(lhs_s[p], s_old[p].astype(BF16)) for p in pairs]
        u = [g[p][:ch] + u_0[p] for p in pairs]
        for p in pairs:
            y_ref[0, 0, rows, cols[p]] = g[p][ch:] + y_0[p]
        d_s = [_dot_tn(jnp.concatenate([u[p], v[p]], axis=0).astype(BF16), bk_p[p]) for p in pairs]
        for p in pairs:
            s_ref[p] = s_old[p] * p_end[p] + jnp.where(same, d_s[p], 0.0)
        return carry

    lax.fori_loop(0, n_sub, sub_chunk, 0)


def _rwkv_scan(r, v, lp, ap, kd, bb, tb, n_ctx_blocks):
    batch, t_len, d_model = r.shape
    n_blocks = t_len // tb
    lanes = min(8 * LANES, d_model)
    n_pairs = lanes // LANES

    def one_map(d, b, g, c):
        return (b, _time_block_order(d, c, n_blocks, n_ctx_blocks), g)

    def two_map(d, b, g, c):
        return (d, b, _time_block_order(d, c, n_blocks, n_ctx_blocks), g)

    one = pl.BlockSpec((1, tb, lanes), one_map)
    two = pl.BlockSpec((1, 1, tb, lanes), two_map)
    return pl.pallas_call(
        functools.partial(_rwkv_scan_kernel, n_pairs=n_pairs, tb=tb),
        out_shape=jax.ShapeDtypeStruct((2, batch, t_len, d_model), F32),
        grid=(2, batch, d_model // lanes, n_blocks),
        in_specs=[one, one, two, two, two, two],
        out_specs=two,
        scratch_shapes=[pltpu.VMEM((n_pairs, LANES, LANES), F32)],
        compiler_params=_params(("parallel", "parallel", "parallel", "arbitrary")),
        name="rwkv_scan",
    )(r, v, lp, ap, kd, bb)


def _rwkv_out_kernel(y_ref, bo_ref, gg_ref, x_ref, mod_ref, lg_ref, lb_ref,
                     wo_ref, g2_ref, wr_ref, br_ref,
                     xo_ref, h2_ref, ti_ref, gt_ref):
    y = y_ref[0, 0] + y_ref[1, 0]
    inv_n = 1.0 / RW_HEAD_DIM
    yc = y - _segment_allsum(y, RW_HEAD_DIM) * inv_n
    var = _segment_allsum(yc * yc, RW_HEAD_DIM) * inv_n
    yn = yc * lax.rsqrt(var + RW_GN_EPS) * lg_ref[...] + lb_ref[...]
    out = ((yn + bo_ref[0]) * gg_ref[0]).astype(BF16)
    _post_mixer(x_ref[0], _dot(out, wo_ref[...]), mod_ref[0], g2_ref[...], wr_ref[...], br_ref[...],
                xo_ref, h2_ref, ti_ref, gt_ref)


def _post_specs(batch, t_len, d_model, tb):
    tok = pl.BlockSpec((1, tb, d_model), lambda b, t: (b, t, 0))
    nar = pl.BlockSpec((1, tb, TOP_K), lambda b, t: (b, t, 0))
    shapes = [jax.ShapeDtypeStruct((batch, t_len, d_model), F32),
              jax.ShapeDtypeStruct((batch, t_len, d_model), F32),
              jax.ShapeDtypeStruct((batch, t_len, TOP_K), I32),
              jax.ShapeDtypeStruct((batch, t_len, TOP_K), F32)]
    return shapes, [tok, tok, nar, nar]


def _rwkv_out(y, bonus, gate, x, mod_l, p, norm2_g, moe_wr, moe_br, tb, n_ctx_blocks):
    batch, t_len, d_model = x.shape
    tok = pl.BlockSpec((1, tb, d_model), lambda b, t: (b, t, 0))
    tok2 = pl.BlockSpec((2, 1, tb, d_model), lambda b, t: (0, b, t, 0))
    mod_spec = pl.BlockSpec((1, N_MOD, d_model), lambda b, t: (jnp.where(t < n_ctx_blocks, batch, b), 0, 0))
    consts = [p["lnx_g"].reshape(1, d_model), p["lnx_b"].reshape(1, d_model), p["wo"].astype(BF16), norm2_g.reshape(1, d_model), *_router_params(moe_wr, moe_br)]
    shapes, specs = _post_specs(batch, t_len, d_model, tb)
    return pl.pallas_call(
        _rwkv_out_kernel,
        out_shape=shapes,
        grid=(batch, t_len // tb),
        in_specs=[tok2, tok, tok, tok, mod_spec] + [_const_spec(c.shape) for c in consts],
        out_specs=specs,
        compiler_params=_params(("parallel", "parallel")),
        name="rwkv_readout",
    )(y, bonus, gate, x, mod_l, *consts)


def _ml_up_kernel(x_ref, mod_ref, g_ref, w_ref, xm_ref, z_ref):
    mod = mod_ref[0]
    h = _adaln(x_ref[0], g_ref[...], mod[0:1], mod[1:2]).astype(BF16)
    up = _dot(h, w_ref[...])
    inner = xm_ref.shape[-1]
    xm_ref[0] = up[:, :inner]
    z_ref[0] = up[:, inner:]


def _ml_up(x, mod_l, norm_g, w_up, tb, n_ctx_blocks):
    batch, t_len, d_model = x.shape
    inner = w_up.shape[1] // 2
    tok = pl.BlockSpec((1, tb, d_model), lambda b, t: (b, t, 0))
    wide = pl.BlockSpec((1, tb, inner), lambda b, t: (b, t, 0))
    mod_spec = pl.BlockSpec((1, N_MOD, d_model), lambda b, t: (jnp.where(t < n_ctx_blocks, batch, b), 0, 0))
    out = jax.ShapeDtypeStruct((batch, t_len, inner), F32)
    return pl.pallas_call(
        _ml_up_kernel,
        out_shape=[out, out],
        grid=(batch, t_len // tb),
        in_specs=[tok, mod_spec, _const_spec((1, d_model)), _const_spec(w_up.shape)],
        out_specs=[wide, wide],
        compiler_params=_params(("parallel", "parallel")),
        name="mlstm_up",
    )(x, mod_l, norm_g.reshape(1, d_model), w_up.astype(BF16))


def _ml_feat_kernel(xm_ref, xp_ref, xn_ref, cw_ref, cb_ref, wq_ref, wk_ref, wv_ref, wg_ref, bg_ref,
                    q_ref, k_ref, v_ref, xc_ref, gt_ref, gs_ref, gtt_ref, gst_ref, *, n_ctx_blocks):
    t = pl.program_id(1)
    n_blocks = pl.num_programs(1)
    is_ctx = t < n_ctx_blocks
    xm = xm_ref[0]
    tb, inner = xm.shape
    row = lax.broadcasted_iota(I32, (tb, 1), 0)
    col = jnp.bitwise_and(row, GRID_W - 1)
    left_ok = jnp.where(is_ctx, row, col) != 0
    right_ok = jnp.where(is_ctx, row - (tb - 1), col - (GRID_W - 1)) != 0
    up = jnp.concatenate([xp_ref[0], xm[:tb - GRID_W]], axis=0)
    down = jnp.concatenate([xm[GRID_W:], xn_ref[0]], axis=0)
    up_ok = jnp.where(is_ctx, -1, jnp.where(t == n_ctx_blocks, row, tb)) >= GRID_W
    down_ok = jnp.where(is_ctx, tb, jnp.where(t == n_blocks - 1, row, 0)) < tb - GRID_W
    up = jnp.where(up_ok, up, 0.0)
    down = jnp.where(down_ok, down, 0.0)
    cw = cw_ref[...]
    acc = jnp.zeros((tb, inner), F32) + cb_ref[...]
    for dy, src in enumerate((up, xm, down)):
        acc = acc + jnp.where(left_ok, pltpu.roll(src, 1, axis=0), 0.0) * cw[3 * dy:3 * dy + 1]
        acc = acc + src * cw[3 * dy + 1:3 * dy + 2]
        acc = acc + jnp.where(right_ok, pltpu.roll(src, tb - 1, axis=0), 0.0) * cw[3 * dy + 2:3 * dy + 3]
    xc = _silu(acc)
    xc_ref[0] = xc
    xcb = xc.astype(BF16)
    xmb = xm.astype(BF16)
    for gidx in range(inner // LANES):
        cols = slice(gidx * LANES, (gidx + 1) * LANES)
        q_ref[0, :, cols] = _dot(xcb[:, cols], wq_ref[gidx]).astype(BF16)
        k_ref[0, :, cols] = _dot(xcb[:, cols], wk_ref[gidx]).astype(BF16)
        v_ref[0, :, cols] = _dot(xmb[:, cols], wv_ref[gidx]).astype(BF16)
    pre = (_dot(q_ref[0], wg_ref[0]) + _dot(k_ref[0], wg_ref[1]) + _dot(v_ref[0], wg_ref[2]) + bg_ref[...])
    lane = lax.broadcasted_iota(I32, pre.shape, 1)
    gates = jnp.where(lane < 2 * ML_HEADS, pre, -_softplus(-pre))
    gt_ref[0] = gates
    fwd = bwd = gates
    step = 1
    while step < tb:
        fwd = fwd + jnp.where(row >= step, pltpu.roll(fwd, step, axis=0), 0.0)
        bwd = bwd + jnp.where(row < tb - step, pltpu.roll(bwd, tb - step, axis=0), 0.0)
        step *= 2
    backward_lane = jnp.bitwise_and(jnp.right_shift(lane, 2), 1) == 1
    sums = jnp.where(backward_lane, bwd, fwd)
    gs_ref[0] = sums
    n_gate = gtt_ref.shape[1]
    gtt_ref[0] = jnp.transpose(gates)[:n_gate]
    gst_ref[0] = jnp.transpose(sums)[:n_gate]


def _block_diag_lanes(w):
    n_blk, blk, _ = w.shape
    per = LANES // blk
    w = w.reshape(n_blk // per, per, blk, blk)
    eye = jnp.eye(per, dtype=w.dtype)
    return jnp.einsum("gaio,ab->gaibo", w, eye).reshape(n_blk // per, LANES, LANES)


def _ml_feat(xm, p, tb, n_ctx_blocks):
    batch, t_len, inner = xm.shape
    hb = tb // GRID_W
    n_halo = t_len // GRID_W
    wide = pl.BlockSpec((1, tb, inner), lambda b, t: (b, t, 0))
    gate_w = jnp.concatenate([p["wi"][0], p["wi"][1], p["wf"][0], p["wf"][1]], axis=1)
    n_gate = gate_w.shape[1]
    gate_w = jnp.pad(gate_w, ((0, 0), (0, LANES - n_gate))).reshape(3, inner, LANES).astype(BF16)
    gate_b = jnp.pad(jnp.concatenate([p["bi"][0], p["bi"][1], p["bf"][0], p["bf"][1]]),
                     (0, LANES - n_gate)).reshape(1, LANES)
    consts = [p["conv_w"].reshape(9, inner), p["conv_b"].reshape(1, inner),
              _block_diag_lanes(p["wq"]).astype(BF16), _block_diag_lanes(p["wk"]).astype(BF16),
              _block_diag_lanes(p["wv"]).astype(BF16), gate_w, gate_b]
    half = jax.ShapeDtypeStruct((batch, t_len, inner), BF16)
    narrow = jax.ShapeDtypeStruct((batch, t_len, LANES), F32)
    narrow_spec = pl.BlockSpec((1, tb, LANES), lambda b, t: (b, t, 0))
    flat = jax.ShapeDtypeStruct((batch, n_gate, t_len), F32)
    flat_spec = pl.BlockSpec((1, n_gate, tb), lambda b, t: (b, 0, t))
    return pl.pallas_call(
        functools.partial(_ml_feat_kernel, n_ctx_blocks=n_ctx_blocks),
        out_shape=[half, half, half, jax.ShapeDtypeStruct((batch, t_len, inner), F32), narrow, narrow,
                   flat, flat],
        grid=(batch, t_len // tb),
        in_specs=[wide,
                  pl.BlockSpec((1, GRID_W, inner), lambda b, t: (b, jnp.maximum(t * hb - 1, 0), 0)),
                  pl.BlockSpec((1, GRID_W, inner), lambda b, t: (b, jnp.minimum((t + 1) * hb, n_halo - 1), 0)),
                  ] + [_const_spec(c.shape) for c in consts],
        out_specs=[wide, wide, wide, wide, narrow_spec, narrow_spec, flat_spec, flat_spec],
        compiler_params=_params(("parallel", "parallel")),
        name="mlstm_features",
    )(xm, xm, xm, *consts)


def _ml_scan_kernel(q_ref, k_ref, v_ref, gc_ref, sc_ref, gr_ref, sr_ref, h_ref, c_ref, n_ref, m_ref, *, scale):
    d = pl.program_id(0)
    hd = pl.program_id(2)
    c = pl.program_id(3)

    @pl.when(c == 0)
    def _():
        c_ref[...] = jnp.zeros_like(c_ref)
        n_ref[...] = jnp.zeros_like(n_ref)
        m_ref[...] = jnp.zeros_like(m_ref)

    q = q_ref[0]
    k = k_ref[0]
    v = v_ref[0]
    ln = q.shape[0]
    sgn = jnp.where(d == 0, 1, -1)
    ri = lax.broadcasted_iota(I32, (ln, ln), 0)
    ci = lax.broadcasted_iota(I32, (ln, ln), 1)
    allowed = (ci - ri) * sgn <= 0
    gcol = gc_ref[0]
    grow = gr_ref[0]
    i_idx = d * ML_HEADS + hd
    f_idx = 2 * ML_HEADS + i_idx
    lane_c = lax.broadcasted_iota(I32, gcol.shape, 1)
    sub_r = lax.broadcasted_iota(I32, grow.shape, 0)
    b_col = jnp.sum(jnp.where(lane_c == f_idx, sc_ref[0], 0.0), axis=1, keepdims=True)
    f_col = jnp.sum(jnp.where(lane_c == f_idx, gcol, 0.0), axis=1, keepdims=True)
    i_col = jnp.sum(jnp.where(lane_c == i_idx, gcol, 0.0), axis=1, keepdims=True)
    b_row = jnp.sum(jnp.where(sub_r == f_idx, sr_ref[0], 0.0), axis=0, keepdims=True)
    i_row = jnp.sum(jnp.where(sub_r == i_idx, grow, 0.0), axis=0, keepdims=True)
    m_prev = m_ref[0:1, 0:1]
    log_inter = b_col + m_prev
    log_intra = jnp.where(allowed, b_col - b_row + i_row, -jnp.inf)
    m_t = jnp.maximum(log_inter, jnp.max(log_intra, axis=1, keepdims=True))
    w_intra = jnp.exp(log_intra - m_t)
    w_inter = jnp.exp(log_inter - m_t)
    s = _dot_nt(q, k) * scale * w_intra
    ct = c_ref[...]
    nrow = n_ref[0:1, :]
    qf = q.astype(F32)
    num = _dot(s.astype(BF16), v) + w_inter * _dot(q, ct.astype(BF16))
    den = jnp.sum(s, axis=1, keepdims=True) + w_inter * jnp.sum(qf * nrow, axis=1, keepdims=True)
    h_ref[0, 0] = num / jnp.maximum(jnp.abs(den), jnp.exp(-m_t))
    gsum = jnp.sum(f_col, axis=0, keepdims=True)
    log_loc = gsum - b_col + i_col
    m_new = jnp.maximum(gsum + m_prev, jnp.max(log_loc, axis=0, keepdims=True))
    w_loc = jnp.exp(log_loc - m_new)
    w_old = jnp.exp(gsum + m_prev - m_new)
    kw = k.astype(F32) * (scale * w_loc)
    c_ref[...] = w_old * ct + _dot_tn(kw.astype(BF16), v)
    n_ref[0:1, :] = w_old * nrow + jnp.sum(kw, axis=0, keepdims=True)
    m_ref[...] = jnp.zeros_like(m_ref) + m_new


def _ml_scan(q, k, v, gates, gate_sums, gates_t, sums_t, tb, n_ctx_blocks):
    batch, t_len, inner = q.shape
    dh = inner // ML_HEADS
    n_blocks = t_len // tb
    n_gate = gates_t.shape[1]

    def head_map(d, b, h, c):
        return (b, _time_block_order(d, c, n_blocks, n_ctx_blocks), h)

    def col_map(d, b, h, c):
        return (b, _time_block_order(d, c, n_blocks, n_ctx_blocks), 0)

    def row_map(d, b, h, c):
        return (b, 0, _time_block_order(d, c, n_blocks, n_ctx_blocks))

    def out_map(d, b, h, c):
        return (d, b, _time_block_order(d, c, n_blocks, n_ctx_blocks), h)

    head = pl.BlockSpec((1, tb, dh), head_map)
    col = pl.BlockSpec((1, tb, LANES), col_map)
    row = pl.BlockSpec((1, n_gate, tb), row_map)
    return pl.pallas_call(
        functools.partial(_ml_scan_kernel, scale=float(dh) ** -0.5),
        out_shape=jax.ShapeDtypeStruct((2, batch, t_len, inner), F32),
        grid=(2, batch, ML_HEADS, n_blocks),
        in_specs=[head, head, head, col, col, row, row],
        out_specs=pl.BlockSpec((1, 1, tb, dh), out_map),
        scratch_shapes=[pltpu.VMEM((dh, dh), F32), pltpu.VMEM((8, dh), F32), pltpu.VMEM((8, LANES), F32)],
        compiler_params=_params(("parallel", "parallel", "parallel", "arbitrary")),
        name="mlstm_scan",
    )(q, k, v, gates, gate_sums, gates_t, sums_t)


def _ml_out_kernel(hs_ref, xc_ref, z_ref, x_ref, mod_ref, ng_ref, sk_ref, wd_ref, g2_ref, wr_ref, br_ref,
                   xo_ref, h2_ref, ti_ref, gt_ref):
    hs = hs_ref[0, 0] + hs_ref[1, 0]
    inner = hs.shape[1]
    dh = inner // ML_HEADS
    parts = []
    for hd in range(ML_HEADS):
        blk = hs[:, hd * dh:(hd + 1) * dh]
        mean = jnp.mean(blk, axis=1, keepdims=True)
        cen = blk - mean
        var = jnp.mean(cen * cen, axis=1, keepdims=True)
        parts.append(cen * lax.rsqrt(var + ML_LN_EPS))
    hn = jnp.concatenate(parts, axis=1) * ng_ref[...]
    y = ((hn + sk_ref[...] * xc_ref[0]) * _silu(z_ref[0])).astype(BF16)
    _post_mixer(x_ref[0], _dot(y, wd_ref[...]), mod_ref[0], g2_ref[...], wr_ref[...], br_ref[...],
                xo_ref, h2_ref, ti_ref, gt_ref)


def _ml_out(hs, xc, z, x, mod_l, p, norm2_g, moe_wr, moe_br, tb, n_ctx_blocks):
    batch, t_len, d_model = x.shape
    inner = xc.shape[-1]
    tok = pl.BlockSpec((1, tb, d_model), lambda b, t: (b, t, 0))
    wide = pl.BlockSpec((1, tb, inner), lambda b, t: (b, t, 0))
    wide2 = pl.BlockSpec((2, 1, tb, inner), lambda b, t: (0, b, t, 0))
    mod_spec = pl.BlockSpec((1, N_MOD, d_model), lambda b, t: (jnp.where(t < n_ctx_blocks, batch, b), 0, 0))
    consts = [p["norm_g"].reshape(1, inner), p["skip"].reshape(1, inner), p["w_down"].astype(BF16),
              norm2_g.reshape(1, d_model), *_router_params(moe_wr, moe_br)]
    shapes, specs = _post_specs(batch, t_len, d_model, tb)
    return pl.pallas_call(
        _ml_out_kernel,
        out_shape=shapes,
        grid=(batch, t_len // tb),
        in_specs=[wide2, wide, wide, tok, mod_spec] + [_const_spec(c.shape) for c in consts],
        out_specs=specs,
        compiler_params=_params(("parallel", "parallel")),
        name="mlstm_readout",
    )(hs, xc, z, x, mod_l, *consts)


def _moe_plan(top_i, tok_rows, n_rows, n_exp, bm):
    n_asg = top_i.size
    assert n_rows < (1 << 16) and MOE_OUT_BUFFERS * bm <= n_rows
    flat_e = top_i.reshape(n_asg)
    onehot = (flat_e[:, None] == jnp.arange(n_exp, dtype=I32)[None, :]).astype(I32)
    csum = jnp.cumsum(onehot, axis=0)
    counts = csum[-1]
    padded = (counts + bm - 1) // bm * bm
    pad_end = jnp.cumsum(padded)
    pad_start = pad_end - padded
    dest = jnp.sum(onehot * (csum - 1 + pad_start[None, :]), axis=1)
    n_blk = n_asg // bm + n_exp
    choice = jnp.arange(TOP_K, dtype=I32)
    real = jnp.bitwise_or(tok_rows[:, None], jnp.left_shift(choice, 16)[None, :]).reshape(n_asg)
    slots = jnp.arange((n_blk + MOE_LEAD_BLOCKS + 1) * bm, dtype=I32)
    buf = (slots // bm - MOE_LEAD_BLOCKS) % MOE_OUT_BUFFERS
    pad = jnp.bitwise_or(buf * bm + slots % bm, TOP_K << 16)
    table = pad.at[MOE_LEAD_BLOCKS * bm + dest].set(real)
    starts = jnp.arange(n_blk, dtype=I32) * bm
    block_e = jnp.minimum(jnp.sum((pad_end[None, :] <= starts[:, None]).astype(I32), axis=1), n_exp - 1)
    return block_e, table


def _moe_expert_kernel(be_ref, src_0, src_p1, dst_m2, dst_m1, dst_0, h_hbm, wgu_ref, bgu_ref, wd_ref, bd_ref,
                       y_hbm, xbuf, obuf, wgu_bf, wd_bf, gsem, ssem):
    i = pl.program_id(0)
    n_blk = pl.num_programs(0)
    n_groups = xbuf.shape[1]
    x_cur = lax.rem(i, 2)
    x_next = 1 - x_cur
    o_cur = lax.rem(i, MOE_OUT_BUFFERS)
    o_prev = lax.rem(i + MOE_OUT_BUFFERS - 1, MOE_OUT_BUFFERS)
    o_prev2 = lax.rem(i + MOE_OUT_BUFFERS - 2, MOE_OUT_BUFFERS)
    src_tables = {0: src_0, 1: src_p1}
    dst_tables = {-2: dst_m2, -1: dst_m1, 0: dst_0}

    def gather_copy(rel, buf, g, k):
        row = src_tables[rel][0, 0, g * SUBLANES + k]
        src = h_hbm.at[jnp.right_shift(row, SUBLANE_SHIFT), pl.ds(jnp.bitwise_and(row, SUBLANES - 1), 1)]
        return pltpu.make_async_copy(src, xbuf.at[buf, g, pl.ds(k, 1)], gsem.at[buf])

    def scatter_copy(rel, buf, g, k):
        row = dst_tables[rel][0, 0, g * SUBLANES + k]
        dst = y_hbm.at[jnp.right_shift(row, SUBLANE_SHIFT), pl.ds(jnp.bitwise_and(row, SUBLANES - 1), 1)]
        return pltpu.make_async_copy(obuf.at[buf, g, pl.ds(k, 1)], dst, ssem.at[buf])

    def for_rows(fn):
        def body(g, carry):
            for k in range(SUBLANES):
                fn(g, k)
            return carry
        lax.fori_loop(0, n_groups, body, 0)

    @pl.when(i == 0)
    def _():
        obuf[...] = jnp.zeros_like(obuf)
        for_rows(lambda g, k: gather_copy(0, x_cur, g, k).start())
        for_rows(lambda g, k: scatter_copy(-2, o_prev2, g, k).start())

    changed = jnp.logical_or(i == 0, be_ref[i] != be_ref[jnp.maximum(i - 1, 0)])

    @pl.when(changed)
    def _():
        wgu_bf[...] = wgu_ref[0, 0].astype(BF16)
        wd_bf[...] = wd_ref[0, 0].astype(BF16)

    for_rows(lambda g, k: gather_copy(1, x_next, g, k).start(priority=k % 2))
    for_rows(lambda g, k: scatter_copy(-1, o_prev, g, k).start(priority=k % 2))
    for_rows(lambda g, k: scatter_copy(-2, o_prev2, g, k).wait())
    for_rows(lambda g, k: gather_copy(0, x_cur, g, k).wait())

    d_model = xbuf.shape[-1]
    x = xbuf[x_cur].reshape(n_groups * SUBLANES, d_model).astype(BF16)
    gu = _dot(x, wgu_bf[...]) + bgu_ref[0, 0]
    ff = gu.shape[1] // 2
    gate = jnp.minimum(gu[:, :ff], SWIGLU_LIMIT)
    up = jnp.clip(gu[:, ff:], -SWIGLU_LIMIT, SWIGLU_LIMIT)
    act = gate * _sigmoid(SWIGLU_ALPHA * gate) * (up + 1.0)
    out = _dot(act.astype(BF16), wd_bf[...]) + bd_ref[0, 0]
    obuf[o_cur] = out.reshape(n_groups, SUBLANES, d_model)

    @pl.when(i == n_blk - 1)
    def _():
        for_rows(lambda g, k: scatter_copy(0, o_cur, g, k).start())
        for_rows(lambda g, k: scatter_copy(-1, o_prev, g, k).wait())
        for_rows(lambda g, k: scatter_copy(0, o_cur, g, k).wait())
        for_rows(lambda g, k: gather_copy(1, x_next, g, k).wait())


def _moe_experts(h2, top_i, tok_rows, layer, w_gu, b_gu, w_d, b_d):
    n_tok, d_model = h2.shape
    depth, n_exp, _, ff2 = w_gu.shape
    bm = MOE_BLOCK
    ff = ff2 // 2
    n_out = TOP_K * n_tok + MOE_OUT_BUFFERS * bm
    assert top_i.size % bm == 0 and bm % SUBLANES == 0 and n_tok % SUBLANES == 0 and n_out % SUBLANES == 0
    block_e, table = _moe_plan(top_i, tok_rows, n_tok, n_exp, bm)
    n_blk = block_e.shape[0]
    table = table.reshape(n_blk + MOE_LEAD_BLOCKS + 1, 1, bm)
    src_rows = jnp.bitwise_and(table, 0xFFFF)
    dst_rows = jnp.right_shift(table, 16) * n_tok + src_rows

    def table_spec(rel):
        return pl.BlockSpec((1, 1, bm), lambda i, be: (i + MOE_LEAD_BLOCKS + rel, 0, 0),
                            memory_space=pltpu.SMEM)

    grid_spec = pltpu.PrefetchScalarGridSpec(
        num_scalar_prefetch=1,
        grid=(n_blk,),
        in_specs=[
            table_spec(0), table_spec(1), table_spec(-2), table_spec(-1), table_spec(0),
            pl.BlockSpec(memory_space=pl.ANY),
            pl.BlockSpec((1, 1, d_model, ff2), lambda i, be: (layer, be[i], 0, 0)),
            pl.BlockSpec((1, 1, 1, ff2), lambda i, be: (layer, be[i], 0, 0)),
            pl.BlockSpec((1, 1, ff, d_model), lambda i, be: (layer, be[i], 0, 0)),
            pl.BlockSpec((1, 1, 1, d_model), lambda i, be: (layer, be[i], 0, 0)),
        ],
        out_specs=pl.BlockSpec(memory_space=pl.ANY),
        scratch_shapes=[
            pltpu.VMEM((2, bm // SUBLANES, SUBLANES, d_model), F32),
            pltpu.VMEM((MOE_OUT_BUFFERS, bm // SUBLANES, SUBLANES, d_model), F32),
            pltpu.VMEM((d_model, ff2), BF16),
            pltpu.VMEM((ff, d_model), BF16),
            pltpu.SemaphoreType.DMA((2,)),
            pltpu.SemaphoreType.DMA((MOE_OUT_BUFFERS,)),
        ],
    )
    y4 = pl.pallas_call(
        _moe_expert_kernel,
        out_shape=jax.ShapeDtypeStruct((n_out // SUBLANES, SUBLANES, d_model), F32),
        grid_spec=grid_spec,
        compiler_params=_params(("arbitrary",)),
        name="moe_experts",
    )(block_e, src_rows, src_rows, dst_rows, dst_rows, dst_rows,
      h2.reshape(n_tok // SUBLANES, SUBLANES, d_model), w_gu, b_gu.reshape(depth, n_exp, 1, ff2), w_d,
      b_d.reshape(depth, n_exp, 1, d_model))
    return y4.reshape(n_out, d_model)


def _moe_combine_kernel(x_ref, gt_ref, mod_ref, fg_ref, *rest, final):
    y_refs, o_ref = rest[:TOP_K], rest[TOP_K]
    gates = gt_ref[0]
    f = jnp.zeros(x_ref.shape[1:], F32)
    for j in range(TOP_K):
        f = f + gates[:, j:j + 1] * y_refs[j][...]
    xn = x_ref[0] + mod_ref[0][5:6] * f
    if final:
        xn = xn * lax.rsqrt(jnp.mean(xn * xn, axis=-1, keepdims=True) + NORM_EPS) * fg_ref[...]
    o_ref[0] = xn


def _moe_combine(x, y4, gates, mod_l, final_g, tb, n_ctx_blocks, final):
    batch, t_len, d_model = x.shape
    off = n_ctx_blocks if final else 0
    t_out = t_len - off * tb
    per_b = t_len // tb
    per_j = batch * per_b
    mod_spec = pl.BlockSpec((1, N_MOD, d_model),
                            lambda b, t: (jnp.where(t + off < n_ctx_blocks, batch, b), 0, 0))

    def choice_spec(j):
        return pl.BlockSpec((tb, d_model), lambda b, t: (j * per_j + b * per_b + t + off, 0))

    return pl.pallas_call(
        functools.partial(_moe_combine_kernel, final=final),
        out_shape=jax.ShapeDtypeStruct((batch, t_out, d_model), F32),
        grid=(batch, t_out // tb),
        in_specs=[pl.BlockSpec((1, tb, d_model), lambda b, t: (b, t + off, 0)),
                  pl.BlockSpec((1, tb, TOP_K), lambda b, t: (b, t + off, 0)),
                  mod_spec, _const_spec((1, d_model))] + [choice_spec(j) for j in range(TOP_K)],
        out_specs=pl.BlockSpec((1, tb, d_model), lambda b, t: (b, t, 0)),
        compiler_params=_params(("parallel", "parallel")),
        name="moe_combine",
    )(x, gates, mod_l, final_g.reshape(1, d_model), *([y4] * TOP_K))


def _moe_layer(x, h2, top_i, gates, mod_l, layer, moe_p, final_g, tb, n_ctx_blocks, last):
    batch, t_len, d_model = x.shape
    tok_rows = jnp.arange(batch * t_len, dtype=I32).reshape(batch, t_len)
    if last:
        n_ctx = n_ctx_blocks * tb
        top_i, tok_rows = top_i[:, n_ctx:], tok_rows[:, n_ctx:]
    y4 = _moe_experts(h2.reshape(batch * t_len, d_model), top_i.reshape(-1, TOP_K), tok_rows.reshape(-1),
                      layer, *moe_p)
    return _moe_combine(x, y4, gates, mod_l, final_g, tb, n_ctx_blocks, last)


def kernel(x, c, ctx, c_ctx, ada_w, ada_b, norm1_g, norm2_g, final_g, rw_mu, rw_wr, rw_wk, rw_wv, rw_wo, rw_w0, rw_w1, rw_w2, rw_a0, rw_a1, rw_a2, rw_g1, rw_g2, rw_kk, rw_ka, rw_rk, rw_lnx_g, rw_lnx_b, rw_v0, rw_v1, rw_v2, ml_w_up, ml_conv_w, ml_conv_b, ml_wq, ml_wk, ml_wv, ml_wi, ml_bi, ml_wf, ml_bf, ml_norm_g, ml_skip, ml_w_down, moe_wr, moe_br, moe_w_gu, moe_b_gu, moe_w_d, moe_b_d):
    batch, seq, d_model = x.shape
    n_ctx = ctx.shape[1]
    depth = ada_w.shape[0]
    tb = math.gcd(MAX_TIME_BLOCK, n_ctx, seq)
    assert tb % GRID_W == 0 and n_ctx == tb and d_model % (2 * RW_HEAD_DIM) == 0
    n_ctx_blocks = n_ctx // tb

    mod = _modulation(c, c_ctx, ada_w, ada_b)
    xs = jnp.concatenate([ctx, x], axis=1)
    v_first = None
    for i in range(depth):
        last = i == depth - 1
        j = i // 2
        mod_l = mod[i]
        if i % 2 == 0:
            p = dict(mu=rw_mu[j], wr=rw_wr[j], wk=rw_wk[j], wv=rw_wv[j], wo=rw_wo[j], w0=rw_w0[j],
                     w1=rw_w1[j], w2=rw_w2[j], a0=rw_a0[j], a1=rw_a1[j], a2=rw_a2[j], g1=rw_g1[j],
                     g2=rw_g2[j], kk=rw_kk[j], ka=rw_ka[j], rk=rw_rk[j], lnx_g=rw_lnx_g[j],
                     lnx_b=rw_lnx_b[j])
            vres = None if j == 0 else (rw_v0[j - 1], rw_v1[j - 1], rw_v2[j - 1])
            r, v, gate, bonus, lp, ap, kd, bb = _rwkv_pre(xs, mod_l, norm1_g[i], p, vres, v_first,
                                                          tb, n_ctx_blocks)
            if j == 0:
                v_first = v
            y = _rwkv_scan(r, v, lp, ap, kd, bb, tb, n_ctx_blocks)
            xs, h2, top_i, gates = _rwkv_out(y, bonus, gate, xs, mod_l, p, norm2_g[i],
                                             moe_wr[i], moe_br[i], tb, n_ctx_blocks)
        else:
            p = dict(conv_w=ml_conv_w[j], conv_b=ml_conv_b[j], wq=ml_wq[j], wk=ml_wk[j], wv=ml_wv[j],
                     wi=ml_wi[j], bi=ml_bi[j], wf=ml_wf[j], bf=ml_bf[j], norm_g=ml_norm_g[j],
                     skip=ml_skip[j], w_down=ml_w_down[j])
            xm, z = _ml_up(xs, mod_l, norm1_g[i], ml_w_up[j], tb, n_ctx_blocks)
            q, k, v, xc, gts, gsums, gts_t, gsums_t = _ml_feat(xm, p, tb, n_ctx_blocks)
            hs = _ml_scan(q, k, v, gts, gsums, gts_t, gsums_t, tb, n_ctx_blocks)
            xs, h2, top_i, gates = _ml_out(hs, xc, z, xs, mod_l, p, norm2_g[i],
                                           moe_wr[i], moe_br[i], tb, n_ctx_blocks)
        moe_p = (moe_w_gu, moe_b_gu, moe_w_d, moe_b_d)
        xs = _moe_layer(xs, h2, top_i, gates, mod_l, i, moe_p, final_g, tb, n_ctx_blocks, last)
    return xs
```
